```python
import math
import jax, jax.numpy as jnp
from jax import lax
import numpy as np

D_MODEL = 1024
BATCH = 2
SEQ = 8192
DEPTH = 4
DEC_BATCH = 128
DEC_SEQ = 4
PAST_LEN = 8192
PAGE_SIZE = 128

N_MIXERS = 3
N_MLA = (DEPTH + 2) // 3
N_RWKV = (DEPTH + 1) // 3
N_S5 = DEPTH // 3
N_META = 16
D_FF = 2816
ALPHA = (2 * DEPTH) ** 0.25
BETA = (8 * DEPTH) ** -0.25
LN_EPS = 1e-5
RMS_EPS = 1e-6
MLA_HEADS = 16
Q_RANK = 768
KV_RANK = 256
NOPE_DIM = 64
ROPE_DIM = 32
QK_DIM = NOPE_DIM + ROPE_DIM
V_DIM = 64
ROPE_THETA = 10000.0
Q_BLOCK = 128
RW_HEAD = 64
RW_HEADS = D_MODEL // RW_HEAD
DECAY_LORA = 64
AAA_LORA = 64
GATE_LORA = 128
GN_EPS = 64e-5
S5_GROUP = 16
S5_GROUPS = D_MODEL // S5_GROUP
S5_STATE = 64

kernel_name = 'hybrid_mla_rwkv7_s5_macaron_step'


def layer_norm(x, g, b):
    xf = x.astype(jnp.float32)
    mu = jnp.mean(xf, -1, keepdims=True)
    var = jnp.mean(jnp.square(xf - mu), -1, keepdims=True)
    return ((xf - mu) * lax.rsqrt(var + LN_EPS) * g + b).astype(x.dtype)


def rms_norm(x, g):
    xf = x.astype(jnp.float32)
    return (xf * lax.rsqrt(jnp.mean(jnp.square(xf), -1, keepdims=True) + RMS_EPS) * g).astype(x.dtype)


def swiglu(x, w1, w3, w2):
    return (jax.nn.silu(x @ w1) * (x @ w3)) @ w2


def rope(x, pos):
    half = ROPE_DIM // 2
    inv = 1.0 / (ROPE_THETA ** (jnp.arange(half, dtype=jnp.float32) * (2.0 / ROPE_DIM)))
    ang = pos.astype(jnp.float32)[:, None] * inv
    ang = ang.reshape(ang.shape[:1] + (1,) * (x.ndim - 3) + (half,))
    c, s = jnp.cos(ang), jnp.sin(ang)
    xf = x.astype(jnp.float32)
    x1, x2 = xf[..., :half], xf[..., half:]
    return jnp.concatenate([x1 * c - x2 * s, x1 * s + x2 * c], -1).astype(x.dtype)


def causal_block_attention(q, k, v, n_lead, scale):
    B, L, H, dk = q.shape
    kpos = jnp.arange(L)

    def attend(qb, qpos):
        s = jnp.einsum('bqhd,bkhd->bhqk', qb, k).astype(jnp.float32) * scale
        s = jnp.where(kpos[None, :] <= qpos[:, None], s, -jnp.inf)
        p = jax.nn.softmax(s, axis=-1).astype(v.dtype)
        return jnp.einsum('bhqk,bkhv->bqhv', p, v)

    lead = attend(q[:, :n_lead], jnp.arange(n_lead))
    nb = (L - n_lead) // Q_BLOCK
    qr = q[:, n_lead:].reshape(B, nb, Q_BLOCK, H, dk).transpose(1, 0, 2, 3, 4)
    qpos = n_lead + jnp.arange(nb * Q_BLOCK).reshape(nb, Q_BLOCK)
    rest = lax.map(lambda a: attend(a[0], a[1]), (qr, qpos))
    rest = rest.transpose(1, 0, 2, 3, 4).reshape(B, nb * Q_BLOCK, H, v.shape[-1])
    return jnp.concatenate([lead, rest], axis=1)


def mla_mixer(x, pos, m, paged, p):
    B, T, _ = x.shape
    cq = rms_norm(x @ p['mla_w_dq'][m], p['mla_q_norm'][m])
    q = jnp.einsum('btq,qhd->bthd', cq, p['mla_w_uq'][m])
    q_nope, q_pe = q[..., :NOPE_DIM], rope(q[..., NOPE_DIM:], pos)
    ckv = x @ p['mla_w_dkv'][m]
    c = rms_norm(ckv[..., :KV_RANK], p['mla_kv_norm'][m])
    k_pe = rope(ckv[..., KV_RANK:], pos)
    w_uk, w_uv = p['mla_w_uk'][m], p['mla_w_uv'][m]
    scale = QK_DIM ** -0.5
    if paged is None:
        k_nope = jnp.einsum('btr,rhn->bthn', c, w_uk)
        v = jnp.einsum('btr,rhv->bthv', c, w_uv)
        qf = jnp.concatenate([q_nope, q_pe], -1)
        kf = jnp.concatenate([k_nope, jnp.broadcast_to(k_pe[:, :, None, :], (B, T, MLA_HEADS, ROPE_DIM))], -1)
        o = causal_block_attention(qf, kf, v, N_META, scale)
    else:
        lat_pool, kr_pool, page_table = paged
        past = page_table.shape[1] * PAGE_SIZE
        c_all = jnp.concatenate([lat_pool[m, page_table].reshape(B, past, KV_RANK).astype(c.dtype), c], 1)
        kr_all = jnp.concatenate([kr_pool[m, page_table].reshape(B, past, ROPE_DIM).astype(k_pe.dtype), k_pe], 1)
        q_lat = jnp.einsum('bthn,rhn->bthr', q_nope, w_uk)
        s = (jnp.einsum('bthr,bsr->bhts', q_lat, c_all)
             + jnp.einsum('bthd,bsd->bhts', q_pe, kr_all)).astype(jnp.float32) * scale
        kpos = jnp.arange(past + T)
        qpos = past + jnp.arange(T)
        s = jnp.where(kpos[None, :] <= qpos[:, None], s, -jnp.inf)
        pr = jax.nn.softmax(s, axis=-1).astype(c_all.dtype)
        o_lat = jnp.einsum('bhts,bsr->bthr', pr, c_all)
        o = jnp.einsum('bthr,rhv->bthv', o_lat, w_uv)
    return jnp.einsum('bthv,hvd->btd', o, p['mla_w_o'][m]), c, k_pe


def rwkv7_mixer(x, shift0, wkv0, m, p):
    B, T, D = x.shape
    f32 = jnp.float32
    xf = x.astype(f32)
    xprev = jnp.concatenate([shift0.astype(f32)[:, None], xf[:, :-1]], axis=1)
    xx = xprev - xf
    mu = p['rw_mu'][m]
    xr, xw, xk, xv, xa, xg = (xf + xx * mu[j] for j in range(6))
    r = xr @ p['rw_wr'][m]
    k = xk @ p['rw_wk'][m]
    v = xv @ p['rw_wv'][m]
    w = -jax.nn.softplus(-(p['rw_w0'][m] + jnp.tanh(xw @ p['rw_w1'][m]) @ p['rw_w2'][m])) - 0.5
    decay = jnp.exp(-jnp.exp(w.astype(f32)))
    a = jax.nn.sigmoid(p['rw_a0'][m] + (xa @ p['rw_a1'][m]) @ p['rw_a2'][m])
    g = jax.nn.sigmoid(xg @ p['rw_g1'][m]) @ p['rw_g2'][m]
    heads = lambda z: z.astype(f32).reshape(B, T, RW_HEADS, RW_HEAD)
    kk = heads(k * p['rw_k_k'][m])
    kk = kk * lax.rsqrt(jnp.maximum(jnp.sum(kk * kk, -1, keepdims=True), 1e-24))
    k = k * (1.0 + (a - 1.0) * p['rw_k_a'][m])
    r_h, k_h, v_h, w_h, a_h = heads(r), heads(k), heads(v), heads(decay), heads(a)

    def step(S, inp):
        r_t, w_t, k_t, v_t, kk_t, a_t = inp
        sa = jnp.einsum('bhvk,bhk->bhv', S, -kk_t)
        S = (S * w_t[:, :, None, :] + sa[..., None] * (kk_t * a_t)[:, :, None, :]
             + v_t[..., None] * k_t[:, :, None, :])
        return S, jnp.einsum('bhvk,bhk->bhv', S, r_t)

    tm = lambda z: jnp.swapaxes(z, 0, 1)
    wkv_T, ys = lax.scan(step, wkv0.astype(f32), tuple(tm(z) for z in (r_h, w_h, k_h, v_h, kk, a_h)))
    y = tm(ys)
    mean = jnp.mean(y, -1, keepdims=True)
    var = jnp.mean(jnp.square(y - mean), -1, keepdims=True)
    y = ((y - mean) * lax.rsqrt(var + GN_EPS)).reshape(B, T, D) * p['rw_lnx_g'][m] + p['rw_lnx_b'][m]
    bonus = jnp.sum(r_h * k_h * p['rw_r_k'][m], -1, keepdims=True) * v_h
    y = y + bonus.reshape(B, T, D)
    out = (y * g) @ p['rw_wo'][m]
    return out.astype(x.dtype), wkv_T, x[:, -1]


def cmul(ar, ai, br, bi):
    return ar * br - ai * bi, ar * bi + ai * br


def s5_combine(e1, e2):
    a1r, a1i, b1r, b1i = e1
    a2r, a2i, b2r, b2i = e2
    ar, ai = cmul(a2r, a2i, a1r, a1i)
    br, bi = cmul(a2r, a2i, b1r, b1i)
    return (ar, ai, br + b2r, bi + b2i)


def s5_mixer(x, h0_re, h0_im, m, p):
    B, T, D = x.shape
    f32 = jnp.float32
    u = x.astype(f32).reshape(B, T, S5_GROUPS, S5_GROUP)
    lre = p['s5_lam_re'][m].astype(f32)
    lim = p['s5_lam_im'][m].astype(f32)
    dt = jnp.exp(p['s5_log_dt'][m].astype(f32))[:, None]
    mag = jnp.exp(lre * dt)
    ab_re, ab_im = mag * jnp.cos(lim * dt), mag * jnp.sin(lim * dt)
    den = lre * lre + lim * lim
    nr = ab_re - 1.0
    co_re = (nr * lre + ab_im * lim) / den
    co_im = (ab_im * lre - nr * lim) / den
    bb_re, bb_im = cmul(co_re[..., None], co_im[..., None], p['s5_b_re'][m], p['s5_b_im'][m])
    bu_re = jnp.einsum('btgs,gps->btgp', u, bb_re)
    bu_im = jnp.einsum('btgs,gps->btgp', u, bb_im)
    a_re = jnp.broadcast_to(ab_re, (1, T) + ab_re.shape)
    a_im = jnp.broadcast_to(ab_im, (1, T) + ab_im.shape)
    acr, aci, bcr, bci = lax.associative_scan(s5_combine, (a_re, a_im, bu_re, bu_im), axis=1)
    hr, hi = cmul(acr, aci, h0_re.astype(f32)[:, None], h0_im.astype(f32)[:, None])
    hr, hi = hr + bcr, hi + bci
    y = (jnp.einsum('btgp,gsp->btgs', hr, p['s5_c_re'][m]) - jnp.einsum('btgp,gsp->btgs', hi, p['s5_c_im'][m])
         + p['s5_d'][m].reshape(S5_GROUPS, S5_GROUP) * u).reshape(B, T, D)
    z = jax.nn.gelu(y)
    out = (z @ p['s5_wv'][m]) * jax.nn.sigmoid(z @ p['s5_wg'][m])
    return out.astype(x.dtype), hr[:, -1], hi[:, -1]


def run_group(x, start, paged, rw_wkv0, rw_shift0, s5_re0, s5_im0, p):
    T = x.shape[1]
    pos = start + jnp.arange(T)
    lat_rows, kr_rows, wkv_out, shift_out, re_out, im_out = [], [], [], [], [], []
    for i in range(DEPTH):
        kind, m = i % N_MIXERS, i // N_MIXERS
        x = layer_norm(ALPHA * x + 0.5 * swiglu(x, p['ffn_w1'][i, 0], p['ffn_w3'][i, 0], p['ffn_w2'][i, 0]),
                       p['ln_g'][i, 0], p['ln_b'][i, 0])
        if kind == 0:
            h, c, k_pe = mla_mixer(x, pos, m, paged, p)
            lat_rows.append(c)
            kr_rows.append(k_pe)
        elif kind == 1:
            h, wkv, sh = rwkv7_mixer(x, rw_shift0[m], rw_wkv0[m], m, p)
            wkv_out.append(wkv)
            shift_out.append(sh)
        else:
            h, sr, si = s5_mixer(x, s5_re0[m], s5_im0[m], m, p)
            re_out.append(sr)
            im_out.append(si)
        x = layer_norm(ALPHA * x + h, p['ln_g'][i, 1], p['ln_b'][i, 1])
        x = layer_norm(ALPHA * x + 0.5 * swiglu(x, p['ffn_w1'][i, 1], p['ffn_w3'][i, 1], p['ffn_w2'][i, 1]),
                       p['ln_g'][i, 2], p['ln_b'][i, 2])
    return (x, jnp.stack(lat_rows), jnp.stack(kr_rows), jnp.stack(wkv_out), jnp.stack(shift_out),
            jnp.stack(re_out), jnp.stack(im_out))


def setup_inputs(seed: int = 0) -> dict:
    key = jax.random.key(seed)
    keys = iter(jax.random.split(key, 96))

    def nrm(shape, scale=1.0):
        return jax.random.normal(next(keys), shape, jnp.float32) * scale

    def unif(shape, lo, hi):
        return jax.random.uniform(next(keys), shape, jnp.float32, lo, hi)

    def gain(shape):
        return 1.0 + nrm(shape, 0.02)

    D = D_MODEL
    n_pages = PAST_LEN // PAGE_SIZE
    n_pool = (DEC_BATCH * n_pages * 5) // 4
    page_table = jax.random.permutation(next(keys), n_pool)[: DEC_BATCH * n_pages]
    page_table = page_table.reshape(DEC_BATCH, n_pages).astype(jnp.int32)
    lam_im0 = jnp.pi * jnp.arange(S5_STATE, dtype=jnp.float32)
    return {
        'x_prompt': nrm((BATCH, SEQ, D)),
        'x_sample': nrm((DEC_BATCH, DEC_SEQ, D)),
        'cache_mla_latent': nrm((N_MLA, n_pool, PAGE_SIZE, KV_RANK)),
        'cache_mla_krope': nrm((N_MLA, n_pool, PAGE_SIZE, ROPE_DIM)),
        'state_rwkv_wkv': nrm((N_RWKV, DEC_BATCH, RW_HEADS, RW_HEAD, RW_HEAD), 0.5),
        'state_rwkv_shift': nrm((N_RWKV, DEC_BATCH, D)),
        'state_s5_re': nrm((N_S5, DEC_BATCH, S5_GROUPS, S5_STATE), 0.1),
        'state_s5_im': nrm((N_S5, DEC_BATCH, S5_GROUPS, S5_STATE), 0.1),
        'page_table': page_table,
        'meta_tokens': nrm((N_META, D)),
        'ln_g': gain((DEPTH, 3, D)),
        'ln_b': nrm((DEPTH, 3, D), 0.02),
        'ffn_w1': nrm((DEPTH, 2, D, D_FF), D ** -0.5),
        'ffn_w3': nrm((DEPTH, 2, D, D_FF), D ** -0.5),
        'ffn_w2': nrm((DEPTH, 2, D_FF, D), BETA * D_FF ** -0.5),
        'mla_w_dq': nrm((N_MLA, D, Q_RANK), D ** -0.5),
        'mla_q_norm': gain((N_MLA, Q_RANK)),
        'mla_w_uq': nrm((N_MLA, Q_RANK, MLA_HEADS, QK_DIM), Q_RANK ** -0.5),
        'mla_w_dkv': nrm((N_MLA, D, KV_RANK + ROPE_DIM), D ** -0.5),
        'mla_kv_norm': gain((N_MLA, KV_RANK)),
        'mla_w_uk': nrm((N_MLA, KV_RANK, MLA_HEADS, NOPE_DIM), KV_RANK ** -0.5),
        'mla_w_uv': nrm((N_MLA, KV_RANK, MLA_HEADS, V_DIM), KV_RANK ** -0.5),
        'mla_w_o': nrm((N_MLA, MLA_HEADS, V_DIM, D), BETA * (MLA_HEADS * V_DIM) ** -0.5),
        'rw_mu': unif((N_RWKV, 6, D), 0.0, 1.0),
        'rw_wr': nrm((N_RWKV, D, D), D ** -0.5),
        'rw_wk': nrm((N_RWKV, D, D), D ** -0.5),
        'rw_wv': nrm((N_RWKV, D, D), D ** -0.5),
        'rw_w0': unif((N_RWKV, D), -6.5, -1.5),
        'rw_w1': nrm((N_RWKV, D, DECAY_LORA), D ** -0.5),
        'rw_w2': nrm((N_RWKV, DECAY_LORA, D), 0.1 * DECAY_LORA ** -0.5),
        'rw_a0': nrm((N_RWKV, D), 0.1),
        'rw_a1': nrm((N_RWKV, D, AAA_LORA), D ** -0.5),
        'rw_a2': nrm((N_RWKV, AAA_LORA, D), 0.1 * AAA_LORA ** -0.5),
        'rw_g1': nrm((N_RWKV, D, GATE_LORA), D ** -0.5),
        'rw_g2': nrm((N_RWKV, GATE_LORA, D), GATE_LORA ** -0.5),
        'rw_k_k': 0.85 + nrm((N_RWKV, D), 0.02),
        'rw_k_a': 1.0 + nrm((N_RWKV, D), 0.02),
        'rw_r_k': nrm((N_RWKV, RW_HEADS, RW_HEAD), 0.1),
        'rw_lnx_g': gain((N_RWKV, D)),
        'rw_lnx_b': nrm((N_RWKV, D), 0.02),
        'rw_wo': nrm((N_RWKV, D, D), BETA * D ** -0.5),
        's5_lam_re': -0.5 + nrm((N_S5, S5_GROUPS, S5_STATE), 0.01),
        's5_lam_im': lam_im0 + nrm((N_S5, S5_GROUPS, S5_STATE), 0.01),
        's5_log_dt': unif((N_S5, S5_GROUPS), math.log(1e-3), math.log(1e-1)),
        's5_b_re': nrm((N_S5, S5_GROUPS, S5_STATE, S5_GROUP), (2 * S5_GROUP) ** -0.5),
        's5_b_im': nrm((N_S5, S5_GROUPS, S5_STATE, S5_GROUP), (2 * S5_GROUP) ** -0.5),
        's5_c_re': nrm((N_S5, S5_GROUPS, S5_GROUP, S5_STATE), S5_STATE ** -0.5),
        's5_c_im': nrm((N_S5, S5_GROUPS, S5_GROUP, S5_STATE), S5_STATE ** -0.5),
        's5_d': nrm((N_S5, D)),
        's5_wv': nrm((N_S5, D, D), BETA * D ** -0.5),
        's5_wg': nrm((N_S5, D, D), D ** -0.5),
    }


def reference(x_prompt, x_sample, cache_mla_latent, cache_mla_krope, state_rwkv_wkv, state_rwkv_shift,
              state_s5_re, state_s5_im, page_table, meta_tokens, ln_g, ln_b, ffn_w1, ffn_w3, ffn_w2,
              mla_w_dq, mla_q_norm, mla_w_uq, mla_w_dkv, mla_kv_norm, mla_w_uk, mla_w_uv, mla_w_o,
              rw_mu, rw_wr, rw_wk, rw_wv, rw_w0, rw_w1, rw_w2, rw_a0, rw_a1, rw_a2, rw_g1, rw_g2,
              rw_k_k, rw_k_a, rw_r_k, rw_lnx_g, rw_lnx_b, rw_wo,
              s5_lam_re, s5_lam_im, s5_log_dt, s5_b_re, s5_b_im, s5_c_re, s5_c_im, s5_d, s5_wv, s5_wg):
    p = dict(ln_g=ln_g, ln_b=ln_b, ffn_w1=ffn_w1, ffn_w3=ffn_w3, ffn_w2=ffn_w2,
             mla_w_dq=mla_w_dq, mla_q_norm=mla_q_norm, mla_w_uq=mla_w_uq, mla_w_dkv=mla_w_dkv,
             mla_kv_norm=mla_kv_norm, mla_w_uk=mla_w_uk, mla_w_uv=mla_w_uv, mla_w_o=mla_w_o,
             rw_mu=rw_mu, rw_wr=rw_wr, rw_wk=rw_wk, rw_wv=rw_wv, rw_w0=rw_w0, rw_w1=rw_w1, rw_w2=rw_w2,
             rw_a0=rw_a0, rw_a1=rw_a1, rw_a2=rw_a2, rw_g1=rw_g1, rw_g2=rw_g2, rw_k_k=rw_k_k, rw_k_a=rw_k_a,
             rw_r_k=rw_r_k, rw_lnx_g=rw_lnx_g, rw_lnx_b=rw_lnx_b, rw_wo=rw_wo,
             s5_lam_re=s5_lam_re, s5_lam_im=s5_lam_im, s5_log_dt=s5_log_dt, s5_b_re=s5_b_re, s5_b_im=s5_b_im,
             s5_c_re=s5_c_re, s5_c_im=s5_c_im, s5_d=s5_d, s5_wv=s5_wv, s5_wg=s5_wg)
    B = x_prompt.shape[0]
    meta = jnp.broadcast_to(meta_tokens.astype(x_prompt.dtype)[None], (B, N_META, D_MODEL))
    x0 = jnp.concatenate([meta, x_prompt], axis=1)
    zw = jnp.zeros((N_RWKV, B, RW_HEADS, RW_HEAD, RW_HEAD), jnp.float32)
    zs = jnp.zeros((N_RWKV, B, D_MODEL), jnp.float32)
    zc = jnp.zeros((N_S5, B, S5_GROUPS, S5_STATE), jnp.float32)
    yp, lat_p, kr_p, wkv_p, sh_p, re_p, im_p = run_group(x0, 0, None, zw, zs, zc, zc, p)
    y_prompt = yp[:, N_META:]
    past_len = page_table.shape[1] * PAGE_SIZE
    y_sample, lat_s, kr_s, wkv_s, sh_s, re_s, im_s = run_group(
        x_sample, past_len, (cache_mla_latent, cache_mla_krope, page_table),
        state_rwkv_wkv, state_rwkv_shift, state_s5_re, state_s5_im, p)
    return (y_prompt, y_sample, lat_p, kr_p, lat_s, kr_s, wkv_p, sh_p, wkv_s, sh_s, re_p, im_p, re_s, im_s)
```

```python
import functools
import math

import jax
import jax.numpy as jnp
from jax import lax
from jax.experimental import pallas as pl
from jax.experimental.pallas import tpu as pltpu

F32 = jnp.float32
BF16 = jnp.bfloat16
HI = lax.Precision.HIGHEST

D_MODEL = 1024
DEPTH = 4
N_META = 16
D_FF = 2816
ALPHA = (2 * DEPTH) ** 0.25
LN_EPS = 1e-5
RMS_EPS = 1e-6
MLA_HEADS = 16
Q_RANK = 768
KV_RANK = 256
NOPE_DIM = 64
ROPE_DIM = 32
QK_DIM = NOPE_DIM + ROPE_DIM
V_DIM = 64
ROPE_THETA = 10000.0
RW_HEAD = 64
RW_HEADS = D_MODEL // RW_HEAD
GN_EPS = 64e-5
S5_GROUP = 16
S5_GROUPS = D_MODEL // S5_GROUP
S5_STATE = 64
S5_WIDTH = S5_GROUPS * S5_STATE
S5_BLOCKS = 8

LANES = 128
HEAD_PAD = 128
ROW_TILE = 256
ROW_ALIGN = 512
ATTN_TILE = 512
RW_CHUNK = 64
S5_TILE = 256
VMEM_LIMIT = 56 * 1024 * 1024
NEG = -1e30


def _dot(a, b, prec=None):
    return jnp.dot(a, b, preferred_element_type=F32, precision=prec)


def _dot_nt(a, b, prec=None):
    return lax.dot_general(a, b, (((1,), (1,)), ((), ())), preferred_element_type=F32, precision=prec)


def _dot_tn(a, b, prec=None):
    return lax.dot_general(a, b, (((0,), (0,)), ((), ())), preferred_element_type=F32, precision=prec)


def _layer_norm(y, g, b):
    mu = jnp.mean(y, -1, keepdims=True)
    yc = y - mu
    var = jnp.mean(yc * yc, -1, keepdims=True)
    return yc * lax.rsqrt(var + LN_EPS) * g + b


def _rms_norm(y, g):
    return y * lax.rsqrt(jnp.mean(y * y, -1, keepdims=True) + RMS_EPS) * g


def _sigmoid(x):
    return 1.0 / (1.0 + jnp.exp(-x))


def _params(*sem):
    return pltpu.CompilerParams(dimension_semantics=sem, vmem_limit_bytes=VMEM_LIMIT)


def _const_spec(a, n_grid):
    zeros = (0,) * a.ndim
    if n_grid == 1:
        imap = lambda i: zeros
    elif n_grid == 2:
        imap = lambda i, j: zeros
    else:
        imap = lambda i, j, k, l: zeros
    return pl.BlockSpec(a.shape, imap, pipeline_mode=pl.Buffered(1))


def _rows_call(body, name, row_ins, const_ins, outs, tm=ROW_TILE):
    n = row_ins[0].shape[0]
    assert n % tm == 0
    in_specs = [pl.BlockSpec((tm, a.shape[1]), lambda i: (i, 0)) for a in row_ins]
    in_specs += [_const_spec(a, 1) for a in const_ins]
    out_specs = [pl.BlockSpec((tm, w), lambda i: (i, 0)) for w, _ in outs]
    out_shape = [jax.ShapeDtypeStruct((n, w), dt) for w, dt in outs]
    res = pl.pallas_call(body, grid=(n // tm,), in_specs=in_specs, out_specs=out_specs, out_shape=out_shape,
                         compiler_params=_params("parallel"), name=name)(*row_ins, *const_ins)
    return res


def _assemble(lay, real, samp, meta):
    w = real.shape[1]
    gap = lambda n: jnp.zeros((n, w), real.dtype)
    return jnp.concatenate([real, gap(lay["soff"] - real.shape[0]), samp,
                            gap(lay["moff"] - lay["soff"] - samp.shape[0]), meta,
                            gap(lay["n_tot"] - lay["moff"] - meta.shape[0])], 0)


FF_CHUNK = 1408


def _ffn_body(x_ref, w1_ref, w3_ref, w2_ref, g_ref, b_ref, o_ref):
    x = x_ref[...]
    xb = x.astype(BF16)
    acc = None
    for c in range(D_FF // FF_CHUNK):
        sl = slice(c * FF_CHUNK, (c + 1) * FF_CHUNK)
        h1 = _dot(xb, w1_ref[:, sl])
        h3 = _dot(xb, w3_ref[:, sl])
        gate = (h1 * _sigmoid(h1) * h3).astype(BF16)
        part = _dot(gate, w2_ref[sl, :])
        acc = part if acc is None else acc + part
    o_ref[...] = _layer_norm(ALPHA * x + 0.5 * acc, g_ref[...], b_ref[...])


def _ffn(x, w1, w3, w2, g, b):
    return _rows_call(_ffn_body, "ffn", [x], [w1, w3, w2, g, b], [(D_MODEL, F32)])[0]


def _mla_proj_body(x_ref, ct_ref, s1_ref, s2_ref, wdq_ref, qn_ref, wq_ref, wdkv_ref, kvn_ref, wk_ref, wv_ref,
                   q_ref, k_ref, v_ref, c_ref, kpe_ref):
    xb = x_ref[...].astype(BF16)
    ct, s1, s2 = ct_ref[...], s1_ref[...], s2_ref[...]
    cq = _rms_norm(_dot(xb, wdq_ref[...]), qn_ref[...])
    q = _dot(cq.astype(BF16), wq_ref[...])
    ckv = _dot(xb, wdkv_ref[...])
    c = _rms_norm(ckv[:, :KV_RANK], kvn_ref[...])
    half = ROPE_DIM // 2
    k1, k2 = ckv[:, KV_RANK:KV_RANK + half], ckv[:, KV_RANK + half:]
    cos, sin = ct[:, NOPE_DIM:NOPE_DIM + half], s1[:, NOPE_DIM + half:QK_DIM]
    kpe = jnp.concatenate([k1 * cos - k2 * sin, k1 * sin + k2 * cos], -1)
    c_ref[...] = c
    kpe_ref[...] = kpe
    cb = c.astype(BF16)
    knope = _dot(cb, wk_ref[...])
    v_ref[...] = _dot(cb, wv_ref[...]).astype(BF16)
    tm = kpe.shape[0]
    kpe_slot = jnp.concatenate([jnp.zeros((tm, NOPE_DIM), F32), kpe, jnp.zeros((tm, HEAD_PAD - QK_DIM), F32)], -1)
    scale = QK_DIM ** -0.5
    for h in range(MLA_HEADS):
        sl = slice(h * HEAD_PAD, (h + 1) * HEAD_PAD)
        blk = q[:, sl]
        rot = blk * ct + pltpu.roll(blk, half, 1) * s1 + pltpu.roll(blk, HEAD_PAD - half, 1) * s2
        q_ref[:, sl] = (rot * scale).astype(BF16)
        k_ref[:, sl] = (knope[:, sl] + kpe_slot).astype(BF16)


def _flash_body(*refs, tq, tk, has_lead):
    if has_lead:
        q_ref, k_ref, v_ref, mk_ref, mv_ref, o_ref, m_sc, l_sc, acc_sc = refs
    else:
        q_ref, k_ref, v_ref, o_ref, m_sc, l_sc, acc_sc = refs
    qi, ki = pl.program_id(2), pl.program_id(3)

    @pl.when(ki == 0)
    def _init():
        for hh in range(2):
            if has_lead:
                sl = slice(hh * HEAD_PAD, (hh + 1) * HEAD_PAD)
                s = _dot_nt(q_ref[:, sl], mk_ref[:, sl])
                m = jnp.max(s, -1, keepdims=True)
                p = jnp.exp(s - m)
                m_sc[hh] = m
                l_sc[hh] = jnp.sum(p, -1, keepdims=True)
                acc_sc[hh] = _dot(p.astype(BF16), mv_ref[...])
            else:
                m_sc[hh] = jnp.full((tq, 1), NEG, F32)
                l_sc[hh] = jnp.zeros((tq, 1), F32)
                acc_sc[hh] = jnp.zeros((tq, 2 * V_DIM), F32)

    @pl.when(ki <= qi)
    def _step():
        row = lax.broadcasted_iota(jnp.int32, (tq, tk), 0)
        col = lax.broadcasted_iota(jnp.int32, (tq, tk), 1)
        visible = jnp.logical_or(ki < qi, col <= row)
        v = v_ref[...]
        for hh in range(2):
            sl = slice(hh * HEAD_PAD, (hh + 1) * HEAD_PAD)
            s = jnp.where(visible, _dot_nt(q_ref[:, sl], k_ref[:, sl]), NEG)
            m_old = m_sc[hh]
            m_new = jnp.maximum(m_old, jnp.max(s, -1, keepdims=True))
            p = jnp.exp(s - m_new)
            corr = jnp.exp(m_old - m_new)
            l_sc[hh] = corr * l_sc[hh] + jnp.sum(p, -1, keepdims=True)
            acc_sc[hh] = corr * acc_sc[hh] + _dot(p.astype(BF16), v)
            m_sc[hh] = m_new

    @pl.when(ki == pl.num_programs(3) - 1)
    def _fin():
        lane = lax.broadcasted_iota(jnp.int32, (tq, 2 * V_DIM), 1)
        o0 = acc_sc[0] / l_sc[0]
        o1 = acc_sc[1] / l_sc[1]
        o_ref[...] = jnp.where(lane < V_DIM, o0, o1).astype(BF16)


def _flash(q, k, v, n_seq, seq_len, row_off, tile, lead_off=None):
    tq = tk = tile
    nq = seq_len // tq
    qo = row_off // tq
    has_lead = lead_off is not None
    in_specs = [
        pl.BlockSpec((tq, 2 * HEAD_PAD), lambda b, h, i, j: (qo + b * nq + i, h)),
        pl.BlockSpec((tk, 2 * HEAD_PAD), lambda b, h, i, j: (qo + b * nq + jnp.minimum(i, j), h)),
        pl.BlockSpec((tk, 2 * V_DIM), lambda b, h, i, j: (qo + b * nq + jnp.minimum(i, j), h)),
    ]
    args = [q, k, v]
    if has_lead:
        lo = lead_off // N_META
        in_specs += [pl.BlockSpec((N_META, 2 * HEAD_PAD), lambda b, h, i, j: (lo + b, h)),
                     pl.BlockSpec((N_META, 2 * V_DIM), lambda b, h, i, j: (lo + b, h))]
        args += [k, v]
    return pl.pallas_call(
        functools.partial(_flash_body, tq=tq, tk=tk, has_lead=has_lead),
        grid=(n_seq, MLA_HEADS // 2, nq, nq),
        in_specs=in_specs,
        out_specs=pl.BlockSpec((tq, 2 * V_DIM), lambda b, h, i, j: (b * nq + i, h)),
        out_shape=jax.ShapeDtypeStruct((n_seq * seq_len, MLA_HEADS * V_DIM), BF16),
        scratch_shapes=[pltpu.VMEM((2, tq, 1), F32), pltpu.VMEM((2, tq, 1), F32),
                        pltpu.VMEM((2, tq, 2 * V_DIM), F32)],
        compiler_params=_params("parallel", "parallel", "parallel", "arbitrary"),
        name="mla_flash_lead" if has_lead else "mla_flash")(*args)


def _qlat_body(q_ref, w_ref, o_ref):
    o_ref[0] = _dot(q_ref[...], w_ref[0]).astype(BF16)


def _decode_body(pt_ref, ql_ref, qp_ref, lat_ref, kr_ref, cn_ref, kn_ref, o_ref, m_sc, l_sc, acc_sc, *, t_new):
    j = pl.program_id(1)
    ql, qp = ql_ref[0], qp_ref[0]

    @pl.when(j == 0)
    def _init():
        m_sc[...] = jnp.full(m_sc.shape, NEG, F32)
        l_sc[...] = jnp.zeros(l_sc.shape, F32)
        acc_sc[...] = jnp.zeros(acc_sc.shape, F32)

    def update(s, vals):
        m_old = m_sc[...]
        m_new = jnp.maximum(m_old, jnp.max(s, -1, keepdims=True))
        p = jnp.exp(s - m_new)
        corr = jnp.exp(m_old - m_new)
        l_sc[...] = corr * l_sc[...] + jnp.sum(p, -1, keepdims=True)
        acc_sc[...] = corr * acc_sc[...] + _dot(p.astype(BF16), vals)
        m_sc[...] = m_new

    lat = lat_ref[0, 0].astype(BF16)
    kr = kr_ref[0, 0].astype(BF16)
    update(_dot_nt(ql, lat) + _dot_nt(qp, kr), lat)

    @pl.when(j == pl.num_programs(1) - 1)
    def _fin():
        cn = cn_ref[0].astype(BF16)
        kn = kn_ref[0].astype(BF16)
        s = _dot_nt(ql, cn) + _dot_nt(qp, kn)
        rows = s.shape[0]
        t_q = lax.rem(lax.broadcasted_iota(jnp.int32, (rows, t_new), 0), t_new)
        t_k = lax.broadcasted_iota(jnp.int32, (rows, t_new), 1)
        update(jnp.where(t_k <= t_q, s, NEG), cn)
        o_ref[0] = (acc_sc[...] / l_sc[...]).astype(BF16)


def _ouv_body(o_ref, w_ref, out_ref):
    out_ref[...] = _dot(o_ref[...], w_ref[0]).astype(BF16)


def _mix_out_body(x_ref, h_ref, w_ref, g_ref, b_ref, o_ref):
    x = x_ref[...]
    o_ref[...] = _layer_norm(ALPHA * x + _dot(h_ref[...], w_ref[...]), g_ref[...], b_ref[...])


def _rope_tables(pos):
    half = ROPE_DIM // 2
    inv = 1.0 / (ROPE_THETA ** (jnp.arange(half, dtype=F32) * (2.0 / ROPE_DIM)))
    ang = pos.astype(F32)[:, None] * inv
    c, s = jnp.cos(ang), jnp.sin(ang)
    n = pos.shape[0]
    one, zero = jnp.ones((n, NOPE_DIM), F32), jnp.zeros((n, NOPE_DIM), F32)
    pad = jnp.zeros((n, HEAD_PAD - QK_DIM), F32)
    z16 = jnp.zeros((n, half), F32)
    ct = jnp.concatenate([one, c, c, pad], 1)
    s1 = jnp.concatenate([zero, z16, s, pad], 1)
    s2 = jnp.concatenate([zero, -s, z16, pad], 1)
    return ct, s1, s2


def _mla_weights(m, w_dq, q_norm, w_uq, w_dkv, kv_norm, w_uk, w_uv, w_o):
    half = ROPE_DIM // 2
    wq = jnp.pad(w_uq[m], ((0, 0), (0, 0), (0, HEAD_PAD - QK_DIM))).reshape(Q_RANK, MLA_HEADS * HEAD_PAD)
    wk = jnp.pad(w_uk[m], ((0, 0), (0, 0), (0, HEAD_PAD - NOPE_DIM))).reshape(KV_RANK, MLA_HEADS * HEAD_PAD)
    wv = w_uv[m].reshape(KV_RANK, MLA_HEADS * V_DIM)
    wukt = jnp.pad(jnp.transpose(w_uk[m], (1, 2, 0)), ((0, 0), (0, HEAD_PAD - NOPE_DIM), (0, 0)))
    wuv_h = jnp.transpose(w_uv[m], (1, 0, 2)).reshape(MLA_HEADS // 2, 2, KV_RANK, V_DIM)
    eye2 = jnp.eye(2, dtype=F32)
    wuv_pair = jnp.einsum('pirv,ij->pirjv', wuv_h, eye2).reshape(MLA_HEADS // 2, 2 * KV_RANK, 2 * V_DIM)
    del half
    return dict(wdq=w_dq[m].astype(BF16), qn=q_norm[m][None], wq=wq.astype(BF16), wdkv=w_dkv[m].astype(BF16),
                kvn=kv_norm[m][None], wk=wk.astype(BF16), wv=wv.astype(BF16), wukt=wukt.astype(BF16),
                wuv_pair=wuv_pair.astype(BF16), wo=w_o[m].reshape(MLA_HEADS * V_DIM, D_MODEL).astype(BF16))


def _mla_layer(x, lay, tabs, w, m, lat_pool, kr_pool, page_table, g, b):
    n_seq, seq, n_dec, t_dec = lay["B"], lay["S"], lay["Bd"], lay["Td"]
    q, k, v, c, kpe = _rows_call(
        _mla_proj_body, "mla_proj", [x, *tabs],
        [w["wdq"], w["qn"], w["wq"], w["wdkv"], w["kvn"], w["wk"], w["wv"]],
        [(MLA_HEADS * HEAD_PAD, BF16), (MLA_HEADS * HEAD_PAD, BF16), (MLA_HEADS * V_DIM, BF16),
         (KV_RANK, F32), (ROPE_DIM, F32)])
    soff, moff = lay["soff"], lay["moff"]
    o_real = _flash(q, k, v, n_seq, seq, 0, min(ATTN_TILE, seq), lead_off=moff)
    o_meta = _flash(q, k, v, n_seq, N_META, moff, N_META)

    n_rows = n_dec * t_dec
    hq = MLA_HEADS * t_dec
    q_lat = pl.pallas_call(
        _qlat_body, grid=(MLA_HEADS,),
        in_specs=[pl.BlockSpec((n_rows, HEAD_PAD), lambda h: (soff // n_rows, h)),
                  pl.BlockSpec((1, HEAD_PAD, KV_RANK), lambda h: (h, 0, 0))],
        out_specs=pl.BlockSpec((1, n_rows, KV_RANK), lambda h: (h, 0, 0)),
        out_shape=jax.ShapeDtypeStruct((MLA_HEADS, n_rows, KV_RANK), BF16),
        compiler_params=_params("parallel"), name="mla_qlat")(q, w["wukt"])
    q_lat = q_lat.reshape(MLA_HEADS, n_dec, t_dec, KV_RANK).transpose(1, 0, 2, 3).reshape(n_dec, hq, KV_RANK)
    q_s = q[soff:soff + n_rows].reshape(n_dec, t_dec, MLA_HEADS, HEAD_PAD)[..., NOPE_DIM:QK_DIM]
    q_pe = q_s.transpose(0, 2, 1, 3).reshape(n_dec, hq, ROPE_DIM)
    c_new = c[soff:soff + n_rows].reshape(n_dec, t_dec, KV_RANK)
    k_new = kpe[soff:soff + n_rows].reshape(n_dec, t_dec, ROPE_DIM)
    n_pages, page = page_table.shape[1], lat_pool.shape[2]
    o_lat = pl.pallas_call(
        functools.partial(_decode_body, t_new=t_dec),
        grid_spec=pltpu.PrefetchScalarGridSpec(
            num_scalar_prefetch=1, grid=(n_dec, n_pages),
            in_specs=[pl.BlockSpec((1, hq, KV_RANK), lambda bb, j, pt: (bb, 0, 0)),
                      pl.BlockSpec((1, hq, ROPE_DIM), lambda bb, j, pt: (bb, 0, 0)),
                      pl.BlockSpec((1, 1, page, KV_RANK), lambda bb, j, pt: (m, pt[bb, j], 0, 0)),
                      pl.BlockSpec((1, 1, page, ROPE_DIM), lambda bb, j, pt: (m, pt[bb, j], 0, 0)),
                      pl.BlockSpec((1, t_dec, KV_RANK), lambda bb, j, pt: (bb, 0, 0)),
                      pl.BlockSpec((1, t_dec, ROPE_DIM), lambda bb, j, pt: (bb, 0, 0))],
            out_specs=pl.BlockSpec((1, hq, KV_RANK), lambda bb, j, pt: (bb, 0, 0)),
            scratch_shapes=[pltpu.VMEM((hq, 1), F32), pltpu.VMEM((hq, 1), F32), pltpu.VMEM((hq, KV_RANK), F32)]),
        out_shape=jax.ShapeDtypeStruct((n_dec, hq, KV_RANK), BF16),
        compiler_params=_params("parallel", "arbitrary"), name="mla_decode")(
            page_table, q_lat, q_pe, lat_pool, kr_pool, c_new, k_new)
    o_lat = o_lat.reshape(n_dec, MLA_HEADS, t_dec, KV_RANK).transpose(0, 2, 1, 3).reshape(n_rows, MLA_HEADS * KV_RANK)
    o_samp = pl.pallas_call(
        _ouv_body, grid=(MLA_HEADS // 2,),
        in_specs=[pl.BlockSpec((n_rows, 2 * KV_RANK), lambda h: (0, h)),
                  pl.BlockSpec((1, 2 * KV_RANK, 2 * V_DIM), lambda h: (h, 0, 0))],
        out_specs=pl.BlockSpec((n_rows, 2 * V_DIM), lambda h: (0, h)),
        out_shape=jax.ShapeDtypeStruct((n_rows, MLA_HEADS * V_DIM), BF16),
        compiler_params=_params("parallel"), name="mla_ouv")(o_lat, w["wuv_pair"])

    o_all = _assemble(lay, o_real, o_samp, o_meta)
    x_new = _rows_call(_mix_out_body, "mla_out", [x, o_all], [w["wo"], g, b], [(D_MODEL, F32)])[0]
    return x_new, c, kpe


def _head_sum(x, gs, gb):
    return _dot(_dot(x, gs, HI), gb, HI)


def _softplus(z):
    return jnp.maximum(z, 0.0) + jnp.log(1.0 + jnp.exp(-jnp.abs(z)))


def _rw_proj_body(x_ref, xp_ref, mu_ref, wr_ref, wk_ref, wv_ref, w0_ref, w1_ref, w2_ref, a0_ref, a1_ref, a2_ref,
                  g1_ref, g2_ref, kk_ref, ka_ref, rk_ref, gs_ref, gb_ref,
                  r_o, d_o, k_o, v_o, kk_o, a_o, g_o, bonus_o):
    x = x_ref[...]
    xx = xp_ref[...] - x
    mix = lambda j: (x + xx * mu_ref[j:j + 1, :]).astype(BF16)
    r = _dot(mix(0), wr_ref[...])
    k = _dot(mix(2), wk_ref[...])
    v = _dot(mix(3), wv_ref[...])
    wl = w0_ref[...] + _dot(jnp.tanh(_dot(mix(1), w1_ref[...])).astype(BF16), w2_ref[...])
    w = -_softplus(-wl) - 0.5
    a = _sigmoid(a0_ref[...] + _dot(_dot(mix(4), a1_ref[...]).astype(BF16), a2_ref[...]))
    g = _dot(_sigmoid(_dot(mix(5), g1_ref[...])).astype(BF16), g2_ref[...])
    gs, gb = gs_ref[...], gb_ref[...]
    kk = k * kk_ref[...]
    kk = kk * lax.rsqrt(jnp.maximum(_head_sum(kk * kk, gs, gb), 1e-24))
    k = k * (1.0 + (a - 1.0) * ka_ref[...])
    r_o[...] = r
    d_o[...] = -jnp.exp(w)
    k_o[...] = k
    v_o[...] = v
    kk_o[...] = kk
    a_o[...] = a
    g_o[...] = g
    bonus_o[...] = _head_sum(r * k * rk_ref[...], gs, gb) * v


def _rw_chunk_body(r_ref, d_ref, k_ref, v_ref, kk_ref, a_ref, s0_ref, y_ref, st_ref, s_sc, *, chunk, n_fac):
    ci = pl.program_id(1)

    @pl.when(ci == 0)
    def _init():
        s_sc[...] = s0_ref[0]

    row = lax.broadcasted_iota(jnp.int32, (chunk, chunk), 0)
    col = lax.broadcasted_iota(jnp.int32, (chunk, chunk), 1)
    strict, incl = col < row, col <= row
    tri = incl.astype(F32)
    for h in range(RW_HEADS):
        sl = slice(h * RW_HEAD, (h + 1) * RW_HEAD)
        r, d, k, v, kk, a = r_ref[:, sl], d_ref[:, sl], k_ref[:, sl], v_ref[:, sl], kk_ref[:, sl], a_ref[:, sl]
        cum = _dot(tri, d, HI)
        cum_end = cum[chunk - 1:chunk, :]
        be = kk * a
        al_t = -kk * jnp.exp(cum - d)
        r_t = r * jnp.exp(cum)
        w_inv = jnp.exp(-cum)
        be_h, k_h = be * w_inv, k * w_inv
        w_rem = jnp.exp(cum_end - cum)
        be_e, k_e = be * w_rem, k * w_rem
        gram = _dot_nt(jnp.concatenate([al_t, r_t], 0), jnp.concatenate([be_h, k_h], 0), HI)
        l_ab = jnp.where(strict, gram[:chunk, :chunk], 0.0)
        l_ak = jnp.where(strict, gram[:chunk, chunk:], 0.0)
        m_rb = jnp.where(incl, gram[chunk:, :chunk], 0.0)
        m_rk = jnp.where(incl, gram[chunk:, chunk:], 0.0)
        u = jnp.concatenate([al_t, _dot(l_ak, v, HI)], 1)
        lp = l_ab
        for i in range(n_fac):
            u = u + _dot(lp, u, HI)
            if i + 1 < n_fac:
                lp = _dot(lp, lp, HI)
        u1, u2 = u[:, :RW_HEAD], u[:, RW_HEAD:]
        g1 = r_t + _dot(m_rb, u1, HI)
        g2 = _dot(m_rb, u2, HI) + _dot(m_rk, v, HI)
        s = s_sc[h]
        y_ref[:, sl] = _dot_nt(g1, s, HI) + g2
        a_mat = _dot_tn(u1, be_e, HI)
        d_mat = _dot_tn(u2, be_e, HI) + _dot_tn(v, k_e, HI)
        s_sc[h] = s * jnp.exp(cum_end) + _dot(s, a_mat, HI) + d_mat

    @pl.when(ci == pl.num_programs(1) - 1)
    def _fin():
        st_ref[0] = s_sc[...]


def _rw_scan(seqs, s0, n_seq, seq_len, row_off, chunk):
    nc = seq_len // chunk
    off = row_off // chunk
    n_fac = max(1, math.ceil(math.log2(chunk)))
    row_spec = pl.BlockSpec((chunk, D_MODEL), lambda b, c: (off + b * nc + c, 0))
    st_spec = pl.BlockSpec((1, RW_HEADS, RW_HEAD, RW_HEAD), lambda b, c: (b, 0, 0, 0))
    y, st = pl.pallas_call(
        functools.partial(_rw_chunk_body, chunk=chunk, n_fac=n_fac),
        grid=(n_seq, nc), in_specs=[row_spec] * 6 + [st_spec],
        out_specs=[pl.BlockSpec((chunk, D_MODEL), lambda b, c: (b * nc + c, 0)), st_spec],
        out_shape=[jax.ShapeDtypeStruct((n_seq * seq_len, D_MODEL), F32),
                   jax.ShapeDtypeStruct((n_seq, RW_HEADS, RW_HEAD, RW_HEAD), F32)],
        scratch_shapes=[pltpu.VMEM((RW_HEADS, RW_HEAD, RW_HEAD), F32)],
        compiler_params=_params("parallel", "arbitrary"), name=f"rwkv_chunk{chunk}")(*seqs, s0)
    return y, st


def _rw_post_body(x_ref, y_ref, g_ref, bonus_ref, lg_ref, lb_ref, gs_ref, gb_ref, wo_ref, ng_ref, nb_ref, o_ref):
    y = y_ref[...]
    gs, gb = gs_ref[...], gb_ref[...]
    mean = _head_sum(y, gs, gb) * (1.0 / RW_HEAD)
    yc = y - mean
    var = _head_sum(yc * yc, gs, gb) * (1.0 / RW_HEAD)
    yn = yc * lax.rsqrt(var + GN_EPS) * lg_ref[...] + lb_ref[...]
    z = ((yn + bonus_ref[...]) * g_ref[...]).astype(BF16)
    o_ref[...] = _layer_norm(ALPHA * x_ref[...] + _dot(z, wo_ref[...]), ng_ref[...], nb_ref[...])


def _rwkv_layer(x, lay, m, p, shift_s, wkv_s, g, b):
    n_seq, seq, n_dec, t_dec = lay["B"], lay["S"], lay["Bd"], lay["Td"]
    soff, moff = lay["soff"], lay["moff"]
    n_rows = n_dec * t_dec
    xr = x[:n_seq * seq].reshape(n_seq, seq, D_MODEL)
    xm = x[moff:moff + n_seq * N_META].reshape(n_seq, N_META, D_MODEL)
    xs = x[soff:soff + n_rows].reshape(n_dec, t_dec, D_MODEL)
    prev_r = jnp.concatenate([xm[:, -1:], xr[:, :-1]], 1).reshape(n_seq * seq, D_MODEL)
    prev_m = jnp.concatenate([jnp.zeros((n_seq, 1, D_MODEL), F32), xm[:, :-1]], 1).reshape(n_seq * N_META, D_MODEL)
    prev_s = jnp.concatenate([shift_s[m][:, None], xs[:, :-1]], 1).reshape(n_rows, D_MODEL)
    xprev = _assemble(lay, prev_r, prev_s, prev_m)

    row = lambda a: a[m][None]
    gs = jnp.pad(jnp.repeat(jnp.eye(RW_HEADS, dtype=F32), RW_HEAD, 0), ((0, 0), (0, LANES - RW_HEADS)))
    gb = gs.T
    bf = lambda a: a[m].astype(BF16)
    r, d, k, v, kk, a, gate, bonus = _rows_call(
        _rw_proj_body, "rwkv_proj", [x, xprev],
        [p['rw_mu'][m], bf(p['rw_wr']), bf(p['rw_wk']), bf(p['rw_wv']), row(p['rw_w0']), bf(p['rw_w1']),
         bf(p['rw_w2']), row(p['rw_a0']), bf(p['rw_a1']), bf(p['rw_a2']), bf(p['rw_g1']), bf(p['rw_g2']),
         row(p['rw_k_k']), row(p['rw_k_a']), p['rw_r_k'][m].reshape(1, D_MODEL), gs, gb],
        [(D_MODEL, F32)] * 8)
    seqs = (r, d, k, v, kk, a)
    zero_state = jnp.zeros((n_seq, RW_HEADS, RW_HEAD, RW_HEAD), F32)
    y_meta, st = _rw_scan(seqs, zero_state, n_seq, N_META, moff, N_META)
    y_real, st_p = _rw_scan(seqs, st, n_seq, seq, 0, min(RW_CHUNK, seq))
    t_pad = 8
    pad_rows = lambda z: jnp.pad(z[soff:soff + n_rows].reshape(n_dec, t_dec, D_MODEL),
                                 ((0, 0), (0, t_pad - t_dec), (0, 0))).reshape(n_dec * t_pad, D_MODEL)
    y_s, st_s = _rw_scan(tuple(pad_rows(z) for z in seqs), wkv_s[m], n_dec, t_pad, 0, t_pad)
    y_s = y_s.reshape(n_dec, t_pad, D_MODEL)[:, :t_dec].reshape(n_rows, D_MODEL)
    y_all = _assemble(lay, y_real, y_s, y_meta)
    x_new = _rows_call(
        _rw_post_body, "rwkv_post", [x, y_all, gate, bonus],
        [row(p['rw_lnx_g']), row(p['rw_lnx_b']), gs, gb, bf(p['rw_wo']), g, b], [(D_MODEL, F32)])[0]
    return x_new, st_p, xr[:, -1], st_s, xs[:, -1]


def _s5_disc_body(lre_ref, lim_ref, dt_ref, bre_ref, bim_ref, lre_s, lim_s, dt_s, bbr_o, bbi_o, ar_o, ai_o):
    def disc(lre, lim, dt):
        mag = jnp.exp(lre * dt)
        ab_re, ab_im = mag * jnp.cos(lim * dt), mag * jnp.sin(lim * dt)
        den = lre * lre + lim * lim
        nr = ab_re - 1.0
        return ab_re, ab_im, (nr * lre + ab_im * lim) / den, (ab_im * lre - nr * lim) / den

    _, _, co_re, co_im = disc(lre_ref[...], lim_ref[...], dt_ref[...])
    b_re, b_im = bre_ref[...], bim_ref[...]
    bbr_o[...] = co_re * b_re - co_im * b_im
    bbi_o[...] = co_re * b_im + co_im * b_re
    ab_re, ab_im, _, _ = disc(lre_s[...], lim_s[...], dt_s[...])
    ar_o[...] = ab_re
    ai_o[...] = ab_im


def _s5_input(u, bre_ref, bim_ref):
    ub = u.astype(BF16)
    w = S5_WIDTH // S5_BLOCKS
    cw = D_MODEL // S5_BLOCKS
    re = [_dot(ub[:, j * cw:(j + 1) * cw], bre_ref[j]) for j in range(S5_BLOCKS)]
    im = [_dot(ub[:, j * cw:(j + 1) * cw], bim_ref[j]) for j in range(S5_BLOCKS)]
    del w
    return re, im


def _s5_output(u, hre, him, cre_ref, cim_ref, dv_ref, wv_ref, wg_ref, g_ref, b_ref):
    w = S5_WIDTH // S5_BLOCKS
    hb_re, hb_im = hre.astype(BF16), him.astype(BF16)
    ys = [_dot(hb_re[:, j * w:(j + 1) * w], cre_ref[j]) - _dot(hb_im[:, j * w:(j + 1) * w], cim_ref[j])
          for j in range(S5_BLOCKS)]
    y = jnp.concatenate(ys, 1) + dv_ref[...] * u
    z = jax.nn.gelu(y).astype(BF16)
    out = _dot(z, wv_ref[...]) * _sigmoid(_dot(z, wg_ref[...]))
    return _layer_norm(ALPHA * u + out, g_ref[...], b_ref[...])


def _s5_seq_body(x_ref, h0r_ref, h0i_ref, ar_ref, ai_ref, bre_ref, bim_ref, cre_ref, cim_ref, dv_ref, wv_ref, wg_ref,
                 g_ref, b_ref, o_ref, htr_ref, hti_ref, hre_sc, him_sc, sr_sc, si_sc, *, tt):
    ti = pl.program_id(1)

    @pl.when(ti == 0)
    def _init():
        sr_sc[...] = h0r_ref[0]
        si_sc[...] = h0i_ref[0]

    u = x_ref[...]
    re, im = _s5_input(u, bre_ref, bim_ref)
    w = S5_WIDTH // S5_BLOCKS
    for j in range(S5_BLOCKS):
        hre_sc[:, j * w:(j + 1) * w] = re[j]
        him_sc[:, j * w:(j + 1) * w] = im[j]
    ar, ai = ar_ref[...], ai_ref[...]

    def step(t, carry):
        hr, hi = carry
        nr = ar * hr - ai * hi + hre_sc[pl.ds(t, 1), :]
        ni = ar * hi + ai * hr + him_sc[pl.ds(t, 1), :]
        hre_sc[pl.ds(t, 1), :] = nr
        him_sc[pl.ds(t, 1), :] = ni
        return nr, ni

    hr, hi = lax.fori_loop(0, tt, step, (sr_sc[...], si_sc[...]))
    sr_sc[...] = hr
    si_sc[...] = hi
    o_ref[...] = _s5_output(u, hre_sc[...], him_sc[...], cre_ref, cim_ref, dv_ref, wv_ref, wg_ref, g_ref, b_ref)

    @pl.when(ti == pl.num_programs(1) - 1)
    def _fin():
        htr_ref[0] = hr
        hti_ref[0] = hi


def _s5_step_body(x_ref, h0r_ref, h0i_ref, ar_ref, ai_ref, bre_ref, bim_ref, cre_ref, cim_ref, dv_ref, wv_ref, wg_ref,
                  g_ref, b_ref, o_ref, htr_ref, hti_ref, sr_sc, si_sc):
    t = pl.program_id(0)

    @pl.when(t == 0)
    def _init():
        sr_sc[...] = h0r_ref[...]
        si_sc[...] = h0i_ref[...]

    u = x_ref[0]
    re, im = _s5_input(u, bre_ref, bim_ref)
    bur, bui = jnp.concatenate(re, 1), jnp.concatenate(im, 1)
    ar, ai = ar_ref[...], ai_ref[...]
    hr, hi = sr_sc[...], si_sc[...]
    nr = ar * hr - ai * hi + bur
    ni = ar * hi + ai * hr + bui
    sr_sc[...] = nr
    si_sc[...] = ni
    o_ref[0] = _s5_output(u, nr, ni, cre_ref, cim_ref, dv_ref, wv_ref, wg_ref, g_ref, b_ref)

    @pl.when(t == pl.num_programs(0) - 1)
    def _fin():
        htr_ref[...] = nr
        hti_ref[...] = ni


def _s5_weights(m, p):
    rep = lambda a: jnp.repeat(a, S5_GROUP, axis=-1)
    lre, lim = p['s5_lam_re'][m], p['s5_lam_im'][m]
    dt = jnp.broadcast_to(jnp.exp(p['s5_log_dt'][m])[:, None], (S5_GROUPS, S5_STATE))
    flat = lambda a: a[m].reshape(S5_GROUPS, S5_STATE * S5_GROUP)
    n = S5_STATE * S5_GROUP
    bb_re, bb_im, ab_re, ab_im = pl.pallas_call(
        _s5_disc_body,
        out_shape=[jax.ShapeDtypeStruct((S5_GROUPS, n), F32)] * 2 + [jax.ShapeDtypeStruct((S5_GROUPS, S5_STATE), F32)] * 2,
        name="s5_disc")(rep(lre), rep(lim), rep(dt), flat(p['s5_b_re']), flat(p['s5_b_im']), lre, lim, dt)
    gpb = S5_GROUPS // S5_BLOCKS
    eye = jnp.eye(gpb, dtype=F32)

    def in_blocks(bb):
        t = bb.reshape(S5_BLOCKS, gpb, S5_STATE, S5_GROUP).transpose(0, 1, 3, 2)
        return jnp.einsum('jgsp,gh->jgshp', t, eye).reshape(S5_BLOCKS, gpb * S5_GROUP, gpb * S5_STATE).astype(BF16)

    def out_blocks(c):
        t = c.reshape(S5_BLOCKS, gpb, S5_GROUP, S5_STATE).transpose(0, 1, 3, 2)
        return jnp.einsum('jgps,gh->jgphs', t, eye).reshape(S5_BLOCKS, gpb * S5_STATE, gpb * S5_GROUP).astype(BF16)

    return dict(ar=ab_re.reshape(1, S5_WIDTH), ai=ab_im.reshape(1, S5_WIDTH),
                bre=in_blocks(bb_re), bim=in_blocks(bb_im),
                cre=out_blocks(p['s5_c_re'][m]), cim=out_blocks(p['s5_c_im'][m]),
                dv=p['s5_d'][m][None], wv=p['s5_wv'][m].astype(BF16), wg=p['s5_wg'][m].astype(BF16))


def _s5_seq(x, h0r, h0i, w, g, b, n_seq, seq_len, row_off, tt):
    nt = seq_len // tt
    off = row_off // tt
    consts = [w["ar"], w["ai"], w["bre"], w["bim"], w["cre"], w["cim"], w["dv"], w["wv"], w["wg"], g, b]
    st_spec = pl.BlockSpec((1, 1, S5_WIDTH), lambda bb, t: (bb, 0, 0))
    return pl.pallas_call(
        functools.partial(_s5_seq_body, tt=tt), grid=(n_seq, nt),
        in_specs=[pl.BlockSpec((tt, D_MODEL), lambda bb, t: (off + bb * nt + t, 0)), st_spec, st_spec]
        + [_const_spec(a, 2) for a in consts],
        out_specs=[pl.BlockSpec((tt, D_MODEL), lambda bb, t: (bb * nt + t, 0)), st_spec, st_spec],
        out_shape=[jax.ShapeDtypeStruct((n_seq * seq_len, D_MODEL), F32)]
        + [jax.ShapeDtypeStruct((n_seq, 1, S5_WIDTH), F32)] * 2,
        scratch_shapes=[pltpu.VMEM((tt, S5_WIDTH), F32), pltpu.VMEM((tt, S5_WIDTH), F32),
                        pltpu.VMEM((1, S5_WIDTH), F32), pltpu.VMEM((1, S5_WIDTH), F32)],
        compiler_params=_params("parallel", "arbitrary"), name=f"s5_seq{tt}")(x, h0r, h0i, *consts)


def _s5_layer(x, lay, m, p, re_s, im_s, g, b):
    n_seq, seq, n_dec, t_dec = lay["B"], lay["S"], lay["Bd"], lay["Td"]
    soff, moff = lay["soff"], lay["moff"]
    n_rows = n_dec * t_dec
    w = _s5_weights(m, p)
    zero = jnp.zeros((n_seq, 1, S5_WIDTH), F32)
    o_meta, hr, hi = _s5_seq(x, zero, zero, w, g, b, n_seq, N_META, moff, N_META)
    o_real, hr, hi = _s5_seq(x, hr, hi, w, g, b, n_seq, seq, 0, min(S5_TILE, seq))
    xs = x[soff:soff + n_rows].reshape(n_dec, t_dec, D_MODEL).transpose(1, 0, 2)
    consts = [w["ar"], w["ai"], w["bre"], w["bim"], w["cre"], w["cim"], w["dv"], w["wv"], w["wg"], g, b]
    st_spec = pl.BlockSpec((n_dec, S5_WIDTH), lambda t: (0, 0))
    o_s, sr, si = pl.pallas_call(
        _s5_step_body, grid=(t_dec,),
        in_specs=[pl.BlockSpec((1, n_dec, D_MODEL), lambda t: (t, 0, 0)), st_spec, st_spec]
        + [_const_spec(a, 1) for a in consts],
        out_specs=[pl.BlockSpec((1, n_dec, D_MODEL), lambda t: (t, 0, 0)), st_spec, st_spec],
        out_shape=[jax.ShapeDtypeStruct((t_dec, n_dec, D_MODEL), F32)]
        + [jax.ShapeDtypeStruct((n_dec, S5_WIDTH), F32)] * 2,
        scratch_shapes=[pltpu.VMEM((n_dec, S5_WIDTH), F32), pltpu.VMEM((n_dec, S5_WIDTH), F32)],
        compiler_params=_params("arbitrary"), name="s5_step")(
            xs, re_s[m].reshape(n_dec, S5_WIDTH), im_s[m].reshape(n_dec, S5_WIDTH), *consts)
    o_s = o_s.transpose(1, 0, 2).reshape(n_rows, D_MODEL)
    x_new = _assemble(lay, o_real, o_s, o_meta)
    shp = (S5_GROUPS, S5_STATE)
    return x_new, hr.reshape(n_seq, *shp), hi.reshape(n_seq, *shp), sr.reshape(n_dec, *shp), si.reshape(n_dec, *shp)


def _round_up(n, m):
    return (n + m - 1) // m * m


def kernel(x_prompt, x_sample, cache_mla_latent, cache_mla_krope, state_rwkv_wkv, state_rwkv_shift, state_s5_re, state_s5_im, page_table, meta_tokens, ln_g, ln_b, ffn_w1, ffn_w3, ffn_w2, mla_w_dq, mla_q_norm, mla_w_uq, mla_w_dkv, mla_kv_norm, mla_w_uk, mla_w_uv, mla_w_o, rw_mu, rw_wr, rw_wk, rw_wv, rw_w0, rw_w1, rw_w2, rw_a0, rw_a1, rw_a2, rw_g1, rw_g2, rw_k_k, rw_k_a, rw_r_k, rw_lnx_g, rw_lnx_b, rw_wo, s5_lam_re, s5_lam_im, s5_log_dt, s5_b_re, s5_b_im, s5_c_re, s5_c_im, s5_d, s5_wv, s5_wg):
    p = dict(rw_mu=rw_mu, rw_wr=rw_wr, rw_wk=rw_wk, rw_wv=rw_wv, rw_w0=rw_w0, rw_w1=rw_w1, rw_w2=rw_w2,
             rw_a0=rw_a0, rw_a1=rw_a1, rw_a2=rw_a2, rw_g1=rw_g1, rw_g2=rw_g2, rw_k_k=rw_k_k, rw_k_a=rw_k_a,
             rw_r_k=rw_r_k, rw_lnx_g=rw_lnx_g, rw_lnx_b=rw_lnx_b, rw_wo=rw_wo,
             s5_lam_re=s5_lam_re, s5_lam_im=s5_lam_im, s5_log_dt=s5_log_dt, s5_b_re=s5_b_re, s5_b_im=s5_b_im,
             s5_c_re=s5_c_re, s5_c_im=s5_c_im, s5_d=s5_d, s5_wv=s5_wv, s5_wg=s5_wg)
    n_seq, seq, _ = x_prompt.shape
    n_dec, t_dec, _ = x_sample.shape
    past = page_table.shape[1] * cache_mla_latent.shape[2]
    n_rows = n_dec * t_dec
    soff = _round_up(n_seq * seq, ROW_ALIGN)
    moff = _round_up(soff + n_rows, ROW_ALIGN)
    n_tot = _round_up(moff + n_seq * N_META, ROW_ALIGN)
    lay = dict(B=n_seq, S=seq, Bd=n_dec, Td=t_dec, soff=soff, moff=moff, n_tot=n_tot)
    assemble = functools.partial(_assemble, lay)

    meta = jnp.broadcast_to(meta_tokens[None], (n_seq, N_META, D_MODEL)).reshape(n_seq * N_META, D_MODEL)
    x = assemble(x_prompt.reshape(n_seq * seq, D_MODEL), x_sample.reshape(n_rows, D_MODEL), meta)
    pos = assemble(jnp.tile(N_META + jnp.arange(seq, dtype=jnp.int32), n_seq)[:, None],
                   jnp.tile(past + jnp.arange(t_dec, dtype=jnp.int32), n_dec)[:, None],
                   jnp.tile(jnp.arange(N_META, dtype=jnp.int32), n_seq)[:, None])[:, 0]
    tabs = _rope_tables(pos)

    lat_p, kr_p, lat_s, kr_s = [], [], [], []
    wkv_p = sh_p = wkv_s = sh_s = re_p = im_p = re_s = im_s = None
    for i in range(DEPTH):
        kind, m = i % 3, i // 3
        norm = lambda j: (ln_g[i, j][None], ln_b[i, j][None])
        ffn = lambda xx, j, nj: _ffn(xx, ffn_w1[i, j].astype(BF16), ffn_w3[i, j].astype(BF16),
                                     ffn_w2[i, j].astype(BF16), *norm(nj))
        x = ffn(x, 0, 0)
        if kind == 0:
            w = _mla_weights(m, mla_w_dq, mla_q_norm, mla_w_uq, mla_w_dkv, mla_kv_norm, mla_w_uk, mla_w_uv, mla_w_o)
            x, c, kpe = _mla_layer(x, lay, tabs, w, m, cache_mla_latent, cache_mla_krope, page_table, *norm(1))
            split = lambda a: (jnp.concatenate(
                [a[moff:moff + n_seq * N_META].reshape(n_seq, N_META, -1), a[:n_seq * seq].reshape(n_seq, seq, -1)], 1),
                a[soff:soff + n_rows].reshape(n_dec, t_dec, -1))
            cp, cs = split(c)
            kp, ks = split(kpe)
            lat_p.append(cp), lat_s.append(cs), kr_p.append(kp), kr_s.append(ks)
        elif kind == 1:
            x, wkv_p, sh_p, wkv_s, sh_s = _rwkv_layer(x, lay, m, p, state_rwkv_shift, state_rwkv_wkv, *norm(1))
        else:
            x, re_p, im_p, re_s, im_s = _s5_layer(x, lay, m, p, state_s5_re, state_s5_im, *norm(1))
        x = ffn(x, 1, 2)

    y_prompt = x[:n_seq * seq].reshape(n_seq, seq, D_MODEL)
    y_sample = x[soff:soff + n_rows].reshape(n_dec, t_dec, D_MODEL)
    return (y_prompt, y_sample, jnp.stack(lat_p), jnp.stack(kr_p), jnp.stack(lat_s), jnp.stack(kr_s),
            wkv_p[None], sh_p[None], wkv_s[None], sh_s[None], re_p[None], im_p[None], re_s[None], im_s[None])
```

```python
import functools
import math

import jax
import jax.numpy as jnp
from jax import lax
from jax.experimental import pallas as pl
from jax.experimental.pallas import tpu as pltpu

F32 = jnp.float32
BF16 = jnp.bfloat16
HI = lax.Precision.HIGHEST

D_MODEL = 1024
DEPTH = 4
N_META = 16
D_FF = 2816
ALPHA = (2 * DEPTH) ** 0.25
LN_EPS = 1e-5
RMS_EPS = 1e-6
MLA_HEADS = 16
Q_RANK = 768
KV_RANK = 256
NOPE_DIM = 64
ROPE_DIM = 32
QK_DIM = NOPE_DIM + ROPE_DIM
V_DIM = 64
ROPE_THETA = 10000.0
RW_HEAD = 64
RW_HEADS = D_MODEL // RW_HEAD
GN_EPS = 64e-5
S5_GROUP = 16
S5_GROUPS = D_MODEL // S5_GROUP
S5_STATE = 64
S5_WIDTH = S5_GROUPS * S5_STATE
S5_BLOCKS = 8

LANES = 128
HEAD_PAD = 128
ROW_TILE = 256
ROW_ALIGN = 512
ATTN_TILE = 512
ATTN_SUB = 128
DECODE_PAGES = 16
RW_CHUNK = 64
RW_GROUP = 4
S5_TILE = 256
VMEM_LIMIT = 56 * 1024 * 1024
NEG = -1e30


def _dot(a, b, prec=None):
    return jnp.dot(a, b, preferred_element_type=F32, precision=prec)


def _dot_nt(a, b, prec=None):
    return lax.dot_general(a, b, (((1,), (1,)), ((), ())), preferred_element_type=F32, precision=prec)


def _dot_tn(a, b, prec=None):
    return lax.dot_general(a, b, (((0,), (0,)), ((), ())), preferred_element_type=F32, precision=prec)


def _layer_norm(y, g, b):
    mu = jnp.mean(y, -1, keepdims=True)
    yc = y - mu
    var = jnp.mean(yc * yc, -1, keepdims=True)
    return yc * lax.rsqrt(var + LN_EPS) * g + b


def _rms_norm(y, g):
    return y * lax.rsqrt(jnp.mean(y * y, -1, keepdims=True) + RMS_EPS) * g


def _sigmoid(x):
    return 1.0 / (1.0 + jnp.exp(-x))


def _params(*sem):
    return pltpu.CompilerParams(dimension_semantics=sem, vmem_limit_bytes=VMEM_LIMIT)


def _const_spec(a, n_grid):
    zeros = (0,) * a.ndim
    if n_grid == 1:
        imap = lambda i: zeros
    elif n_grid == 2:
        imap = lambda i, j: zeros
    else:
        imap = lambda i, j, k, l: zeros
    return pl.BlockSpec(a.shape, imap, pipeline_mode=pl.Buffered(1))


def _rows_call(body, name, row_ins, const_ins, outs, tm=ROW_TILE):
    n = row_ins[0].shape[0]
    assert n % tm == 0
    in_specs = [pl.BlockSpec((tm, a.shape[1]), lambda i: (i, 0)) for a in row_ins]
    in_specs += [_const_spec(a, 1) for a in const_ins]
    out_specs = [pl.BlockSpec((tm, w), lambda i: (i, 0)) for w, _ in outs]
    out_shape = [jax.ShapeDtypeStruct((n, w), dt) for w, dt in outs]
    res = pl.pallas_call(body, grid=(n // tm,), in_specs=in_specs, out_specs=out_specs, out_shape=out_shape,
                         compiler_params=_params("parallel"), name=name)(*row_ins, *const_ins)
    return res


def _assemble(lay, real, samp, meta):
    w = real.shape[1]
    gap = lambda n: jnp.zeros((n, w), real.dtype)
    return jnp.concatenate([real, gap(lay["soff"] - real.shape[0]), samp,
                            gap(lay["moff"] - lay["soff"] - samp.shape[0]), meta,
                            gap(lay["n_tot"] - lay["moff"] - meta.shape[0])], 0)


FF_CHUNK = 1408


def _ffn_body(x_ref, w1_ref, w3_ref, w2_ref, g_ref, b_ref, o_ref):
    x = x_ref[...]
    xb = x.astype(BF16)
    acc = None
    for c in range(D_FF // FF_CHUNK):
        sl = slice(c * FF_CHUNK, (c + 1) * FF_CHUNK)
        h1 = _dot(xb, w1_ref[:, sl])
        h3 = _dot(xb, w3_ref[:, sl])
        gate = (h1 * _sigmoid(h1) * h3).astype(BF16)
        part = _dot(gate, w2_ref[sl, :])
        acc = part if acc is None else acc + part
    o_ref[...] = _layer_norm(ALPHA * x + 0.5 * acc, g_ref[...], b_ref[...])


def _ffn(x, w1, w3, w2, g, b):
    return _rows_call(_ffn_body, "ffn", [x], [w1, w3, w2, g, b], [(D_MODEL, F32)])[0]


def _mla_proj_body(x_ref, ct_ref, s1_ref, s2_ref, wdq_ref, qn_ref, wq_ref, wdkv_ref, kvn_ref, wk_ref, wv_ref,
                   q_ref, k_ref, v_ref, c_ref, kpe_ref):
    xb = x_ref[...].astype(BF16)
    ct, s1, s2 = ct_ref[...], s1_ref[...], s2_ref[...]
    cq = _rms_norm(_dot(xb, wdq_ref[...]), qn_ref[...])
    q = _dot(cq.astype(BF16), wq_ref[...])
    ckv = _dot(xb, wdkv_ref[...])
    c = _rms_norm(ckv[:, :KV_RANK], kvn_ref[...])
    half = ROPE_DIM // 2
    k1, k2 = ckv[:, KV_RANK:KV_RANK + half], ckv[:, KV_RANK + half:]
    cos, sin = ct[:, NOPE_DIM:NOPE_DIM + half], s1[:, NOPE_DIM + half:QK_DIM]
    kpe = jnp.concatenate([k1 * cos - k2 * sin, k1 * sin + k2 * cos], -1)
    c_ref[...] = c
    kpe_ref[...] = kpe
    cb = c.astype(BF16)
    knope = _dot(cb, wk_ref[...])
    v_ref[...] = _dot(cb, wv_ref[...]).astype(BF16)
    tm = kpe.shape[0]
    kpe_slot = jnp.concatenate([jnp.zeros((tm, NOPE_DIM), F32), kpe, jnp.zeros((tm, HEAD_PAD - QK_DIM), F32)], -1)
    scale = QK_DIM ** -0.5 * math.log2(math.e)
    for h in range(MLA_HEADS):
        sl = slice(h * HEAD_PAD, (h + 1) * HEAD_PAD)
        blk = q[:, sl]
        rot = blk * ct + pltpu.roll(blk, half, 1) * s1 + pltpu.roll(blk, HEAD_PAD - half, 1) * s2
        q_ref[:, sl] = (rot * scale).astype(BF16)
        k_ref[:, sl] = (knope[:, sl] + kpe_slot).astype(BF16)


def _flash_body(qt_ref, kt_ref, *refs, tq, tk, sub, has_lead):
    if has_lead:
        q_ref, k_ref, v_ref, mk_ref, mv_ref, o_ref, m_sc, l_sc, acc_sc = refs
    else:
        q_ref, k_ref, v_ref, o_ref, m_sc, l_sc, acc_sc = refs
    t = pl.program_id(2)
    qi, ki = qt_ref[t], kt_ref[t]
    blocks = [(hh, r0) for r0 in range(0, tq, sub) for hh in range(2)]

    def q_blk(hh, r0):
        return q_ref[r0:r0 + sub, hh * HEAD_PAD:(hh + 1) * HEAD_PAD]

    def weights(hh, r0, s, first):
        rows = slice(r0, r0 + sub)
        width = s.shape[1]
        m_cur = jnp.max(s, -1, keepdims=True)
        if first:
            m_new = jnp.broadcast_to(m_cur, (sub, LANES))
        else:
            m_old = m_sc[hh, rows, :]
            m_new = jnp.maximum(m_old, m_cur)
        m_wide = pltpu.repeat(m_new, width // LANES, 1) if width >= LANES else m_new[:, :width]
        p = jnp.exp2(s - m_wide)
        p_sum = jnp.sum(p, -1, keepdims=True)
        if first:
            corr = None
            l_sc[hh, rows, :] = jnp.broadcast_to(p_sum, (sub, LANES))
        else:
            corr = jnp.exp2(m_old - m_new)
            l_sc[hh, rows, :] = corr * l_sc[hh, rows, :] + p_sum
        m_sc[hh, rows, :] = m_new
        return p.astype(BF16), corr

    def attend(scores, vals, first=False):
        pcs = [weights(hh, r0, s, first) for (hh, r0), s in zip(blocks, scores)]
        for (hh, r0), (p, corr) in zip(blocks, pcs):
            rows = slice(r0, r0 + sub)
            pv = _dot(p, vals)
            acc_sc[hh, rows, :] = pv if first else corr * acc_sc[hh, rows, :] + pv

    @pl.when(ki == 0)
    def _init():
        if has_lead:
            scores = [_dot_nt(q_blk(hh, r0), mk_ref[:, hh * HEAD_PAD:(hh + 1) * HEAD_PAD]) for hh, r0 in blocks]
            attend(scores, mv_ref[...], first=True)
        else:
            m_sc[...] = jnp.full(m_sc.shape, NEG, F32)
            l_sc[...] = jnp.zeros(l_sc.shape, F32)
            acc_sc[...] = jnp.zeros(acc_sc.shape, F32)

    def step(masked):
        scores = []
        for hh, r0 in blocks:
            s = _dot_nt(q_blk(hh, r0), k_ref[:, hh * HEAD_PAD:(hh + 1) * HEAD_PAD])
            if masked:
                row = r0 + lax.broadcasted_iota(jnp.int32, (sub, tk), 0)
                col = lax.broadcasted_iota(jnp.int32, (sub, tk), 1)
                s = jnp.where(col <= row, s, NEG)
            scores.append(s)
        attend(scores, v_ref[...])

    @pl.when(ki < qi)
    def _below():
        step(False)

    @pl.when(ki == qi)
    def _diag():
        step(True)
        lane = lax.broadcasted_iota(jnp.int32, (tq, 2 * V_DIM), 1)
        o0 = acc_sc[0] / l_sc[0]
        o1 = acc_sc[1] / l_sc[1]
        o_ref[...] = jnp.where(lane < V_DIM, o0, o1).astype(BF16)


def _flash(q, k, v, n_seq, seq_len, row_off, tile, lead_off=None):
    tq = tk = tile
    nq = seq_len // tq
    qo = row_off // tq
    has_lead = lead_off is not None
    pairs = [(i, j) for i in range(nq) for j in range(i + 1)]
    qt = jnp.array([i for i, _ in pairs], jnp.int32)
    kt = jnp.array([j for _, j in pairs], jnp.int32)
    in_specs = [
        pl.BlockSpec((tq, 2 * HEAD_PAD), lambda b, h, t, qt, kt: (qo + b * nq + qt[t], h)),
        pl.BlockSpec((tk, 2 * HEAD_PAD), lambda b, h, t, qt, kt: (qo + b * nq + kt[t], h)),
        pl.BlockSpec((tk, 2 * V_DIM), lambda b, h, t, qt, kt: (qo + b * nq + kt[t], h)),
    ]
    args = [q, k, v]
    if has_lead:
        lo = lead_off // N_META
        in_specs += [pl.BlockSpec((N_META, 2 * HEAD_PAD), lambda b, h, t, qt, kt: (lo + b, h)),
                     pl.BlockSpec((N_META, 2 * V_DIM), lambda b, h, t, qt, kt: (lo + b, h))]
        args += [k, v]
    return pl.pallas_call(
        functools.partial(_flash_body, tq=tq, tk=tk, sub=min(ATTN_SUB, tq), has_lead=has_lead),
        grid_spec=pltpu.PrefetchScalarGridSpec(
            num_scalar_prefetch=2, grid=(n_seq, MLA_HEADS // 2, len(pairs)),
            in_specs=in_specs,
            out_specs=pl.BlockSpec((tq, 2 * V_DIM), lambda b, h, t, qt, kt: (b * nq + qt[t], h)),
            scratch_shapes=[pltpu.VMEM((2, tq, LANES), F32), pltpu.VMEM((2, tq, LANES), F32),
                            pltpu.VMEM((2, tq, 2 * V_DIM), F32)]),
        out_shape=jax.ShapeDtypeStruct((n_seq * seq_len, MLA_HEADS * V_DIM), BF16),
        compiler_params=_params("parallel", "parallel", "arbitrary"),
        name="mla_flash_lead" if has_lead else "mla_flash")(qt, kt, *args)


def _qlat_body(q_ref, w_ref, o_ref):
    o_ref[0] = _dot(q_ref[...], w_ref[0]).astype(BF16)


def _decode_body(pt_ref, ql_ref, qp_ref, *refs, t_new, n_pg):
    lat_refs, kr_refs = refs[:n_pg], refs[n_pg:2 * n_pg]
    cn_ref, kn_ref, o_ref, m_sc, l_sc, acc_sc = refs[2 * n_pg:]
    j = pl.program_id(1)
    ql, qp = ql_ref[0], qp_ref[0]

    @pl.when(j == 0)
    def _init():
        m_sc[...] = jnp.full(m_sc.shape, NEG, F32)
        l_sc[...] = jnp.zeros(l_sc.shape, F32)
        acc_sc[...] = jnp.zeros(acc_sc.shape, F32)

    def update(s, vals):
        m_old = m_sc[...]
        m_new = jnp.maximum(m_old, jnp.max(s, -1, keepdims=True))
        p = jnp.exp2(s - m_new)
        corr = jnp.exp2(m_old - m_new)
        l_sc[...] = corr * l_sc[...] + jnp.sum(p, -1, keepdims=True)
        pb = p.astype(BF16)
        acc = corr * acc_sc[...]
        off = 0
        for blk in vals:
            acc = acc + _dot(pb[:, off:off + blk.shape[0]], blk)
            off += blk.shape[0]
        acc_sc[...] = acc
        m_sc[...] = m_new

    lats = [r[0, 0].astype(BF16) for r in lat_refs]
    s = jnp.concatenate([_dot_nt(ql, lat) + _dot(qp, kr[0, 0].astype(BF16)) for lat, kr in zip(lats, kr_refs)], 1)
    update(s, lats)

    @pl.when(j == pl.num_programs(1) - 1)
    def _fin():
        cn = cn_ref[0].astype(BF16)
        kn = kn_ref[0].astype(BF16)
        s = _dot_nt(ql, cn) + _dot_nt(qp, kn)
        rows = s.shape[0]
        t_q = lax.rem(lax.broadcasted_iota(jnp.int32, (rows, t_new), 0), t_new)
        t_k = lax.broadcasted_iota(jnp.int32, (rows, t_new), 1)
        update(jnp.where(t_k <= t_q, s, NEG), [cn])
        o_ref[0] = (acc_sc[...] / l_sc[...]).astype(BF16)


def _ouv_body(o_ref, w_ref, out_ref):
    out_ref[...] = _dot(o_ref[...], w_ref[0]).astype(BF16)


def _mix_out_body(x_ref, h_ref, w_ref, g_ref, b_ref, o_ref):
    x = x_ref[...]
    o_ref[...] = _layer_norm(ALPHA * x + _dot(h_ref[...], w_ref[...]), g_ref[...], b_ref[...])


def _rope_tables(pos):
    half = ROPE_DIM // 2
    inv = 1.0 / (ROPE_THETA ** (jnp.arange(half, dtype=F32) * (2.0 / ROPE_DIM)))
    ang = pos.astype(F32)[:, None] * inv
    c, s = jnp.cos(ang), jnp.sin(ang)
    n = pos.shape[0]
    one, zero = jnp.ones((n, NOPE_DIM), F32), jnp.zeros((n, NOPE_DIM), F32)
    pad = jnp.zeros((n, HEAD_PAD - QK_DIM), F32)
    z16 = jnp.zeros((n, half), F32)
    ct = jnp.concatenate([one, c, c, pad], 1)
    s1 = jnp.concatenate([zero, z16, s, pad], 1)
    s2 = jnp.concatenate([zero, -s, z16, pad], 1)
    return ct, s1, s2


def _mla_weights(m, w_dq, q_norm, w_uq, w_dkv, kv_norm, w_uk, w_uv, w_o):
    half = ROPE_DIM // 2
    wq = jnp.pad(w_uq[m], ((0, 0), (0, 0), (0, HEAD_PAD - QK_DIM))).reshape(Q_RANK, MLA_HEADS * HEAD_PAD)
    wk = jnp.pad(w_uk[m], ((0, 0), (0, 0), (0, HEAD_PAD - NOPE_DIM))).reshape(KV_RANK, MLA_HEADS * HEAD_PAD)
    wv = w_uv[m].reshape(KV_RANK, MLA_HEADS * V_DIM)
    wukt = jnp.pad(jnp.transpose(w_uk[m], (1, 2, 0)), ((0, 0), (0, HEAD_PAD - NOPE_DIM), (0, 0)))
    wuv_h = jnp.transpose(w_uv[m], (1, 0, 2)).reshape(MLA_HEADS // 2, 2, KV_RANK, V_DIM)
    eye2 = jnp.eye(2, dtype=F32)
    wuv_pair = jnp.einsum('pirv,ij->pirjv', wuv_h, eye2).reshape(MLA_HEADS // 2, 2 * KV_RANK, 2 * V_DIM)
    del half
    return dict(wdq=w_dq[m].astype(BF16), qn=q_norm[m][None], wq=wq.astype(BF16), wdkv=w_dkv[m].astype(BF16),
                kvn=kv_norm[m][None], wk=wk.astype(BF16), wv=wv.astype(BF16), wukt=wukt.astype(BF16),
                wuv_pair=wuv_pair.astype(BF16), wo=w_o[m].reshape(MLA_HEADS * V_DIM, D_MODEL).astype(BF16))


def _mla_layer(x, lay, tabs, w, m, lat_pool, kr_pool, page_table, g, b):
    n_seq, seq, n_dec, t_dec = lay["B"], lay["S"], lay["Bd"], lay["Td"]
    q, k, v, c, kpe = _rows_call(
        _mla_proj_body, "mla_proj", [x, *tabs],
        [w["wdq"], w["qn"], w["wq"], w["wdkv"], w["kvn"], w["wk"], w["wv"]],
        [(MLA_HEADS * HEAD_PAD, BF16), (MLA_HEADS * HEAD_PAD, BF16), (MLA_HEADS * V_DIM, BF16),
         (KV_RANK, F32), (ROPE_DIM, F32)])
    soff, moff = lay["soff"], lay["moff"]
    o_real = _flash(q, k, v, n_seq, seq, 0, min(ATTN_TILE, seq), lead_off=moff)
    o_meta = _flash(q, k, v, n_seq, N_META, moff, N_META)

    n_rows = n_dec * t_dec
    hq = MLA_HEADS * t_dec
    q_lat = pl.pallas_call(
        _qlat_body, grid=(MLA_HEADS,),
        in_specs=[pl.BlockSpec((n_rows, HEAD_PAD), lambda h: (soff // n_rows, h)),
                  pl.BlockSpec((1, HEAD_PAD, KV_RANK), lambda h: (h, 0, 0))],
        out_specs=pl.BlockSpec((1, n_rows, KV_RANK), lambda h: (h, 0, 0)),
        out_shape=jax.ShapeDtypeStruct((MLA_HEADS, n_rows, KV_RANK), BF16),
        compiler_params=_params("parallel"), name="mla_qlat")(q, w["wukt"])
    q_lat = q_lat.reshape(MLA_HEADS, n_dec, t_dec, KV_RANK).transpose(1, 0, 2, 3).reshape(n_dec, hq, KV_RANK)
    q_s = q[soff:soff + n_rows].reshape(n_dec, t_dec, MLA_HEADS, HEAD_PAD)[..., NOPE_DIM:QK_DIM]
    q_pe = q_s.transpose(0, 2, 1, 3).reshape(n_dec, hq, ROPE_DIM)
    c_new = c[soff:soff + n_rows].reshape(n_dec, t_dec, KV_RANK)
    k_new = kpe[soff:soff + n_rows].reshape(n_dec, t_dec, ROPE_DIM)
    n_pages, page = page_table.shape[1], lat_pool.shape[2]
    n_pg = math.gcd(DECODE_PAGES, n_pages)
    kr_pool_t = jnp.swapaxes(kr_pool, 2, 3)
    lat_spec = lambda i: pl.BlockSpec((1, 1, page, KV_RANK), lambda bb, j, pt: (m, pt[bb, j * n_pg + i], 0, 0))
    kr_spec = lambda i: pl.BlockSpec((1, 1, ROPE_DIM, page), lambda bb, j, pt: (m, pt[bb, j * n_pg + i], 0, 0))
    o_lat = pl.pallas_call(
        functools.partial(_decode_body, t_new=t_dec, n_pg=n_pg),
        grid_spec=pltpu.PrefetchScalarGridSpec(
            num_scalar_prefetch=1, grid=(n_dec, n_pages // n_pg),
            in_specs=[pl.BlockSpec((1, hq, KV_RANK), lambda bb, j, pt: (bb, 0, 0)),
                      pl.BlockSpec((1, hq, ROPE_DIM), lambda bb, j, pt: (bb, 0, 0))]
            + [lat_spec(i) for i in range(n_pg)] + [kr_spec(i) for i in range(n_pg)]
            + [pl.BlockSpec((1, t_dec, KV_RANK), lambda bb, j, pt: (bb, 0, 0)),
               pl.BlockSpec((1, t_dec, ROPE_DIM), lambda bb, j, pt: (bb, 0, 0))],
            out_specs=pl.BlockSpec((1, hq, KV_RANK), lambda bb, j, pt: (bb, 0, 0)),
            scratch_shapes=[pltpu.VMEM((hq, 1), F32), pltpu.VMEM((hq, 1), F32), pltpu.VMEM((hq, KV_RANK), F32)]),
        out_shape=jax.ShapeDtypeStruct((n_dec, hq, KV_RANK), BF16),
        compiler_params=_params("parallel", "arbitrary"), name="mla_decode")(
            page_table, q_lat, q_pe, *([lat_pool] * n_pg), *([kr_pool_t] * n_pg), c_new, k_new)
    o_lat = o_lat.reshape(n_dec, MLA_HEADS, t_dec, KV_RANK).transpose(0, 2, 1, 3).reshape(n_rows, MLA_HEADS * KV_RANK)
    o_samp = pl.pallas_call(
        _ouv_body, grid=(MLA_HEADS // 2,),
        in_specs=[pl.BlockSpec((n_rows, 2 * KV_RANK), lambda h: (0, h)),
                  pl.BlockSpec((1, 2 * KV_RANK, 2 * V_DIM), lambda h: (h, 0, 0))],
        out_specs=pl.BlockSpec((n_rows, 2 * V_DIM), lambda h: (0, h)),
        out_shape=jax.ShapeDtypeStruct((n_rows, MLA_HEADS * V_DIM), BF16),
        compiler_params=_params("parallel"), name="mla_ouv")(o_lat, w["wuv_pair"])

    o_all = _assemble(lay, o_real, o_samp, o_meta)
    x_new = _rows_call(_mix_out_body, "mla_out", [x, o_all], [w["wo"], g, b], [(D_MODEL, F32)])[0]
    return x_new, c, kpe


def _head_sum(x, gs, gb):
    return _dot(_dot(x, gs, HI), gb, HI)


def _softplus(z):
    return jnp.maximum(z, 0.0) + jnp.log(1.0 + jnp.exp(-jnp.abs(z)))


def _rw_proj_body(x_ref, xp_ref, mu_ref, wr_ref, wk_ref, wv_ref, w0_ref, w1_ref, w2_ref, a0_ref, a1_ref, a2_ref,
                  g1_ref, g2_ref, kk_ref, ka_ref, rk_ref, gs_ref, gb_ref,
                  r_o, d_o, k_o, v_o, kk_o, a_o, g_o, bonus_o):
    x = x_ref[...]
    xx = xp_ref[...] - x
    mix = lambda j: (x + xx * mu_ref[j:j + 1, :]).astype(BF16)
    r = _dot(mix(0), wr_ref[...])
    k = _dot(mix(2), wk_ref[...])
    v = _dot(mix(3), wv_ref[...])
    wl = w0_ref[...] + _dot(jnp.tanh(_dot(mix(1), w1_ref[...])).astype(BF16), w2_ref[...])
    w = -_softplus(-wl) - 0.5
    a = _sigmoid(a0_ref[...] + _dot(_dot(mix(4), a1_ref[...]).astype(BF16), a2_ref[...]))
    g = _dot(_sigmoid(_dot(mix(5), g1_ref[...])).astype(BF16), g2_ref[...])
    gs, gb = gs_ref[...], gb_ref[...]
    kk = k * kk_ref[...]
    kk = kk * lax.rsqrt(jnp.maximum(_head_sum(kk * kk, gs, gb), 1e-24))
    k = k * (1.0 + (a - 1.0) * ka_ref[...])
    r_o[...] = r
    d_o[...] = -jnp.exp(w)
    k_o[...] = k
    v_o[...] = v
    kk_o[...] = kk
    a_o[...] = a
    g_o[...] = g
    bonus_o[...] = _head_sum(r * k * rk_ref[...], gs, gb) * v


def _rw_chunk_body(r_ref, d_ref, k_ref, v_ref, kk_ref, a_ref, s0_ref, y_ref, st_ref, s_sc, *, chunk, n_fac, group):
    ci = pl.program_id(1)

    @pl.when(ci == 0)
    def _init():
        s_sc[...] = s0_ref[...]

    row = lax.broadcasted_iota(jnp.int32, (chunk, chunk), 0)
    col = lax.broadcasted_iota(jnp.int32, (chunk, chunk), 1)
    strict, incl = col < row, col <= row
    rows = group * chunk
    if group == 1:
        same, below = None, incl
    else:
        rr = lax.broadcasted_iota(jnp.int32, (rows, rows), 0)
        cc = lax.broadcasted_iota(jnp.int32, (rows, rows), 1)
        same = (rr // chunk) == (cc // chunk)
        below = jnp.logical_and(same, cc <= rr)
    d_all, k_all, kk_all = d_ref[...], k_ref[...], kk_ref[...]
    cum = _dot(below.astype(F32), d_all, HI)
    if group == 1:
        cum_end = cum[chunk - 1:chunk, :]
    else:
        cum_end = _dot(same.astype(F32), d_all, HI)
    be_all = kk_all * a_ref[...]
    w_inv = jnp.exp(-cum)
    w_rem = jnp.exp(cum_end - cum)
    al_t_all = (-kk_all * jnp.exp(cum - d_all)).astype(BF16)
    r_t_all = r_ref[...] * jnp.exp(cum)
    be_h_all = (be_all * w_inv).astype(BF16)
    k_h_all = (k_all * w_inv).astype(BF16)
    be_e_all = (be_all * w_rem).astype(BF16)
    k_e_all = (k_all * w_rem).astype(BF16)
    w_end = jnp.exp(cum_end)
    v_all = v_ref[...].astype(BF16)
    zero = jnp.zeros((chunk, RW_HEAD), BF16)
    chains = [(g, h) for g in range(group) for h in range(RW_HEADS)]
    ids = range(len(chains))
    cut = lambda a, g, h: a[g * chunk:(g + 1) * chunk, h * RW_HEAD:(h + 1) * RW_HEAD]
    gram = [_dot_nt(jnp.concatenate([cut(al_t_all, *c), cut(r_t_all, *c).astype(BF16)], 0),
                    jnp.concatenate([cut(be_h_all, *c), cut(k_h_all, *c)], 0)) for c in chains]
    l_ab = [jnp.where(strict, g[:chunk, :chunk], 0.0).astype(BF16) for g in gram]
    l_ak = [jnp.where(strict, g[:chunk, chunk:], 0.0).astype(BF16) for g in gram]
    m_r = [jnp.concatenate([jnp.where(incl, g[chunk:, :chunk], 0.0),
                            jnp.where(incl, g[chunk:, chunk:], 0.0)], 1).astype(BF16) for g in gram]
    u = [jnp.concatenate([cut(al_t_all, *c).astype(F32), _dot(l_ak[i], cut(v_all, *c))], 1) for i, c in zip(ids, chains)]
    lp = l_ab
    for f in range(n_fac):
        u = [u[i] + _dot(lp[i], u[i].astype(BF16)) for i in ids]
        if f + 1 < n_fac:
            lp = [_dot(lp[i], lp[i]).astype(BF16) for i in ids]
    z = [jnp.concatenate([u[i].astype(BF16), jnp.concatenate([zero, cut(v_all, *c)], 1)], 0) for i, c in zip(ids, chains)]
    g = [_dot(m_r[i], z[i]) for i in ids]
    ad = [_dot_tn(z[i], jnp.concatenate([cut(be_e_all, *c), cut(k_e_all, *c)], 0)) for i, c in zip(ids, chains)]
    s_old = [s_sc[c[0], c[1]] for c in chains]
    s_bf = [s.astype(BF16) for s in s_old]
    y = [_dot_nt((cut(r_t_all, *c) + g[i][:, :RW_HEAD]).astype(BF16), s_bf[i]) + g[i][:, RW_HEAD:]
         for i, c in zip(ids, chains)]
    y_ref[...] = jnp.concatenate(
        [jnp.concatenate(y[gi * RW_HEADS:(gi + 1) * RW_HEADS], 1) for gi in range(group)], 0)
    for i, (gi, h) in zip(ids, chains):
        w_h = w_end[gi * chunk:gi * chunk + 1, h * RW_HEAD:(h + 1) * RW_HEAD]
        s_sc[gi, h] = s_old[i] * w_h + _dot(s_bf[i], ad[i][:RW_HEAD].astype(BF16)) + ad[i][RW_HEAD:]

    @pl.when(ci == pl.num_programs(1) - 1)
    def _fin():
        st_ref[...] = s_sc[...]


def _rw_scan(seqs, s0, n_seq, seq_len, row_off, chunk, group=1):
    nc = seq_len // chunk
    assert group == 1 or nc == 1
    assert n_seq % group == 0
    rows = group * chunk
    off = row_off // rows
    n_fac = max(1, math.ceil(math.log2(chunk)))
    row_spec = pl.BlockSpec((rows, D_MODEL), lambda b, c: (off + b * nc + c, 0))
    st_spec = pl.BlockSpec((group, RW_HEADS, RW_HEAD, RW_HEAD), lambda b, c: (b, 0, 0, 0))
    y, st = pl.pallas_call(
        functools.partial(_rw_chunk_body, chunk=chunk, n_fac=n_fac, group=group),
        grid=(n_seq // group, nc), in_specs=[row_spec] * 6 + [st_spec],
        out_specs=[pl.BlockSpec((rows, D_MODEL), lambda b, c: (b * nc + c, 0)), st_spec],
        out_shape=[jax.ShapeDtypeStruct((n_seq * seq_len, D_MODEL), F32),
                   jax.ShapeDtypeStruct((n_seq, RW_HEADS, RW_HEAD, RW_HEAD), F32)],
        scratch_shapes=[pltpu.VMEM((group, RW_HEADS, RW_HEAD, RW_HEAD), F32)],
        compiler_params=_params("parallel", "arbitrary"), name=f"rwkv_chunk{chunk}")(*seqs, s0)
    return y, st


def _rw_post_body(x_ref, y_ref, g_ref, bonus_ref, lg_ref, lb_ref, gs_ref, gb_ref, wo_ref, ng_ref, nb_ref, o_ref):
    y = y_ref[...]
    gs, gb = gs_ref[...], gb_ref[...]
    mean = _head_sum(y, gs, gb) * (1.0 / RW_HEAD)
    yc = y - mean
    var = _head_sum(yc * yc, gs, gb) * (1.0 / RW_HEAD)
    yn = yc * lax.rsqrt(var + GN_EPS) * lg_ref[...] + lb_ref[...]
    z = ((yn + bonus_ref[...]) * g_ref[...]).astype(BF16)
    o_ref[...] = _layer_norm(ALPHA * x_ref[...] + _dot(z, wo_ref[...]), ng_ref[...], nb_ref[...])


def _rwkv_layer(x, lay, m, p, shift_s, wkv_s, g, b):
    n_seq, seq, n_dec, t_dec = lay["B"], lay["S"], lay["Bd"], lay["Td"]
    soff, moff = lay["soff"], lay["moff"]
    n_rows = n_dec * t_dec
    xr = x[:n_seq * seq].reshape(n_seq, seq, D_MODEL)
    xm = x[moff:moff + n_seq * N_META].reshape(n_seq, N_META, D_MODEL)
    xs = x[soff:soff + n_rows].reshape(n_dec, t_dec, D_MODEL)
    prev_r = jnp.concatenate([xm[:, -1:], xr[:, :-1]], 1).reshape(n_seq * seq, D_MODEL)
    prev_m = jnp.concatenate([jnp.zeros((n_seq, 1, D_MODEL), F32), xm[:, :-1]], 1).reshape(n_seq * N_META, D_MODEL)
    prev_s = jnp.concatenate([shift_s[m][:, None], xs[:, :-1]], 1).reshape(n_rows, D_MODEL)
    xprev = _assemble(lay, prev_r, prev_s, prev_m)

    row = lambda a: a[m][None]
    gs = jnp.pad(jnp.repeat(jnp.eye(RW_HEADS, dtype=F32), RW_HEAD, 0), ((0, 0), (0, LANES - RW_HEADS)))
    gb = gs.T
    bf = lambda a: a[m].astype(BF16)
    r, d, k, v, kk, a, gate, bonus = _rows_call(
        _rw_proj_body, "rwkv_proj", [x, xprev],
        [p['rw_mu'][m], bf(p['rw_wr']), bf(p['rw_wk']), bf(p['rw_wv']), row(p['rw_w0']), bf(p['rw_w1']),
         bf(p['rw_w2']), row(p['rw_a0']), bf(p['rw_a1']), bf(p['rw_a2']), bf(p['rw_g1']), bf(p['rw_g2']),
         row(p['rw_k_k']), row(p['rw_k_a']), p['rw_r_k'][m].reshape(1, D_MODEL), gs, gb],
        [(D_MODEL, F32)] * 8)
    seqs = (r, d, k, v, kk, a)
    zero_state = jnp.zeros((n_seq, RW_HEADS, RW_HEAD, RW_HEAD), F32)
    y_meta, st = _rw_scan(seqs, zero_state, n_seq, N_META, moff, N_META)
    y_real, st_p = _rw_scan(seqs, st, n_seq, seq, 0, min(RW_CHUNK, seq))
    t_pad = 8
    pad_rows = lambda z: jnp.pad(z[soff:soff + n_rows].reshape(n_dec, t_dec, D_MODEL),
                                 ((0, 0), (0, t_pad - t_dec), (0, 0))).reshape(n_dec * t_pad, D_MODEL)
    y_s, st_s = _rw_scan(tuple(pad_rows(z) for z in seqs), wkv_s[m], n_dec, t_pad, 0, t_pad,
                         group=math.gcd(RW_GROUP, n_dec))
    y_s = y_s.reshape(n_dec, t_pad, D_MODEL)[:, :t_dec].reshape(n_rows, D_MODEL)
    y_all = _assemble(lay, y_real, y_s, y_meta)
    x_new = _rows_call(
        _rw_post_body, "rwkv_post", [x, y_all, gate, bonus],
        [row(p['rw_lnx_g']), row(p['rw_lnx_b']), gs, gb, bf(p['rw_wo']), g, b], [(D_MODEL, F32)])[0]
    return x_new, st_p, xr[:, -1], st_s, xs[:, -1]


def _s5_disc_body(lre_ref, lim_ref, dt_ref, bre_ref, bim_ref, lre_s, lim_s, dt_s, bbr_o, bbi_o, ar_o, ai_o):
    def disc(lre, lim, dt):
        mag = jnp.exp(lre * dt)
        ab_re, ab_im = mag * jnp.cos(lim * dt), mag * jnp.sin(lim * dt)
        den = lre * lre + lim * lim
        nr = ab_re - 1.0
        return ab_re, ab_im, (nr * lre + ab_im * lim) / den, (ab_im * lre - nr * lim) / den

    _, _, co_re, co_im = disc(lre_ref[...], lim_ref[...], dt_ref[...])
    b_re, b_im = bre_ref[...], bim_ref[...]
    bbr_o[...] = co_re * b_re - co_im * b_im
    bbi_o[...] = co_re * b_im + co_im * b_re
    ab_re, ab_im, _, _ = disc(lre_s[...], lim_s[...], dt_s[...])
    ar_o[...] = ab_re
    ai_o[...] = ab_im


def _s5_input(u, bre_ref, bim_ref):
    ub = u.astype(BF16)
    w = S5_WIDTH // S5_BLOCKS
    cw = D_MODEL // S5_BLOCKS
    re = [_dot(ub[:, j * cw:(j + 1) * cw], bre_ref[j]) for j in range(S5_BLOCKS)]
    im = [_dot(ub[:, j * cw:(j + 1) * cw], bim_ref[j]) for j in range(S5_BLOCKS)]
    del w
    return re, im


def _s5_output(u, hre, him, cre_ref, cim_ref, dv_ref, wv_ref, wg_ref, g_ref, b_ref):
    w = S5_WIDTH // S5_BLOCKS
    hb_re, hb_im = hre.astype(BF16), him.astype(BF16)
    ys = [_dot(hb_re[:, j * w:(j + 1) * w], cre_ref[j]) - _dot(hb_im[:, j * w:(j + 1) * w], cim_ref[j])
          for j in range(S5_BLOCKS)]
    y = jnp.concatenate(ys, 1) + dv_ref[...] * u
    z = jax.nn.gelu(y).astype(BF16)
    out = _dot(z, wv_ref[...]) * _sigmoid(_dot(z, wg_ref[...]))
    return _layer_norm(ALPHA * u + out, g_ref[...], b_ref[...])


def _s5_seq_body(x_ref, h0r_ref, h0i_ref, ar_ref, ai_ref, bre_ref, bim_ref, cre_ref, cim_ref, dv_ref, wv_ref, wg_ref,
                 g_ref, b_ref, o_ref, htr_ref, hti_ref, hre_sc, him_sc, sr_sc, si_sc, *, tt):
    ti = pl.program_id(1)

    @pl.when(ti == 0)
    def _init():
        sr_sc[...] = h0r_ref[0]
        si_sc[...] = h0i_ref[0]

    u = x_ref[...]
    re, im = _s5_input(u, bre_ref, bim_ref)
    w = S5_WIDTH // S5_BLOCKS
    for j in range(S5_BLOCKS):
        hre_sc[:, j * w:(j + 1) * w] = re[j]
        him_sc[:, j * w:(j + 1) * w] = im[j]
    ar, ai = ar_ref[...], ai_ref[...]

    def step(t, carry):
        hr, hi = carry
        nr = ar * hr - ai * hi + hre_sc[pl.ds(t, 1), :]
        ni = ar * hi + ai * hr + him_sc[pl.ds(t, 1), :]
        hre_sc[pl.ds(t, 1), :] = nr
        him_sc[pl.ds(t, 1), :] = ni
        return nr, ni

    hr, hi = lax.fori_loop(0, tt, step, (sr_sc[...], si_sc[...]))
    sr_sc[...] = hr
    si_sc[...] = hi
    o_ref[...] = _s5_output(u, hre_sc[...], him_sc[...], cre_ref, cim_ref, dv_ref, wv_ref, wg_ref, g_ref, b_ref)

    @pl.when(ti == pl.num_programs(1) - 1)
    def _fin():
        htr_ref[0] = hr
        hti_ref[0] = hi


def _s5_step_body(x_ref, h0r_ref, h0i_ref, ar_ref, ai_ref, bre_ref, bim_ref, cre_ref, cim_ref, dv_ref, wv_ref, wg_ref,
                  g_ref, b_ref, o_ref, htr_ref, hti_ref, sr_sc, si_sc):
    t = pl.program_id(0)

    @pl.when(t == 0)
    def _init():
        sr_sc[...] = h0r_ref[...]
        si_sc[...] = h0i_ref[...]

    u = x_ref[0]
    re, im = _s5_input(u, bre_ref, bim_ref)
    bur, bui = jnp.concatenate(re, 1), jnp.concatenate(im, 1)
    ar, ai = ar_ref[...], ai_ref[...]
    hr, hi = sr_sc[...], si_sc[...]
    nr = ar * hr - ai * hi + bur
    ni = ar * hi + ai * hr + bui
    sr_sc[...] = nr
    si_sc[...] = ni
    o_ref[0] = _s5_output(u, nr, ni, cre_ref, cim_ref, dv_ref, wv_ref, wg_ref, g_ref, b_ref)

    @pl.when(t == pl.num_programs(0) - 1)
    def _fin():
        htr_ref[...] = nr
        hti_ref[...] = ni


def _s5_weights(m, p):
    rep = lambda a: jnp.repeat(a, S5_GROUP, axis=-1)
    lre, lim = p['s5_lam_re'][m], p['s5_lam_im'][m]
    dt = jnp.broadcast_to(jnp.exp(p['s5_log_dt'][m])[:, None], (S5_GROUPS, S5_STATE))
    flat = lambda a: a[m].reshape(S5_GROUPS, S5_STATE * S5_GROUP)
    n = S5_STATE * S5_GROUP
    bb_re, bb_im, ab_re, ab_im = pl.pallas_call(
        _s5_disc_body,
        out_shape=[jax.ShapeDtypeStruct((S5_GROUPS, n), F32)] * 2 + [jax.ShapeDtypeStruct((S5_GROUPS, S5_STATE), F32)] * 2,
        name="s5_disc")(rep(lre), rep(lim), rep(dt), flat(p['s5_b_re']), flat(p['s5_b_im']), lre, lim, dt)
    gpb = S5_GROUPS // S5_BLOCKS
    eye = jnp.eye(gpb, dtype=F32)

    def in_blocks(bb):
        t = bb.reshape(S5_BLOCKS, gpb, S5_STATE, S5_GROUP).transpose(0, 1, 3, 2)
        return jnp.einsum('jgsp,gh->jgshp', t, eye).reshape(S5_BLOCKS, gpb * S5_GROUP, gpb * S5_STATE).astype(BF16)

    def out_blocks(c):
        t = c.reshape(S5_BLOCKS, gpb, S5_GROUP, S5_STATE).transpose(0, 1, 3, 2)
        return jnp.einsum('jgps,gh->jgphs', t, eye).reshape(S5_BLOCKS, gpb * S5_STATE, gpb * S5_GROUP).astype(BF16)

    return dict(ar=ab_re.reshape(1, S5_WIDTH), ai=ab_im.reshape(1, S5_WIDTH),
                bre=in_blocks(bb_re), bim=in_blocks(bb_im),
                cre=out_blocks(p['s5_c_re'][m]), cim=out_blocks(p['s5_c_im'][m]),
                dv=p['s5_d'][m][None], wv=p['s5_wv'][m].astype(BF16), wg=p['s5_wg'][m].astype(BF16))


def _s5_seq(x, h0r, h0i, w, g, b, n_seq, seq_len, row_off, tt):
    nt = seq_len // tt
    off = row_off // tt
    consts = [w["ar"], w["ai"], w["bre"], w["bim"], w["cre"], w["cim"], w["dv"], w["wv"], w["wg"], g, b]
    st_spec = pl.BlockSpec((1, 1, S5_WIDTH), lambda bb, t: (bb, 0, 0))
    return pl.pallas_call(
        functools.partial(_s5_seq_body, tt=tt), grid=(n_seq, nt),
        in_specs=[pl.BlockSpec((tt, D_MODEL), lambda bb, t: (off + bb * nt + t, 0)), st_spec, st_spec]
        + [_const_spec(a, 2) for a in consts],
        out_specs=[pl.BlockSpec((tt, D_MODEL), lambda bb, t: (bb * nt + t, 0)), st_spec, st_spec],
        out_shape=[jax.ShapeDtypeStruct((n_seq * seq_len, D_MODEL), F32)]
        + [jax.ShapeDtypeStruct((n_seq, 1, S5_WIDTH), F32)] * 2,
        scratch_shapes=[pltpu.VMEM((tt, S5_WIDTH), F32), pltpu.VMEM((tt, S5_WIDTH), F32),
                        pltpu.VMEM((1, S5_WIDTH), F32), pltpu.VMEM((1, S5_WIDTH), F32)],
        compiler_params=_params("parallel", "arbitrary"), name=f"s5_seq{tt}")(x, h0r, h0i, *consts)


def _s5_layer(x, lay, m, p, re_s, im_s, g, b):
    n_seq, seq, n_dec, t_dec = lay["B"], lay["S"], lay["Bd"], lay["Td"]
    soff, moff = lay["soff"], lay["moff"]
    n_rows = n_dec * t_dec
    w = _s5_weights(m, p)
    zero = jnp.zeros((n_seq, 1, S5_WIDTH), F32)
    o_meta, hr, hi = _s5_seq(x, zero, zero, w, g, b, n_seq, N_META, moff, N_META)
    o_real, hr, hi = _s5_seq(x, hr, hi, w, g, b, n_seq, seq, 0, min(S5_TILE, seq))
    xs = x[soff:soff + n_rows].reshape(n_dec, t_dec, D_MODEL).transpose(1, 0, 2)
    consts = [w["ar"], w["ai"], w["bre"], w["bim"], w["cre"], w["cim"], w["dv"], w["wv"], w["wg"], g, b]
    st_spec = pl.BlockSpec((n_dec, S5_WIDTH), lambda t: (0, 0))
    o_s, sr, si = pl.pallas_call(
        _s5_step_body, grid=(t_dec,),
        in_specs=[pl.BlockSpec((1, n_dec, D_MODEL), lambda t: (t, 0, 0)), st_spec, st_spec]
        + [_const_spec(a, 1) for a in consts],
        out_specs=[pl.BlockSpec((1, n_dec, D_MODEL), lambda t: (t, 0, 0)), st_spec, st_spec],
        out_shape=[jax.ShapeDtypeStruct((t_dec, n_dec, D_MODEL), F32)]
        + [jax.ShapeDtypeStruct((n_dec, S5_WIDTH), F32)] * 2,
        scratch_shapes=[pltpu.VMEM((n_dec, S5_WIDTH), F32), pltpu.VMEM((n_dec, S5_WIDTH), F32)],
        compiler_params=_params("arbitrary"), name="s5_step")(
            xs, re_s[m].reshape(n_dec, S5_WIDTH), im_s[m].reshape(n_dec, S5_WIDTH), *consts)
    o_s = o_s.transpose(1, 0, 2).reshape(n_rows, D_MODEL)
    x_new = _assemble(lay, o_real, o_s, o_meta)
    shp = (S5_GROUPS, S5_STATE)
    return x_new, hr.reshape(n_seq, *shp), hi.reshape(n_seq, *shp), sr.reshape(n_dec, *shp), si.reshape(n_dec, *shp)


def _round_up(n, m):
    return (n + m - 1) // m * m


def kernel(x_prompt, x_sample, cache_mla_latent, cache_mla_krope, state_rwkv_wkv, state_rwkv_shift, state_s5_re, state_s5_im, page_table, meta_tokens, ln_g, ln_b, ffn_w1, ffn_w3, ffn_w2, mla_w_dq, mla_q_norm, mla_w_uq, mla_w_dkv, mla_kv_norm, mla_w_uk, mla_w_uv, mla_w_o, rw_mu, rw_wr, rw_wk, rw_wv, rw_w0, rw_w1, rw_w2, rw_a0, rw_a1, rw_a2, rw_g1, rw_g2, rw_k_k, rw_k_a, rw_r_k, rw_lnx_g, rw_lnx_b, rw_wo, s5_lam_re, s5_lam_im, s5_log_dt, s5_b_re, s5_b_im, s5_c_re, s5_c_im, s5_d, s5_wv, s5_wg):
    p = dict(rw_mu=rw_mu, rw_wr=rw_wr, rw_wk=rw_wk, rw_wv=rw_wv, rw_w0=rw_w0, rw_w1=rw_w1, rw_w2=rw_w2,
             rw_a0=rw_a0, rw_a1=rw_a1, rw_a2=rw_a2, rw_g1=rw_g1, rw_g2=rw_g2, rw_k_k=rw_k_k, rw_k_a=rw_k_a,
             rw_r_k=rw_r_k, rw_lnx_g=rw_lnx_g, rw_lnx_b=rw_lnx_b, rw_wo=rw_wo,
             s5_lam_re=s5_lam_re, s5_lam_im=s5_lam_im, s5_log_dt=s5_log_dt, s5_b_re=s5_b_re, s5_b_im=s5_b_im,
             s5_c_re=s5_c_re, s5_c_im=s5_c_im, s5_d=s5_d, s5_wv=s5_wv, s5_wg=s5_wg)
    n_seq, seq, _ = x_prompt.shape
    n_dec, t_dec, _ = x_sample.shape
    past = page_table.shape[1] * cache_mla_latent.shape[2]
    n_rows = n_dec * t_dec
    soff = _round_up(n_seq * seq, ROW_ALIGN)
    moff = _round_up(soff + n_rows, ROW_ALIGN)
    n_tot = _round_up(moff + n_seq * N_META, ROW_ALIGN)
    lay = dict(B=n_seq, S=seq, Bd=n_dec, Td=t_dec, soff=soff, moff=moff, n_tot=n_tot)
    assemble = functools.partial(_assemble, lay)

    meta = jnp.broadcast_to(meta_tokens[None], (n_seq, N_META, D_MODEL)).reshape(n_seq * N_META, D_MODEL)
    x = assemble(x_prompt.reshape(n_seq * seq, D_MODEL), x_sample.reshape(n_rows, D_MODEL), meta)
    pos = assemble(jnp.tile(N_META + jnp.arange(seq, dtype=jnp.int32), n_seq)[:, None],
                   jnp.tile(past + jnp.arange(t_dec, dtype=jnp.int32), n_dec)[:, None],
                   jnp.tile(jnp.arange(N_META, dtype=jnp.int32), n_seq)[:, None])[:, 0]
    tabs = _rope_tables(pos)

    lat_p, kr_p, lat_s, kr_s = [], [], [], []
    wkv_p = sh_p = wkv_s = sh_s = re_p = im_p = re_s = im_s = None
    for i in range(DEPTH):
        kind, m = i % 3, i // 3
        norm = lambda j: (ln_g[i, j][None], ln_b[i, j][None])
        ffn = lambda xx, j, nj: _ffn(xx, ffn_w1[i, j].astype(BF16), ffn_w3[i, j].astype(BF16),
                                     ffn_w2[i, j].astype(BF16), *norm(nj))
        x = ffn(x, 0, 0)
        if kind == 0:
            w = _mla_weights(m, mla_w_dq, mla_q_norm, mla_w_uq, mla_w_dkv, mla_kv_norm, mla_w_uk, mla_w_uv, mla_w_o)
            x, c, kpe = _mla_layer(x, lay, tabs, w, m, cache_mla_latent, cache_mla_krope, page_table, *norm(1))
            split = lambda a: (jnp.concatenate(
                [a[moff:moff + n_seq * N_META].reshape(n_seq, N_META, -1), a[:n_seq * seq].reshape(n_seq, seq, -1)], 1),
                a[soff:soff + n_rows].reshape(n_dec, t_dec, -1))
            cp, cs = split(c)
            kp, ks = split(kpe)
            lat_p.append(cp), lat_s.append(cs), kr_p.append(kp), kr_s.append(ks)
        elif kind == 1:
            x, wkv_p, sh_p, wkv_s, sh_s = _rwkv_layer(x, lay, m, p, state_rwkv_shift, state_rwkv_wkv, *norm(1))
        else:
            x, re_p, im_p, re_s, im_s = _s5_layer(x, lay, m, p, state_s5_re, state_s5_im, *norm(1))
        x = ffn(x, 1, 2)

    y_prompt = x[:n_seq * seq].reshape(n_seq, seq, D_MODEL)
    y_sample = x[soff:soff + n_rows].reshape(n_dec, t_dec, D_MODEL)
    return (y_prompt, y_sample, jnp.stack(lat_p), jnp.stack(kr_p), jnp.stack(lat_s), jnp.stack(kr_s),
            wkv_p[None], sh_p[None], wkv_s[None], sh_s[None], re_p[None], im_p[None], re_s[None], im_s[None])
```

```python
import functools
import math

import jax
import jax.numpy as jnp
from jax import lax
from jax.experimental import pallas as pl
from jax.experimental.pallas import tpu as pltpu

F32 = jnp.float32
BF16 = jnp.bfloat16
HI = lax.Precision.HIGHEST

D_MODEL = 1024
DEPTH = 4
N_META = 16
D_FF = 2816
ALPHA = (2 * DEPTH) ** 0.25
LN_EPS = 1e-5
RMS_EPS = 1e-6
MLA_HEADS = 16
Q_RANK = 768
KV_RANK = 256
NOPE_DIM = 64
ROPE_DIM = 32
QK_DIM = NOPE_DIM + ROPE_DIM
V_DIM = 64
ROPE_THETA = 10000.0
RW_HEAD = 64
RW_HEADS = D_MODEL // RW_HEAD
GN_EPS = 64e-5
S5_GROUP = 16
S5_GROUPS = D_MODEL // S5_GROUP
S5_STATE = 64
S5_WIDTH = S5_GROUPS * S5_STATE
S5_BLOCKS = 8

LANES = 128
HEAD_PAD = 128
ROW_TILE = 256
ROW_ALIGN = 512
ATTN_TILE = 1024
ATTN_SUB = 128
ATTN_AHEAD = 4
DECODE_PAGES = 64
RW_CHUNK = 64
RW_GROUP = 4
S5_TILE = 256
VMEM_LIMIT = 56 * 1024 * 1024
NEG = -1e30


def _dot(a, b, prec=None):
    return jnp.dot(a, b, preferred_element_type=F32, precision=prec)


def _dot_nt(a, b, prec=None):
    return lax.dot_general(a, b, (((1,), (1,)), ((), ())), preferred_element_type=F32, precision=prec)


def _dot_tn(a, b, prec=None):
    return lax.dot_general(a, b, (((0,), (0,)), ((), ())), preferred_element_type=F32, precision=prec)


def _layer_norm(y, g, b):
    mu = jnp.mean(y, -1, keepdims=True)
    yc = y - mu
    var = jnp.mean(yc * yc, -1, keepdims=True)
    return yc * lax.rsqrt(var + LN_EPS) * g + b


def _rms_norm(y, g):
    return y * lax.rsqrt(jnp.mean(y * y, -1, keepdims=True) + RMS_EPS) * g


def _sigmoid(x):
    return 1.0 / (1.0 + jnp.exp(-x))


def _params(*sem):
    return pltpu.CompilerParams(dimension_semantics=sem, vmem_limit_bytes=VMEM_LIMIT)


def _const_spec(a, n_grid):
    zeros = (0,) * a.ndim
    if n_grid == 1:
        imap = lambda i: zeros
    elif n_grid == 2:
        imap = lambda i, j: zeros
    else:
        imap = lambda i, j, k, l: zeros
    return pl.BlockSpec(a.shape, imap, pipeline_mode=pl.Buffered(1))


def _rows_call(body, name, row_ins, const_ins, outs, tm=ROW_TILE):
    n = row_ins[0].shape[0]
    assert n % tm == 0
    in_specs = [pl.BlockSpec((tm, a.shape[1]), lambda i: (i, 0)) for a in row_ins]
    in_specs += [_const_spec(a, 1) for a in const_ins]
    out_specs = [pl.BlockSpec((tm, w), lambda i: (i, 0)) for w, _ in outs]
    out_shape = [jax.ShapeDtypeStruct((n, w), dt) for w, dt in outs]
    res = pl.pallas_call(body, grid=(n // tm,), in_specs=in_specs, out_specs=out_specs, out_shape=out_shape,
                         compiler_params=_params("parallel"), name=name)(*row_ins, *const_ins)
    return res


def _assemble(lay, real, samp, meta):
    w = real.shape[1]
    gap = lambda n: jnp.zeros((n, w), real.dtype)
    return jnp.concatenate([real, gap(lay["soff"] - real.shape[0]), samp,
                            gap(lay["moff"] - lay["soff"] - samp.shape[0]), meta,
                            gap(lay["n_tot"] - lay["moff"] - meta.shape[0])], 0)


FF_CHUNK = 1408


def _ffn_body(x_ref, w1_ref, w3_ref, w2_ref, g_ref, b_ref, o_ref):
    x = x_ref[...]
    xb = x.astype(BF16)
    chunks = [slice(c * FF_CHUNK, (c + 1) * FF_CHUNK) for c in range(D_FF // FF_CHUNK)]
    ups = [(_dot(xb, w1_ref[:, sl]), _dot(xb, w3_ref[:, sl])) for sl in chunks]
    gates = [(h1 * _sigmoid(h1) * h3).astype(BF16) for h1, h3 in ups]
    acc = None
    for sl, gate in zip(chunks, gates):
        part = _dot(gate, w2_ref[sl, :])
        acc = part if acc is None else acc + part
    o_ref[...] = _layer_norm(ALPHA * x + 0.5 * acc, g_ref[...], b_ref[...])


def _ffn(x, w1, w3, w2, g, b):
    return _rows_call(_ffn_body, "ffn", [x], [w1, w3, w2, g, b], [(D_MODEL, F32)])[0]


def _mla_proj_body(x_ref, ct_ref, s1_ref, s2_ref, wdq_ref, qn_ref, wq_ref, wdkv_ref, kvn_ref, wk_ref, wv_ref,
                   q_ref, k_ref, v_ref, c_ref, kpe_ref):
    xb = x_ref[...].astype(BF16)
    ct, s1, s2 = ct_ref[...], s1_ref[...], s2_ref[...]
    dq = _dot(xb, wdq_ref[...])
    ckv = _dot(xb, wdkv_ref[...])
    cq = _rms_norm(dq, qn_ref[...])
    c = _rms_norm(ckv[:, :KV_RANK], kvn_ref[...])
    q = _dot(cq.astype(BF16), wq_ref[...])
    half = ROPE_DIM // 2
    k1, k2 = ckv[:, KV_RANK:KV_RANK + half], ckv[:, KV_RANK + half:]
    cos, sin = ct[:, NOPE_DIM:NOPE_DIM + half], s1[:, NOPE_DIM + half:QK_DIM]
    kpe = jnp.concatenate([k1 * cos - k2 * sin, k1 * sin + k2 * cos], -1)
    c_ref[...] = c
    kpe_ref[...] = kpe
    cb = c.astype(BF16)
    knope = _dot(cb, wk_ref[...])
    v_ref[...] = _dot(cb, wv_ref[...]).astype(BF16)
    tm = kpe.shape[0]
    kpe_slot = jnp.concatenate([jnp.zeros((tm, NOPE_DIM), F32), kpe, jnp.zeros((tm, HEAD_PAD - QK_DIM), F32)], -1)
    scale = QK_DIM ** -0.5 * math.log2(math.e)
    for h in range(MLA_HEADS):
        sl = slice(h * HEAD_PAD, (h + 1) * HEAD_PAD)
        blk = q[:, sl]
        rot = blk * ct + pltpu.roll(blk, half, 1) * s1 + pltpu.roll(blk, HEAD_PAD - half, 1) * s2
        q_ref[:, sl] = (rot * scale).astype(BF16)
        k_ref[:, sl] = (knope[:, sl] + kpe_slot).astype(BF16)


def _flash_body(qt_ref, kt_ref, *refs, tq, tk, sub, ahead, has_lead):
    if has_lead:
        q_ref, k_ref, v_ref, mk_ref, mv_ref, o_ref, m_sc, l_sc, acc_sc = refs
    else:
        q_ref, k_ref, v_ref, o_ref, m_sc, l_sc, acc_sc = refs
    t = pl.program_id(2)
    qi, ki = qt_ref[t], kt_ref[t]
    blocks = [(hh, r0) for r0 in range(0, tq, sub) for hh in range(2)]

    def q_blk(hh, r0):
        return q_ref[r0:r0 + sub, hh * HEAD_PAD:(hh + 1) * HEAD_PAD]

    def weights(hh, r0, s, first):
        rows = slice(r0, r0 + sub)
        width = s.shape[1]
        m_cur = jnp.max(s, -1, keepdims=True)
        if first:
            m_new = jnp.broadcast_to(m_cur, (sub, LANES))
        else:
            m_old = m_sc[hh, rows, :]
            m_new = jnp.maximum(m_old, m_cur)
        m_wide = jnp.concatenate([m_new] * (width // LANES), 1) if width >= LANES else m_new[:, :width]
        p = jnp.exp2(s - m_wide)
        p_sum = jnp.sum(p, -1, keepdims=True)
        if first:
            corr = None
            l_sc[hh, rows, :] = jnp.broadcast_to(p_sum, (sub, LANES))
        else:
            corr = jnp.exp2(m_old - m_new)
            l_sc[hh, rows, :] = corr * l_sc[hh, rows, :] + p_sum
        m_sc[hh, rows, :] = m_new
        return p.astype(BF16), corr

    def attend(score_fn, vals, first=False):
        scores = {}
        for i in range(len(blocks) + ahead):
            if i < len(blocks):
                scores[i] = score_fn(*blocks[i])
            if i >= ahead:
                hh, r0 = blocks[i - ahead]
                p, corr = weights(hh, r0, scores.pop(i - ahead), first)
                rows = slice(r0, r0 + sub)
                pv = _dot(p, vals[:p.shape[1]])
                acc_sc[hh, rows, :] = pv if first else corr * acc_sc[hh, rows, :] + pv

    @pl.when(ki == 0)
    def _init():
        if has_lead:
            attend(lambda hh, r0: _dot_nt(q_blk(hh, r0), mk_ref[:, hh * HEAD_PAD:(hh + 1) * HEAD_PAD]),
                   mv_ref[...], first=True)
        else:
            m_sc[...] = jnp.full(m_sc.shape, NEG, F32)
            l_sc[...] = jnp.zeros(l_sc.shape, F32)
            acc_sc[...] = jnp.zeros(acc_sc.shape, F32)

    def step(masked):
        def score(hh, r0):
            kw = r0 + sub if masked else tk
            s = _dot_nt(q_blk(hh, r0), k_ref[:kw, hh * HEAD_PAD:(hh + 1) * HEAD_PAD])
            if masked:
                row = r0 + lax.broadcasted_iota(jnp.int32, (sub, kw), 0)
                col = lax.broadcasted_iota(jnp.int32, (sub, kw), 1)
                s = jnp.where(col <= row, s, NEG)
            return s
        attend(score, v_ref[...])

    @pl.when(ki < qi)
    def _below():
        step(False)

    @pl.when(ki == qi)
    def _diag():
        step(True)
        lane = lax.broadcasted_iota(jnp.int32, (tq, 2 * V_DIM), 1)
        o0 = acc_sc[0] / l_sc[0]
        o1 = acc_sc[1] / l_sc[1]
        o_ref[...] = jnp.where(lane < V_DIM, o0, o1).astype(BF16)


def _flash(q, k, v, n_seq, seq_len, row_off, tile, lead_off=None):
    tq = tk = tile
    nq = seq_len // tq
    qo = row_off // tq
    has_lead = lead_off is not None
    pairs = [(i, j) for i in range(nq) for j in range(i + 1)]
    qt = jnp.array([i for i, _ in pairs], jnp.int32)
    kt = jnp.array([j for _, j in pairs], jnp.int32)
    in_specs = [
        pl.BlockSpec((tq, 2 * HEAD_PAD), lambda b, h, t, qt, kt: (qo + b * nq + qt[t], h)),
        pl.BlockSpec((tk, 2 * HEAD_PAD), lambda b, h, t, qt, kt: (qo + b * nq + kt[t], h)),
        pl.BlockSpec((tk, 2 * V_DIM), lambda b, h, t, qt, kt: (qo + b * nq + kt[t], h)),
    ]
    args = [q, k, v]
    if has_lead:
        lo = lead_off // N_META
        in_specs += [pl.BlockSpec((N_META, 2 * HEAD_PAD), lambda b, h, t, qt, kt: (lo + b, h)),
                     pl.BlockSpec((N_META, 2 * V_DIM), lambda b, h, t, qt, kt: (lo + b, h))]
        args += [k, v]
    return pl.pallas_call(
        functools.partial(_flash_body, tq=tq, tk=tk, sub=min(ATTN_SUB, tq), ahead=ATTN_AHEAD, has_lead=has_lead),
        grid_spec=pltpu.PrefetchScalarGridSpec(
            num_scalar_prefetch=2, grid=(n_seq, MLA_HEADS // 2, len(pairs)),
            in_specs=in_specs,
            out_specs=pl.BlockSpec((tq, 2 * V_DIM), lambda b, h, t, qt, kt: (b * nq + qt[t], h)),
            scratch_shapes=[pltpu.VMEM((2, tq, LANES), F32), pltpu.VMEM((2, tq, LANES), F32),
                            pltpu.VMEM((2, tq, 2 * V_DIM), F32)]),
        out_shape=jax.ShapeDtypeStruct((n_seq * seq_len, MLA_HEADS * V_DIM), BF16),
        compiler_params=_params("parallel", "parallel", "arbitrary"),
        name="mla_flash_lead" if has_lead else "mla_flash")(qt, kt, *args)


def _qlat_body(q_ref, w_ref, o_ref):
    o_ref[0] = _dot(q_ref[...], w_ref[0]).astype(BF16)


def _decode_body(pt_ref, ql_ref, qp_ref, *refs, t_new, n_pg):
    lat_refs, kr_refs = refs[:n_pg], refs[n_pg:2 * n_pg]
    cn_ref, kn_ref, o_ref, m_sc, l_sc, acc_sc = refs[2 * n_pg:]
    j = pl.program_id(1)
    ql, qp = ql_ref[0], qp_ref[0]

    @pl.when(j == 0)
    def _init():
        m_sc[...] = jnp.full(m_sc.shape, NEG, F32)
        l_sc[...] = jnp.zeros(l_sc.shape, F32)
        acc_sc[...] = jnp.zeros(acc_sc.shape, F32)

    def update(s, vals):
        m_old = m_sc[...]
        m_new = jnp.maximum(m_old, jnp.max(s, -1, keepdims=True))
        p = jnp.exp2(s - m_new)
        corr = jnp.exp2(m_old - m_new)
        l_sc[...] = corr * l_sc[...] + jnp.sum(p, -1, keepdims=True)
        pb = p.astype(BF16)
        acc = corr * acc_sc[...]
        off = 0
        for blk in vals:
            acc = acc + _dot(pb[:, off:off + blk.shape[0]], blk)
            off += blk.shape[0]
        acc_sc[...] = acc
        m_sc[...] = m_new

    lats = [r[0, 0].astype(BF16) for r in lat_refs]
    s = jnp.concatenate([_dot_nt(ql, lat) + _dot(qp, kr[0, 0].astype(BF16)) for lat, kr in zip(lats, kr_refs)], 1)
    update(s, lats)

    @pl.when(j == pl.num_programs(1) - 1)
    def _fin():
        cn = cn_ref[0].astype(BF16)
        kn = kn_ref[0].astype(BF16)
        s = _dot_nt(ql, cn) + _dot_nt(qp, kn)
        rows = s.shape[0]
        t_q = lax.rem(lax.broadcasted_iota(jnp.int32, (rows, t_new), 0), t_new)
        t_k = lax.broadcasted_iota(jnp.int32, (rows, t_new), 1)
        update(jnp.where(t_k <= t_q, s, NEG), [cn])
        o_ref[0] = (acc_sc[...] / l_sc[...]).astype(BF16)


def _ouv_body(o_ref, w_ref, out_ref):
    out_ref[...] = _dot(o_ref[...], w_ref[0]).astype(BF16)


def _mix_out_body(x_ref, h_ref, w_ref, g_ref, b_ref, o_ref):
    x = x_ref[...]
    o_ref[...] = _layer_norm(ALPHA * x + _dot(h_ref[...], w_ref[...]), g_ref[...], b_ref[...])


def _rope_tables(pos):
    half = ROPE_DIM // 2
    inv = 1.0 / (ROPE_THETA ** (jnp.arange(half, dtype=F32) * (2.0 / ROPE_DIM)))
    ang = pos.astype(F32)[:, None] * inv
    c, s = jnp.cos(ang), jnp.sin(ang)
    n = pos.shape[0]
    one, zero = jnp.ones((n, NOPE_DIM), F32), jnp.zeros((n, NOPE_DIM), F32)
    pad = jnp.zeros((n, HEAD_PAD - QK_DIM), F32)
    z16 = jnp.zeros((n, half), F32)
    ct = jnp.concatenate([one, c, c, pad], 1)
    s1 = jnp.concatenate([zero, z16, s, pad], 1)
    s2 = jnp.concatenate([zero, -s, z16, pad], 1)
    return ct, s1, s2


def _mla_weights(m, w_dq, q_norm, w_uq, w_dkv, kv_norm, w_uk, w_uv, w_o):
    half = ROPE_DIM // 2
    wq = jnp.pad(w_uq[m], ((0, 0), (0, 0), (0, HEAD_PAD - QK_DIM))).reshape(Q_RANK, MLA_HEADS * HEAD_PAD)
    wk = jnp.pad(w_uk[m], ((0, 0), (0, 0), (0, HEAD_PAD - NOPE_DIM))).reshape(KV_RANK, MLA_HEADS * HEAD_PAD)
    wv = w_uv[m].reshape(KV_RANK, MLA_HEADS * V_DIM)
    wukt = jnp.pad(jnp.transpose(w_uk[m], (1, 2, 0)), ((0, 0), (0, HEAD_PAD - NOPE_DIM), (0, 0)))
    wuv_h = jnp.transpose(w_uv[m], (1, 0, 2)).reshape(MLA_HEADS // 2, 2, KV_RANK, V_DIM)
    eye2 = jnp.eye(2, dtype=F32)
    wuv_pair = jnp.einsum('pirv,ij->pirjv', wuv_h, eye2).reshape(MLA_HEADS // 2, 2 * KV_RANK, 2 * V_DIM)
    del half
    return dict(wdq=w_dq[m].astype(BF16), qn=q_norm[m][None], wq=wq.astype(BF16), wdkv=w_dkv[m].astype(BF16),
                kvn=kv_norm[m][None], wk=wk.astype(BF16), wv=wv.astype(BF16), wukt=wukt.astype(BF16),
                wuv_pair=wuv_pair.astype(BF16), wo=w_o[m].reshape(MLA_HEADS * V_DIM, D_MODEL).astype(BF16))


def _mla_layer(x, lay, tabs, w, m, lat_pool, kr_pool, page_table, g, b):
    n_seq, seq, n_dec, t_dec = lay["B"], lay["S"], lay["Bd"], lay["Td"]
    q, k, v, c, kpe = _rows_call(
        _mla_proj_body, "mla_proj", [x, *tabs],
        [w["wdq"], w["qn"], w["wq"], w["wdkv"], w["kvn"], w["wk"], w["wv"]],
        [(MLA_HEADS * HEAD_PAD, BF16), (MLA_HEADS * HEAD_PAD, BF16), (MLA_HEADS * V_DIM, BF16),
         (KV_RANK, F32), (ROPE_DIM, F32)])
    soff, moff = lay["soff"], lay["moff"]
    o_real = _flash(q, k, v, n_seq, seq, 0, min(ATTN_TILE, seq), lead_off=moff)
    o_meta = _flash(q, k, v, n_seq, N_META, moff, N_META)

    n_rows = n_dec * t_dec
    hq = MLA_HEADS * t_dec
    q_lat = pl.pallas_call(
        _qlat_body, grid=(MLA_HEADS,),
        in_specs=[pl.BlockSpec((n_rows, HEAD_PAD), lambda h: (soff // n_rows, h)),
                  pl.BlockSpec((1, HEAD_PAD, KV_RANK), lambda h: (h, 0, 0))],
        out_specs=pl.BlockSpec((1, n_rows, KV_RANK), lambda h: (h, 0, 0)),
        out_shape=jax.ShapeDtypeStruct((MLA_HEADS, n_rows, KV_RANK), BF16),
        compiler_params=_params("parallel"), name="mla_qlat")(q, w["wukt"])
    q_lat = q_lat.reshape(MLA_HEADS, n_dec, t_dec, KV_RANK).transpose(1, 0, 2, 3).reshape(n_dec, hq, KV_RANK)
    q_s = q[soff:soff + n_rows].reshape(n_dec, t_dec, MLA_HEADS, HEAD_PAD)[..., NOPE_DIM:QK_DIM]
    q_pe = q_s.transpose(0, 2, 1, 3).reshape(n_dec, hq, ROPE_DIM)
    c_new = c[soff:soff + n_rows].reshape(n_dec, t_dec, KV_RANK)
    k_new = kpe[soff:soff + n_rows].reshape(n_dec, t_dec, ROPE_DIM)
    n_pages, page = page_table.shape[1], lat_pool.shape[2]
    n_pg = math.gcd(DECODE_PAGES, n_pages)
    kr_pool_t = jnp.swapaxes(kr_pool, 2, 3)
    lat_spec = lambda i: pl.BlockSpec((1, 1, page, KV_RANK), lambda bb, j, pt: (m, pt[bb, j * n_pg + i], 0, 0))
    kr_spec = lambda i: pl.BlockSpec((1, 1, ROPE_DIM, page), lambda bb, j, pt: (m, pt[bb, j * n_pg + i], 0, 0))
    o_lat = pl.pallas_call(
        functools.partial(_decode_body, t_new=t_dec, n_pg=n_pg),
        grid_spec=pltpu.PrefetchScalarGridSpec(
            num_scalar_prefetch=1, grid=(n_dec, n_pages // n_pg),
            in_specs=[pl.BlockSpec((1, hq, KV_RANK), lambda bb, j, pt: (bb, 0, 0)),
                      pl.BlockSpec((1, hq, ROPE_DIM), lambda bb, j, pt: (bb, 0, 0))]
            + [lat_spec(i) for i in range(n_pg)] + [kr_spec(i) for i in range(n_pg)]
            + [pl.BlockSpec((1, t_dec, KV_RANK), lambda bb, j, pt: (bb, 0, 0)),
               pl.BlockSpec((1, t_dec, ROPE_DIM), lambda bb, j, pt: (bb, 0, 0))],
            out_specs=pl.BlockSpec((1, hq, KV_RANK), lambda bb, j, pt: (bb, 0, 0)),
            scratch_shapes=[pltpu.VMEM((hq, 1), F32), pltpu.VMEM((hq, 1), F32), pltpu.VMEM((hq, KV_RANK), F32)]),
        out_shape=jax.ShapeDtypeStruct((n_dec, hq, KV_RANK), BF16),
        compiler_params=_params("parallel", "arbitrary"), name="mla_decode")(
            page_table, q_lat, q_pe, *([lat_pool] * n_pg), *([kr_pool_t] * n_pg), c_new, k_new)
    o_lat = o_lat.reshape(n_dec, MLA_HEADS, t_dec, KV_RANK).transpose(0, 2, 1, 3).reshape(n_rows, MLA_HEADS * KV_RANK)
    o_samp = pl.pallas_call(
        _ouv_body, grid=(MLA_HEADS // 2,),
        in_specs=[pl.BlockSpec((n_rows, 2 * KV_RANK), lambda h: (0, h)),
                  pl.BlockSpec((1, 2 * KV_RANK, 2 * V_DIM), lambda h: (h, 0, 0))],
        out_specs=pl.BlockSpec((n_rows, 2 * V_DIM), lambda h: (0, h)),
        out_shape=jax.ShapeDtypeStruct((n_rows, MLA_HEADS * V_DIM), BF16),
        compiler_params=_params("parallel"), name="mla_ouv")(o_lat, w["wuv_pair"])

    o_all = _assemble(lay, o_real, o_samp, o_meta)
    x_new = _rows_call(_mix_out_body, "mla_out", [x, o_all], [w["wo"], g, b], [(D_MODEL, F32)])[0]
    return x_new, c, kpe


def _split_dot(a, sel):
    hi = a.astype(BF16)
    lo = (a - hi.astype(F32)).astype(BF16)
    return _dot(hi, sel) + _dot(lo, sel)


def _head_sum(x, gs, gb):
    return _split_dot(_split_dot(x, gs), gb)


def _softplus(z):
    return jnp.maximum(z, 0.0) + jnp.log(1.0 + jnp.exp(-jnp.abs(z)))


def _rw_proj_body(x_ref, xp_ref, mu_ref, wr_ref, wk_ref, wv_ref, w0_ref, w1_ref, w2_ref, a0_ref, a1_ref, a2_ref,
                  g1_ref, g2_ref, kk_ref, ka_ref, rk_ref, gs_ref, gb_ref,
                  r_o, d_o, k_o, v_o, kk_o, a_o, g_o, bonus_o):
    x = x_ref[...]
    xx = xp_ref[...] - x
    mix = lambda j: (x + xx * mu_ref[j:j + 1, :]).astype(BF16)
    w_lo = _dot(mix(1), w1_ref[...])
    a_lo = _dot(mix(4), a1_ref[...])
    g_lo = _dot(mix(5), g1_ref[...])
    r = _dot(mix(0), wr_ref[...])
    k = _dot(mix(2), wk_ref[...])
    v = _dot(mix(3), wv_ref[...])
    wl = w0_ref[...] + _dot(jnp.tanh(w_lo).astype(BF16), w2_ref[...])
    w = -_softplus(-wl) - 0.5
    a = _sigmoid(a0_ref[...] + _dot(a_lo.astype(BF16), a2_ref[...]))
    g = _dot(_sigmoid(g_lo).astype(BF16), g2_ref[...])
    gs, gb = gs_ref[...], gb_ref[...]
    kk = k * kk_ref[...]
    kk = kk * lax.rsqrt(jnp.maximum(_head_sum(kk * kk, gs, gb), 1e-24))
    k = k * (1.0 + (a - 1.0) * ka_ref[...])
    r_o[...] = r
    d_o[...] = -jnp.exp(w)
    k_o[...] = k
    v_o[...] = v
    kk_o[...] = kk
    a_o[...] = a
    g_o[...] = g
    bonus_o[...] = _head_sum(r * k * rk_ref[...], gs, gb) * v


def _rw_chunk_body(r_ref, d_ref, k_ref, v_ref, kk_ref, a_ref, s0_ref, y_ref, st_ref, s_sc, *, chunk, n_fac, group):
    ci = pl.program_id(1)

    @pl.when(ci == 0)
    def _init():
        s_sc[...] = s0_ref[...]

    row = lax.broadcasted_iota(jnp.int32, (chunk, chunk), 0)
    col = lax.broadcasted_iota(jnp.int32, (chunk, chunk), 1)
    strict, incl = col < row, col <= row
    rows = group * chunk
    if group == 1:
        same, below = None, incl
    else:
        rr = lax.broadcasted_iota(jnp.int32, (rows, rows), 0)
        cc = lax.broadcasted_iota(jnp.int32, (rows, rows), 1)
        same = (rr // chunk) == (cc // chunk)
        below = jnp.logical_and(same, cc <= rr)
    d_all, k_all, kk_all = d_ref[...], k_ref[...], kk_ref[...]
    cum = _dot(below.astype(F32), d_all, HI)
    if group == 1:
        cum_end = cum[chunk - 1:chunk, :]
    else:
        cum_end = _dot(same.astype(F32), d_all, HI)
    be_all = kk_all * a_ref[...]
    w_inv = jnp.exp(-cum)
    w_rem = jnp.exp(cum_end - cum)
    al_t_all = (-kk_all * jnp.exp(cum - d_all)).astype(BF16)
    r_t_all = r_ref[...] * jnp.exp(cum)
    be_h_all = (be_all * w_inv).astype(BF16)
    k_h_all = (k_all * w_inv).astype(BF16)
    be_e_all = (be_all * w_rem).astype(BF16)
    k_e_all = (k_all * w_rem).astype(BF16)
    w_end = jnp.exp(cum_end)
    v_all = v_ref[...].astype(BF16)
    zero = jnp.zeros((chunk, RW_HEAD), BF16)
    chains = [(g, h) for g in range(group) for h in range(RW_HEADS)]
    ids = range(len(chains))
    cut = lambda a, g, h: a[g * chunk:(g + 1) * chunk, h * RW_HEAD:(h + 1) * RW_HEAD]
    gram = [_dot_nt(jnp.concatenate([cut(al_t_all, *c), cut(r_t_all, *c).astype(BF16)], 0),
                    jnp.concatenate([cut(be_h_all, *c), cut(k_h_all, *c)], 0)) for c in chains]
    l_ab = [jnp.where(strict, g[:chunk, :chunk], 0.0).astype(BF16) for g in gram]
    l_ak = [jnp.where(strict, g[:chunk, chunk:], 0.0).astype(BF16) for g in gram]
    m_r = [jnp.concatenate([jnp.where(incl, g[chunk:, :chunk], 0.0),
                            jnp.where(incl, g[chunk:, chunk:], 0.0)], 1).astype(BF16) for g in gram]
    u = [jnp.concatenate([cut(al_t_all, *c).astype(F32), _dot(l_ak[i], cut(v_all, *c))], 1) for i, c in zip(ids, chains)]
    lp = l_ab
    for f in range(n_fac):
        u = [u[i] + _dot(lp[i], u[i].astype(BF16)) for i in ids]
        if f + 1 < n_fac:
            lp = [_dot(lp[i], lp[i]).astype(BF16) for i in ids]
    z = [jnp.concatenate([u[i].astype(BF16), jnp.concatenate([zero, cut(v_all, *c)], 1)], 0) for i, c in zip(ids, chains)]
    g = [_dot(m_r[i], z[i]) for i in ids]
    ad = [_dot_tn(z[i], jnp.concatenate([cut(be_e_all, *c), cut(k_e_all, *c)], 0)) for i, c in zip(ids, chains)]
    s_old = [s_sc[c[0], c[1]] for c in chains]
    s_bf = [s.astype(BF16) for s in s_old]
    y = [_dot_nt((cut(r_t_all, *c) + g[i][:, :RW_HEAD]).astype(BF16), s_bf[i]) + g[i][:, RW_HEAD:]
         for i, c in zip(ids, chains)]
    y_ref[...] = jnp.concatenate(
        [jnp.concatenate(y[gi * RW_HEADS:(gi + 1) * RW_HEADS], 1) for gi in range(group)], 0)
    for i, (gi, h) in zip(ids, chains):
        w_h = w_end[gi * chunk:gi * chunk + 1, h * RW_HEAD:(h + 1) * RW_HEAD]
        s_sc[gi, h] = s_old[i] * w_h + _dot(s_bf[i], ad[i][:RW_HEAD].astype(BF16)) + ad[i][RW_HEAD:]

    @pl.when(ci == pl.num_programs(1) - 1)
    def _fin():
        st_ref[...] = s_sc[...]


def _rw_scan(seqs, s0, n_seq, seq_len, row_off, chunk, group=1):
    nc = seq_len // chunk
    assert group == 1 or nc == 1
    assert n_seq % group == 0
    rows = group * chunk
    off = row_off // rows
    n_fac = max(1, math.ceil(math.log2(chunk)))
    row_spec = pl.BlockSpec((rows, D_MODEL), lambda b, c: (off + b * nc + c, 0))
    st_spec = pl.BlockSpec((group, RW_HEADS, RW_HEAD, RW_HEAD), lambda b, c: (b, 0, 0, 0))
    y, st = pl.pallas_call(
        functools.partial(_rw_chunk_body, chunk=chunk, n_fac=n_fac, group=group),
        grid=(n_seq // group, nc), in_specs=[row_spec] * 6 + [st_spec],
        out_specs=[pl.BlockSpec((rows, D_MODEL), lambda b, c: (b * nc + c, 0)), st_spec],
        out_shape=[jax.ShapeDtypeStruct((n_seq * seq_len, D_MODEL), F32),
                   jax.ShapeDtypeStruct((n_seq, RW_HEADS, RW_HEAD, RW_HEAD), F32)],
        scratch_shapes=[pltpu.VMEM((group, RW_HEADS, RW_HEAD, RW_HEAD), F32)],
        compiler_params=_params("parallel", "arbitrary"), name=f"rwkv_chunk{chunk}")(*seqs, s0)
    return y, st


def _rw_post_body(x_ref, y_ref, g_ref, bonus_ref, lg_ref, lb_ref, gs_ref, gb_ref, wo_ref, ng_ref, nb_ref, o_ref):
    y = y_ref[...]
    gs, gb = gs_ref[...], gb_ref[...]
    mean = _head_sum(y, gs, gb) * (1.0 / RW_HEAD)
    yc = y - mean
    var = _head_sum(yc * yc, gs, gb) * (1.0 / RW_HEAD)
    yn = yc * lax.rsqrt(var + GN_EPS) * lg_ref[...] + lb_ref[...]
    z = ((yn + bonus_ref[...]) * g_ref[...]).astype(BF16)
    o_ref[...] = _layer_norm(ALPHA * x_ref[...] + _dot(z, wo_ref[...]), ng_ref[...], nb_ref[...])


def _rwkv_layer(x, lay, m, p, shift_s, wkv_s, g, b):
    n_seq, seq, n_dec, t_dec = lay["B"], lay["S"], lay["Bd"], lay["Td"]
    soff, moff = lay["soff"], lay["moff"]
    n_rows = n_dec * t_dec
    xr = x[:n_seq * seq].reshape(n_seq, seq, D_MODEL)
    xm = x[moff:moff + n_seq * N_META].reshape(n_seq, N_META, D_MODEL)
    xs = x[soff:soff + n_rows].reshape(n_dec, t_dec, D_MODEL)
    prev_r = jnp.concatenate([xm[:, -1:], xr[:, :-1]], 1).reshape(n_seq * seq, D_MODEL)
    prev_m = jnp.concatenate([jnp.zeros((n_seq, 1, D_MODEL), F32), xm[:, :-1]], 1).reshape(n_seq * N_META, D_MODEL)
    prev_s = jnp.concatenate([shift_s[m][:, None], xs[:, :-1]], 1).reshape(n_rows, D_MODEL)
    xprev = _assemble(lay, prev_r, prev_s, prev_m)

    row = lambda a: a[m][None]
    gs = jnp.pad(jnp.repeat(jnp.eye(RW_HEADS, dtype=BF16), RW_HEAD, 0), ((0, 0), (0, LANES - RW_HEADS)))
    gb = gs.T
    bf = lambda a: a[m].astype(BF16)
    r, d, k, v, kk, a, gate, bonus = _rows_call(
        _rw_proj_body, "rwkv_proj", [x, xprev],
        [p['rw_mu'][m], bf(p['rw_wr']), bf(p['rw_wk']), bf(p['rw_wv']), row(p['rw_w0']), bf(p['rw_w1']),
         bf(p['rw_w2']), row(p['rw_a0']), bf(p['rw_a1']), bf(p['rw_a2']), bf(p['rw_g1']), bf(p['rw_g2']),
         row(p['rw_k_k']), row(p['rw_k_a']), p['rw_r_k'][m].reshape(1, D_MODEL), gs, gb],
        [(D_MODEL, F32)] * 8)
    seqs = (r, d, k, v, kk, a)
    zero_state = jnp.zeros((n_seq, RW_HEADS, RW_HEAD, RW_HEAD), F32)
    y_meta, st = _rw_scan(seqs, zero_state, n_seq, N_META, moff, N_META)
    y_real, st_p = _rw_scan(seqs, st, n_seq, seq, 0, min(RW_CHUNK, seq))
    t_pad = 8
    pad_rows = lambda z: jnp.pad(z[soff:soff + n_rows].reshape(n_dec, t_dec, D_MODEL),
                                 ((0, 0), (0, t_pad - t_dec), (0, 0))).reshape(n_dec * t_pad, D_MODEL)
    y_s, st_s = _rw_scan(tuple(pad_rows(z) for z in seqs), wkv_s[m], n_dec, t_pad, 0, t_pad,
                         group=math.gcd(RW_GROUP, n_dec))
    y_s = y_s.reshape(n_dec, t_pad, D_MODEL)[:, :t_dec].reshape(n_rows, D_MODEL)
    y_all = _assemble(lay, y_real, y_s, y_meta)
    x_new = _rows_call(
        _rw_post_body, "rwkv_post", [x, y_all, gate, bonus],
        [row(p['rw_lnx_g']), row(p['rw_lnx_b']), gs, gb, bf(p['rw_wo']), g, b], [(D_MODEL, F32)])[0]
    return x_new, st_p, xr[:, -1], st_s, xs[:, -1]


def _s5_disc_body(lre_ref, lim_ref, dt_ref, bre_ref, bim_ref, lre_s, lim_s, dt_s, bbr_o, bbi_o, ar_o, ai_o):
    def disc(lre, lim, dt):
        mag = jnp.exp(lre * dt)
        ab_re, ab_im = mag * jnp.cos(lim * dt), mag * jnp.sin(lim * dt)
        den = lre * lre + lim * lim
        nr = ab_re - 1.0
        return ab_re, ab_im, (nr * lre + ab_im * lim) / den, (ab_im * lre - nr * lim) / den

    _, _, co_re, co_im = disc(lre_ref[...], lim_ref[...], dt_ref[...])
    b_re, b_im = bre_ref[...], bim_ref[...]
    bbr_o[...] = co_re * b_re - co_im * b_im
    bbi_o[...] = co_re * b_im + co_im * b_re
    ab_re, ab_im, _, _ = disc(lre_s[...], lim_s[...], dt_s[...])
    ar_o[...] = ab_re
    ai_o[...] = ab_im


def _s5_input(u, bre_ref, bim_ref):
    ub = u.astype(BF16)
    w = S5_WIDTH // S5_BLOCKS
    cw = D_MODEL // S5_BLOCKS
    re = [_dot(ub[:, j * cw:(j + 1) * cw], bre_ref[j]) for j in range(S5_BLOCKS)]
    im = [_dot(ub[:, j * cw:(j + 1) * cw], bim_ref[j]) for j in range(S5_BLOCKS)]
    del w
    return re, im


def _s5_output(u, hre, him, cre_ref, cim_ref, dv_ref, wv_ref, wg_ref, g_ref, b_ref):
    w = S5_WIDTH // S5_BLOCKS
    hb_re, hb_im = hre.astype(BF16), him.astype(BF16)
    ys = [_dot(hb_re[:, j * w:(j + 1) * w], cre_ref[j]) - _dot(hb_im[:, j * w:(j + 1) * w], cim_ref[j])
          for j in range(S5_BLOCKS)]
    y = jnp.concatenate(ys, 1) + dv_ref[...] * u
    z = jax.nn.gelu(y).astype(BF16)
    out = _dot(z, wv_ref[...]) * _sigmoid(_dot(z, wg_ref[...]))
    return _layer_norm(ALPHA * u + out, g_ref[...], b_ref[...])


def _s5_seq_body(x_ref, h0r_ref, h0i_ref, ar_ref, ai_ref, bre_ref, bim_ref, cre_ref, cim_ref, dv_ref, wv_ref, wg_ref,
                 g_ref, b_ref, o_ref, htr_ref, hti_ref, hre_sc, him_sc, sr_sc, si_sc, *, tt):
    ti = pl.program_id(1)

    @pl.when(ti == 0)
    def _init():
        sr_sc[...] = h0r_ref[0]
        si_sc[...] = h0i_ref[0]

    u = x_ref[...]
    re, im = _s5_input(u, bre_ref, bim_ref)
    w = S5_WIDTH // S5_BLOCKS
    for j in range(S5_BLOCKS):
        hre_sc[:, j * w:(j + 1) * w] = re[j]
        him_sc[:, j * w:(j + 1) * w] = im[j]
    ar, ai = ar_ref[...], ai_ref[...]

    def step(t, carry):
        hr, hi = carry
        nr = ar * hr - ai * hi + hre_sc[pl.ds(t, 1), :]
        ni = ar * hi + ai * hr + him_sc[pl.ds(t, 1), :]
        hre_sc[pl.ds(t, 1), :] = nr
        him_sc[pl.ds(t, 1), :] = ni
        return nr, ni

    hr, hi = lax.fori_loop(0, tt, step, (sr_sc[...], si_sc[...]))
    sr_sc[...] = hr
    si_sc[...] = hi
    o_ref[...] = _s5_output(u, hre_sc[...], him_sc[...], cre_ref, cim_ref, dv_ref, wv_ref, wg_ref, g_ref, b_ref)

    @pl.when(ti == pl.num_programs(1) - 1)
    def _fin():
        htr_ref[0] = hr
        hti_ref[0] = hi


def _s5_step_body(x_ref, h0r_ref, h0i_ref, ar_ref, ai_ref, bre_ref, bim_ref, cre_ref, cim_ref, dv_ref, wv_ref, wg_ref,
                  g_ref, b_ref, o_ref, htr_ref, hti_ref, sr_sc, si_sc):
    t = pl.program_id(0)

    @pl.when(t == 0)
    def _init():
        sr_sc[...] = h0r_ref[...]
        si_sc[...] = h0i_ref[...]

    u = x_ref[0]
    re, im = _s5_input(u, bre_ref, bim_ref)
    bur, bui = jnp.concatenate(re, 1), jnp.concatenate(im, 1)
    ar, ai = ar_ref[...], ai_ref[...]
    hr, hi = sr_sc[...], si_sc[...]
    nr = ar * hr - ai * hi + bur
    ni = ar * hi + ai * hr + bui
    sr_sc[...] = nr
    si_sc[...] = ni
    o_ref[0] = _s5_output(u, nr, ni, cre_ref, cim_ref, dv_ref, wv_ref, wg_ref, g_ref, b_ref)

    @pl.when(t == pl.num_programs(0) - 1)
    def _fin():
        htr_ref[...] = nr
        hti_ref[...] = ni


def _s5_weights(m, p):
    rep = lambda a: jnp.repeat(a, S5_GROUP, axis=-1)
    lre, lim = p['s5_lam_re'][m], p['s5_lam_im'][m]
    dt = jnp.broadcast_to(jnp.exp(p['s5_log_dt'][m])[:, None], (S5_GROUPS, S5_STATE))
    flat = lambda a: a[m].reshape(S5_GROUPS, S5_STATE * S5_GROUP)
    n = S5_STATE * S5_GROUP
    bb_re, bb_im, ab_re, ab_im = pl.pallas_call(
        _s5_disc_body,
        out_shape=[jax.ShapeDtypeStruct((S5_GROUPS, n), F32)] * 2 + [jax.ShapeDtypeStruct((S5_GROUPS, S5_STATE), F32)] * 2,
        name="s5_disc")(rep(lre), rep(lim), rep(dt), flat(p['s5_b_re']), flat(p['s5_b_im']), lre, lim, dt)
    gpb = S5_GROUPS // S5_BLOCKS
    eye = jnp.eye(gpb, dtype=F32)

    def in_blocks(bb):
        t = bb.reshape(S5_BLOCKS, gpb, S5_STATE, S5_GROUP).transpose(0, 1, 3, 2)
        return jnp.einsum('jgsp,gh->jgshp', t, eye).reshape(S5_BLOCKS, gpb * S5_GROUP, gpb * S5_STATE).astype(BF16)

    def out_blocks(c):
        t = c.reshape(S5_BLOCKS, gpb, S5_GROUP, S5_STATE).transpose(0, 1, 3, 2)
        return jnp.einsum('jgps,gh->jgphs', t, eye).reshape(S5_BLOCKS, gpb * S5_STATE, gpb * S5_GROUP).astype(BF16)

    return dict(ar=ab_re.reshape(1, S5_WIDTH), ai=ab_im.reshape(1, S5_WIDTH),
                bre=in_blocks(bb_re), bim=in_blocks(bb_im),
                cre=out_blocks(p['s5_c_re'][m]), cim=out_blocks(p['s5_c_im'][m]),
                dv=p['s5_d'][m][None], wv=p['s5_wv'][m].astype(BF16), wg=p['s5_wg'][m].astype(BF16))


def _s5_seq(x, h0r, h0i, w, g, b, n_seq, seq_len, row_off, tt):
    nt = seq_len // tt
    off = row_off // tt
    consts = [w["ar"], w["ai"], w["bre"], w["bim"], w["cre"], w["cim"], w["dv"], w["wv"], w["wg"], g, b]
    st_spec = pl.BlockSpec((1, 1, S5_WIDTH), lambda bb, t: (bb, 0, 0))
    return pl.pallas_call(
        functools.partial(_s5_seq_body, tt=tt), grid=(n_seq, nt),
        in_specs=[pl.BlockSpec((tt, D_MODEL), lambda bb, t: (off + bb * nt + t, 0)), st_spec, st_spec]
        + [_const_spec(a, 2) for a in consts],
        out_specs=[pl.BlockSpec((tt, D_MODEL), lambda bb, t: (bb * nt + t, 0)), st_spec, st_spec],
        out_shape=[jax.ShapeDtypeStruct((n_seq * seq_len, D_MODEL), F32)]
        + [jax.ShapeDtypeStruct((n_seq, 1, S5_WIDTH), F32)] * 2,
        scratch_shapes=[pltpu.VMEM((tt, S5_WIDTH), F32), pltpu.VMEM((tt, S5_WIDTH), F32),
                        pltpu.VMEM((1, S5_WIDTH), F32), pltpu.VMEM((1, S5_WIDTH), F32)],
        compiler_params=_params("parallel", "arbitrary"), name=f"s5_seq{tt}")(x, h0r, h0i, *consts)


def _s5_layer(x, lay, m, p, re_s, im_s, g, b):
    n_seq, seq, n_dec, t_dec = lay["B"], lay["S"], lay["Bd"], lay["Td"]
    soff, moff = lay["soff"], lay["moff"]
    n_rows = n_dec * t_dec
    w = _s5_weights(m, p)
    zero = jnp.zeros((n_seq, 1, S5_WIDTH), F32)
    o_meta, hr, hi = _s5_seq(x, zero, zero, w, g, b, n_seq, N_META, moff, N_META)
    o_real, hr, hi = _s5_seq(x, hr, hi, w, g, b, n_seq, seq, 0, min(S5_TILE, seq))
    xs = x[soff:soff + n_rows].reshape(n_dec, t_dec, D_MODEL).transpose(1, 0, 2)
    consts = [w["ar"], w["ai"], w["bre"], w["bim"], w["cre"], w["cim"], w["dv"], w["wv"], w["wg"], g, b]
    st_spec = pl.BlockSpec((n_dec, S5_WIDTH), lambda t: (0, 0))
    o_s, sr, si = pl.pallas_call(
        _s5_step_body, grid=(t_dec,),
        in_specs=[pl.BlockSpec((1, n_dec, D_MODEL), lambda t: (t, 0, 0)), st_spec, st_spec]
        + [_const_spec(a, 1) for a in consts],
        out_specs=[pl.BlockSpec((1, n_dec, D_MODEL), lambda t: (t, 0, 0)), st_spec, st_spec],
        out_shape=[jax.ShapeDtypeStruct((t_dec, n_dec, D_MODEL), F32)]
        + [jax.ShapeDtypeStruct((n_dec, S5_WIDTH), F32)] * 2,
        scratch_shapes=[pltpu.VMEM((n_dec, S5_WIDTH), F32), pltpu.VMEM((n_dec, S5_WIDTH), F32)],
        compiler_params=_params("arbitrary"), name="s5_step")(
            xs, re_s[m].reshape(n_dec, S5_WIDTH), im_s[m].reshape(n_dec, S5_WIDTH), *consts)
    o_s = o_s.transpose(1, 0, 2).reshape(n_rows, D_MODEL)
    x_new = _assemble(lay, o_real, o_s, o_meta)
    shp = (S5_GROUPS, S5_STATE)
    return x_new, hr.reshape(n_seq, *shp), hi.reshape(n_seq, *shp), sr.reshape(n_dec, *shp), si.reshape(n_dec, *shp)


def _round_up(n, m):
    return (n + m - 1) // m * m


def kernel(x_prompt, x_sample, cache_mla_latent, cache_mla_krope, state_rwkv_wkv, state_rwkv_shift, state_s5_re, state_s5_im, page_table, meta_tokens, ln_g, ln_b, ffn_w1, ffn_w3, ffn_w2, mla_w_dq, mla_q_norm, mla_w_uq, mla_w_dkv, mla_kv_norm, mla_w_uk, mla_w_uv, mla_w_o, rw_mu, rw_wr, rw_wk, rw_wv, rw_w0, rw_w1, rw_w2, rw_a0, rw_a1, rw_a2, rw_g1, rw_g2, rw_k_k, rw_k_a, rw_r_k, rw_lnx_g, rw_lnx_b, rw_wo, s5_lam_re, s5_lam_im, s5_log_dt, s5_b_re, s5_b_im, s5_c_re, s5_c_im, s5_d, s5_wv, s5_wg):
    p = dict(rw_mu=rw_mu, rw_wr=rw_wr, rw_wk=rw_wk, rw_wv=rw_wv, rw_w0=rw_w0, rw_w1=rw_w1, rw_w2=rw_w2,
             rw_a0=rw_a0, rw_a1=rw_a1, rw_a2=rw_a2, rw_g1=rw_g1, rw_g2=rw_g2, rw_k_k=rw_k_k, rw_k_a=rw_k_a,
             rw_r_k=rw_r_k, rw_lnx_g=rw_lnx_g, rw_lnx_b=rw_lnx_b, rw_wo=rw_wo,
             s5_lam_re=s5_lam_re, s5_lam_im=s5_lam_im, s5_log_dt=s5_log_dt, s5_b_re=s5_b_re, s5_b_im=s5_b_im,
             s5_c_re=s5_c_re, s5_c_im=s5_c_im, s5_d=s5_d, s5_wv=s5_wv, s5_wg=s5_wg)
    n_seq, seq, _ = x_prompt.shape
    n_dec, t_dec, _ = x_sample.shape
    past = page_table.shape[1] * cache_mla_latent.shape[2]
    n_rows = n_dec * t_dec
    soff = _round_up(n_seq * seq, ROW_ALIGN)
    moff = _round_up(soff + n_rows, ROW_ALIGN)
    n_tot = _round_up(moff + n_seq * N_META, ROW_ALIGN)
    lay = dict(B=n_seq, S=seq, Bd=n_dec, Td=t_dec, soff=soff, moff=moff, n_tot=n_tot)
    assemble = functools.partial(_assemble, lay)

    meta = jnp.broadcast_to(meta_tokens[None], (n_seq, N_META, D_MODEL)).reshape(n_seq * N_META, D_MODEL)
    x = assemble(x_prompt.reshape(n_seq * seq, D_MODEL), x_sample.reshape(n_rows, D_MODEL), meta)
    pos = assemble(jnp.tile(N_META + jnp.arange(seq, dtype=jnp.int32), n_seq)[:, None],
                   jnp.tile(past + jnp.arange(t_dec, dtype=jnp.int32), n_dec)[:, None],
                   jnp.tile(jnp.arange(N_META, dtype=jnp.int32), n_seq)[:, None])[:, 0]
    tabs = _rope_tables(pos)

    lat_p, kr_p, lat_s, kr_s = [], [], [], []
    wkv_p = sh_p = wkv_s = sh_s = re_p = im_p = re_s = im_s = None
    for i in range(DEPTH):
        kind, m = i % 3, i // 3
        norm = lambda j: (ln_g[i, j][None], ln_b[i, j][None])
        ffn = lambda xx, j, nj: _ffn(xx, ffn_w1[i, j].astype(BF16), ffn_w3[i, j].astype(BF16),
                                     ffn_w2[i, j].astype(BF16), *norm(nj))
        x = ffn(x, 0, 0)
        if kind == 0:
            w = _mla_weights(m, mla_w_dq, mla_q_norm, mla_w_uq, mla_w_dkv, mla_kv_norm, mla_w_uk, mla_w_uv, mla_w_o)
            x, c, kpe = _mla_layer(x, lay, tabs, w, m, cache_mla_latent, cache_mla_krope, page_table, *norm(1))
            split = lambda a: (jnp.concatenate(
                [a[moff:moff + n_seq * N_META].reshape(n_seq, N_META, -1), a[:n_seq * seq].reshape(n_seq, seq, -1)], 1),
                a[soff:soff + n_rows].reshape(n_dec, t_dec, -1))
            cp, cs = split(c)
            kp, ks = split(kpe)
            lat_p.append(cp), lat_s.append(cs), kr_p.append(kp), kr_s.append(ks)
        elif kind == 1:
            x, wkv_p, sh_p, wkv_s, sh_s = _rwkv_layer(x, lay, m, p, state_rwkv_shift, state_rwkv_wkv, *norm(1))
        else:
            x, re_p, im_p, re_s, im_s = _s5_layer(x, lay, m, p, state_s5_re, state_s5_im, *norm(1))
        x = ffn(x, 1, 2)

    y_prompt = x[:n_seq * seq].reshape(n_seq, seq, D_MODEL)
    y_sample = x[soff:soff + n_rows].reshape(n_dec, t_dec, D_MODEL)
    return (y_prompt, y_sample, jnp.stack(lat_p), jnp.stack(kr_p), jnp.stack(lat_s), jnp.stack(kr_s),
            wkv_p[None], sh_p[None], wkv_s[None], sh_s[None], re_p[None], im_p[None], re_s[None], im_s[None])
```

```python
import functools
import math

import jax
import jax.numpy as jnp
from jax import lax
from jax.experimental import pallas as pl
from jax.experimental.pallas import tpu as pltpu

F32 = jnp.float32
BF16 = jnp.bfloat16
HI = lax.Precision.HIGHEST

D_MODEL = 1024
DEPTH = 4
N_META = 16
D_FF = 2816
ALPHA = (2 * DEPTH) ** 0.25
LN_EPS = 1e-5
RMS_EPS = 1e-6
MLA_HEADS = 16
Q_RANK = 768
KV_RANK = 256
NOPE_DIM = 64
ROPE_DIM = 32
QK_DIM = NOPE_DIM + ROPE_DIM
V_DIM = 64
ROPE_THETA = 10000.0
RW_HEAD = 64
RW_HEADS = D_MODEL // RW_HEAD
GN_EPS = 64e-5
S5_GROUP = 16
S5_GROUPS = D_MODEL // S5_GROUP
S5_STATE = 64
S5_WIDTH = S5_GROUPS * S5_STATE
S5_BLOCKS = 8

LANES = 128
HEAD_PAD = 128
ROW_TILE = 256
ROW_ALIGN = 256
ATTN_TILE = 1024
ATTN_SUB = 128
ATTN_AHEAD = 2
DECODE_PAGES = 64
RW_CHUNK = 64
RW_GROUP = 4
S5_TILE = 256
VMEM_LIMIT = 56 * 1024 * 1024
NEG = -1e30


def _dot(a, b, prec=None):
    return jnp.dot(a, b, preferred_element_type=F32, precision=prec)


def _dot_nt(a, b, prec=None):
    return lax.dot_general(a, b, (((1,), (1,)), ((), ())), preferred_element_type=F32, precision=prec)


def _dot_tn(a, b, prec=None):
    return lax.dot_general(a, b, (((0,), (0,)), ((), ())), preferred_element_type=F32, precision=prec)


def _layer_norm(y, g, b):
    mu = jnp.mean(y, -1, keepdims=True)
    yc = y - mu
    var = jnp.mean(yc * yc, -1, keepdims=True)
    return yc * lax.rsqrt(var + LN_EPS) * g + b


def _rms_norm(y, g):
    return y * lax.rsqrt(jnp.mean(y * y, -1, keepdims=True) + RMS_EPS) * g


def _sigmoid(x):
    return 1.0 / (1.0 + jnp.exp(-x))


def _params(*sem):
    return pltpu.CompilerParams(dimension_semantics=sem, vmem_limit_bytes=VMEM_LIMIT)


def _const_spec(a, n_grid):
    zeros = (0,) * a.ndim
    if n_grid == 1:
        imap = lambda i: zeros
    elif n_grid == 2:
        imap = lambda i, j: zeros
    else:
        imap = lambda i, j, k, l: zeros
    return pl.BlockSpec(a.shape, imap, pipeline_mode=pl.Buffered(1))


def _pick_spec(a, idx):
    tail = (0,) * (a.ndim - 1)
    return pl.BlockSpec((None,) + a.shape[1:], lambda i: (idx,) + tail, pipeline_mode=pl.Buffered(1))


def _rows_call(body, name, row_ins, const_ins, outs, tm=ROW_TILE):
    n = row_ins[0].shape[0]
    assert n % tm == 0
    in_specs = [pl.BlockSpec((tm, a.shape[1]), lambda i: (i, 0)) for a in row_ins]
    in_specs += [_pick_spec(*a) if isinstance(a, tuple) else _const_spec(a, 1) for a in const_ins]
    consts = [a[0] if isinstance(a, tuple) else a for a in const_ins]
    out_specs = [pl.BlockSpec((tm, w), lambda i: (i, 0)) for w, _ in outs]
    out_shape = [jax.ShapeDtypeStruct((n, w), dt) for w, dt in outs]
    res = pl.pallas_call(body, grid=(n // tm,), in_specs=in_specs, out_specs=out_specs, out_shape=out_shape,
                         compiler_params=_params("parallel"), name=name)(*row_ins, *consts)
    return res


def _assemble(lay, real, samp, meta):
    w = real.shape[1]
    gap = lambda n: jnp.zeros((n, w), real.dtype)
    return jnp.concatenate([real, gap(lay["soff"] - real.shape[0]), samp,
                            gap(lay["moff"] - lay["soff"] - samp.shape[0]), meta,
                            gap(lay["n_tot"] - lay["moff"] - meta.shape[0])], 0)


FF_CHUNK = 1408


def _ffn_stage(x, w1_ref, w3_ref, w2_ref, g_ref, b_ref):
    xb = x.astype(BF16)
    acc = None
    for c in range(D_FF // FF_CHUNK):
        sl = slice(c * FF_CHUNK, (c + 1) * FF_CHUNK)
        h1 = _dot(xb, w1_ref[:, sl])
        h3 = _dot(xb, w3_ref[:, sl])
        part = _dot((h1 * _sigmoid(h1) * h3).astype(BF16), w2_ref[sl, :])
        acc = part if acc is None else acc + part
    return _layer_norm(ALPHA * x + 0.5 * acc, g_ref[...], b_ref[...])


def _chain_body(*refs, n_ffn, has_mix):
    x = refs[0][...]
    pos = 1
    if has_mix:
        h_ref, w_ref, g_ref, b_ref = refs[1:5]
        x = _layer_norm(ALPHA * x + _dot(h_ref[...], w_ref[...]), g_ref[...], b_ref[...])
        pos = 5
    for _ in range(n_ffn):
        x = _ffn_stage(x, *refs[pos:pos + 5])
        pos += 5
    refs[pos][...] = x


def _chain(x, stacks, ffn_ids, mix=None):
    w1_all, w3_all, w2_all, g_all, b_all = stacks
    rows, consts = [x], []
    if mix is not None:
        h, wo, ni = mix
        rows.append(h)
        consts += [wo, (g_all, ni), (b_all, ni)]
    for wi, ni in ffn_ids:
        consts += [(w1_all, wi), (w3_all, wi), (w2_all, wi), (g_all, ni), (b_all, ni)]
    body = functools.partial(_chain_body, n_ffn=len(ffn_ids), has_mix=mix is not None)
    name = ("mix_" if mix is not None else "") + "ffn" * len(ffn_ids)
    return _rows_call(body, name, rows, consts, [(D_MODEL, F32)])[0]


def _mla_proj_body(x_ref, ct_ref, s1_ref, s2_ref, wdq_ref, qn_ref, wq_ref, wdkv_ref, kvn_ref, wk_ref, wv_ref,
                   q_ref, k_ref, v_ref, c_ref, kpe_ref):
    xb = x_ref[...].astype(BF16)
    ct, s1, s2 = ct_ref[...], s1_ref[...], s2_ref[...]
    dq = _dot(xb, wdq_ref[...])
    ckv = _dot(xb, wdkv_ref[...])
    cq = _rms_norm(dq, qn_ref[...])
    c = _rms_norm(ckv[:, :KV_RANK], kvn_ref[...])
    q = _dot(cq.astype(BF16), wq_ref[...])
    half = ROPE_DIM // 2
    k1, k2 = ckv[:, KV_RANK:KV_RANK + half], ckv[:, KV_RANK + half:]
    cos, sin = ct[:, NOPE_DIM:NOPE_DIM + half], s1[:, NOPE_DIM + half:QK_DIM]
    kpe = jnp.concatenate([k1 * cos - k2 * sin, k1 * sin + k2 * cos], -1)
    c_ref[...] = c
    kpe_ref[...] = kpe
    cb = c.astype(BF16)
    knope = _dot(cb, wk_ref[...])
    v_ref[...] = _dot(cb, wv_ref[...]).astype(BF16)
    tm = kpe.shape[0]
    kpe_slot = jnp.concatenate([jnp.zeros((tm, NOPE_DIM), F32), kpe, jnp.zeros((tm, HEAD_PAD - QK_DIM), F32)], -1)
    scale = QK_DIM ** -0.5 * math.log2(math.e)
    for h in range(MLA_HEADS):
        sl = slice(h * HEAD_PAD, (h + 1) * HEAD_PAD)
        blk = q[:, sl]
        rot = blk * ct + pltpu.roll(blk, half, 1) * s1 + pltpu.roll(blk, HEAD_PAD - half, 1) * s2
        q_ref[:, sl] = (rot * scale).astype(BF16)
        k_ref[:, sl] = (knope[:, sl] + kpe_slot).astype(BF16)


def _flash_body(qt_ref, kt_ref, *refs, tq, tk, sub, ahead, has_lead):
    if has_lead:
        q_ref, k_ref, v_ref, mk_ref, mv_ref, o_ref, m_sc, l_sc, acc_sc = refs
    else:
        q_ref, k_ref, v_ref, o_ref, m_sc, l_sc, acc_sc = refs
    t = pl.program_id(2)
    qi, ki = qt_ref[t], kt_ref[t]
    blocks = [(hh, r0) for r0 in range(0, tq, sub) for hh in range(2)]

    def q_blk(hh, r0):
        return q_ref[r0:r0 + sub, hh * HEAD_PAD:(hh + 1) * HEAD_PAD]

    def weights(hh, r0, s, first):
        rows = slice(r0, r0 + sub)
        width = s.shape[1]
        m_cur = jnp.max(s, -1, keepdims=True)
        if first:
            m_new = jnp.broadcast_to(m_cur, (sub, LANES))
        else:
            m_old = m_sc[hh, rows, :]
            m_new = jnp.maximum(m_old, m_cur)
        m_wide = jnp.concatenate([m_new] * (width // LANES), 1) if width >= LANES else m_new[:, :width]
        p = jnp.exp2(s - m_wide)
        p_sum = jnp.sum(p, -1, keepdims=True)
        if first:
            corr = None
            l_sc[hh, rows, :] = jnp.broadcast_to(p_sum, (sub, LANES))
        else:
            corr = jnp.exp2(m_old - m_new)
            l_sc[hh, rows, :] = corr * l_sc[hh, rows, :] + p_sum
        m_sc[hh, rows, :] = m_new
        return p.astype(BF16), corr

    def attend(score_fn, vals, first=False):
        scores = {}
        for i in range(len(blocks) + ahead):
            if i < len(blocks):
                scores[i] = score_fn(*blocks[i])
            if i >= ahead:
                hh, r0 = blocks[i - ahead]
                p, corr = weights(hh, r0, scores.pop(i - ahead), first)
                rows = slice(r0, r0 + sub)
                pv = _dot(p, vals[:p.shape[1]])
                acc_sc[hh, rows, :] = pv if first else corr * acc_sc[hh, rows, :] + pv

    @pl.when(ki == 0)
    def _init():
        if has_lead:
            attend(lambda hh, r0: _dot_nt(q_blk(hh, r0), mk_ref[:, hh * HEAD_PAD:(hh + 1) * HEAD_PAD]),
                   mv_ref[...], first=True)
        else:
            m_sc[...] = jnp.full(m_sc.shape, NEG, F32)
            l_sc[...] = jnp.zeros(l_sc.shape, F32)
            acc_sc[...] = jnp.zeros(acc_sc.shape, F32)

    def step(masked):
        def score(hh, r0):
            kw = r0 + sub if masked else tk
            s = _dot_nt(q_blk(hh, r0), k_ref[:kw, hh * HEAD_PAD:(hh + 1) * HEAD_PAD])
            if masked:
                row = r0 + lax.broadcasted_iota(jnp.int32, (sub, kw), 0)
                col = lax.broadcasted_iota(jnp.int32, (sub, kw), 1)
                s = jnp.where(col <= row, s, NEG)
            return s
        attend(score, v_ref[...])

    @pl.when(ki < qi)
    def _below():
        step(False)

    @pl.when(ki == qi)
    def _diag():
        step(True)
        lane = lax.broadcasted_iota(jnp.int32, (tq, 2 * V_DIM), 1)
        o0 = acc_sc[0] / l_sc[0]
        o1 = acc_sc[1] / l_sc[1]
        o_ref[...] = jnp.where(lane < V_DIM, o0, o1).astype(BF16)


def _flash(q, k, v, n_seq, seq_len, row_off, tile, lead_off=None):
    tq = tk = tile
    nq = seq_len // tq
    qo = row_off // tq
    has_lead = lead_off is not None
    pairs = [(i, j) for i in range(nq) for j in range(i + 1)]
    qt = jnp.array([i for i, _ in pairs], jnp.int32)
    kt = jnp.array([j for _, j in pairs], jnp.int32)
    in_specs = [
        pl.BlockSpec((tq, 2 * HEAD_PAD), lambda b, h, t, qt, kt: (qo + b * nq + qt[t], h)),
        pl.BlockSpec((tk, 2 * HEAD_PAD), lambda b, h, t, qt, kt: (qo + b * nq + kt[t], h)),
        pl.BlockSpec((tk, 2 * V_DIM), lambda b, h, t, qt, kt: (qo + b * nq + kt[t], h)),
    ]
    args = [q, k, v]
    if has_lead:
        lo = lead_off // N_META
        in_specs += [pl.BlockSpec((N_META, 2 * HEAD_PAD), lambda b, h, t, qt, kt: (lo + b, h)),
                     pl.BlockSpec((N_META, 2 * V_DIM), lambda b, h, t, qt, kt: (lo + b, h))]
        args += [k, v]
    return pl.pallas_call(
        functools.partial(_flash_body, tq=tq, tk=tk, sub=min(ATTN_SUB, tq), ahead=ATTN_AHEAD, has_lead=has_lead),
        grid_spec=pltpu.PrefetchScalarGridSpec(
            num_scalar_prefetch=2, grid=(n_seq, MLA_HEADS // 2, len(pairs)),
            in_specs=in_specs,
            out_specs=pl.BlockSpec((tq, 2 * V_DIM), lambda b, h, t, qt, kt: (b * nq + qt[t], h)),
            scratch_shapes=[pltpu.VMEM((2, tq, LANES), F32), pltpu.VMEM((2, tq, LANES), F32),
                            pltpu.VMEM((2, tq, 2 * V_DIM), F32)]),
        out_shape=jax.ShapeDtypeStruct((n_seq * seq_len, MLA_HEADS * V_DIM), BF16),
        compiler_params=_params("parallel", "parallel", "arbitrary"),
        name="mla_flash_lead" if has_lead else "mla_flash")(qt, kt, *args)


def _qlat_body(q_ref, w_ref, o_ref):
    o_ref[0] = _dot(q_ref[...], w_ref[0]).astype(BF16)


def _decode_body(pt_ref, ql_ref, qp_ref, *refs, t_new, n_pg):
    lat_refs, kr_refs = refs[:n_pg], refs[n_pg:2 * n_pg]
    cn_ref, kn_ref, o_ref, m_sc, l_sc, acc_sc = refs[2 * n_pg:]
    j = pl.program_id(1)
    ql, qp = ql_ref[0], qp_ref[0]

    @pl.when(j == 0)
    def _init():
        m_sc[...] = jnp.full(m_sc.shape, NEG, F32)
        l_sc[...] = jnp.zeros(l_sc.shape, F32)
        acc_sc[...] = jnp.zeros(acc_sc.shape, F32)

    def update(s, vals):
        m_old = m_sc[...]
        m_new = jnp.maximum(m_old, jnp.max(s, -1, keepdims=True))
        p = jnp.exp2(s - m_new)
        corr = jnp.exp2(m_old - m_new)
        l_sc[...] = corr * l_sc[...] + jnp.sum(p, -1, keepdims=True)
        acc_sc[...] = corr * acc_sc[...] + _dot(p.astype(BF16), vals)
        m_sc[...] = m_new

    lat = jnp.concatenate([r[0, 0].astype(BF16) for r in lat_refs], 0)
    kr_t = jnp.concatenate([r[0, 0].astype(BF16) for r in kr_refs], 1)
    update(_dot_nt(ql, lat) + _dot(qp, kr_t), lat)

    @pl.when(j == pl.num_programs(1) - 1)
    def _fin():
        cn = cn_ref[0].astype(BF16)
        kn = kn_ref[0].astype(BF16)
        s = _dot_nt(ql, cn) + _dot_nt(qp, kn)
        rows = s.shape[0]
        t_q = lax.rem(lax.broadcasted_iota(jnp.int32, (rows, t_new), 0), t_new)
        t_k = lax.broadcasted_iota(jnp.int32, (rows, t_new), 1)
        update(jnp.where(t_k <= t_q, s, NEG), cn)
        o_ref[0] = (acc_sc[...] / l_sc[...]).astype(BF16)


def _ouv_body(o_ref, w_ref, out_ref):
    out_ref[...] = _dot(o_ref[...], w_ref[0]).astype(BF16)


def _rope_tables(pos):
    half = ROPE_DIM // 2
    inv = 1.0 / (ROPE_THETA ** (jnp.arange(half, dtype=F32) * (2.0 / ROPE_DIM)))
    ang = pos.astype(F32)[:, None] * inv
    c, s = jnp.cos(ang), jnp.sin(ang)
    n = pos.shape[0]
    one, zero = jnp.ones((n, NOPE_DIM), F32), jnp.zeros((n, NOPE_DIM), F32)
    pad = jnp.zeros((n, HEAD_PAD - QK_DIM), F32)
    z16 = jnp.zeros((n, half), F32)
    ct = jnp.concatenate([one, c, c, pad], 1)
    s1 = jnp.concatenate([zero, z16, s, pad], 1)
    s2 = jnp.concatenate([zero, -s, z16, pad], 1)
    return ct, s1, s2


def _mla_weights(m, w_dq, q_norm, w_uq, w_dkv, kv_norm, w_uk, w_uv, w_o):
    half = ROPE_DIM // 2
    wq = jnp.pad(w_uq[m], ((0, 0), (0, 0), (0, HEAD_PAD - QK_DIM))).reshape(Q_RANK, MLA_HEADS * HEAD_PAD)
    wk = jnp.pad(w_uk[m], ((0, 0), (0, 0), (0, HEAD_PAD - NOPE_DIM))).reshape(KV_RANK, MLA_HEADS * HEAD_PAD)
    wv = w_uv[m].reshape(KV_RANK, MLA_HEADS * V_DIM)
    wukt = jnp.pad(jnp.transpose(w_uk[m], (1, 2, 0)), ((0, 0), (0, HEAD_PAD - NOPE_DIM), (0, 0)))
    wuv_h = jnp.transpose(w_uv[m], (1, 0, 2)).reshape(MLA_HEADS // 2, 2, KV_RANK, V_DIM)
    eye2 = jnp.eye(2, dtype=F32)
    wuv_pair = jnp.einsum('pirv,ij->pirjv', wuv_h, eye2).reshape(MLA_HEADS // 2, 2 * KV_RANK, 2 * V_DIM)
    del half
    return dict(wdq=w_dq[m].astype(BF16), qn=q_norm[m][None], wq=wq.astype(BF16), wdkv=w_dkv[m].astype(BF16),
                kvn=kv_norm[m][None], wk=wk.astype(BF16), wv=wv.astype(BF16), wukt=wukt.astype(BF16),
                wuv_pair=wuv_pair.astype(BF16), wo=w_o[m].reshape(MLA_HEADS * V_DIM, D_MODEL).astype(BF16))


def _mla_layer(x, lay, tabs, w, m, lat_pool, kr_pool, page_table):
    n_seq, seq, n_dec, t_dec = lay["B"], lay["S"], lay["Bd"], lay["Td"]
    q, k, v, c, kpe = _rows_call(
        _mla_proj_body, "mla_proj", [x, *tabs],
        [w["wdq"], w["qn"], w["wq"], w["wdkv"], w["kvn"], w["wk"], w["wv"]],
        [(MLA_HEADS * HEAD_PAD, BF16), (MLA_HEADS * HEAD_PAD, BF16), (MLA_HEADS * V_DIM, BF16),
         (KV_RANK, F32), (ROPE_DIM, F32)])
    soff, moff = lay["soff"], lay["moff"]
    o_real = _flash(q, k, v, n_seq, seq, 0, min(ATTN_TILE, seq), lead_off=moff)
    o_meta = _flash(q, k, v, n_seq, N_META, moff, N_META)

    n_rows = n_dec * t_dec
    hq = MLA_HEADS * t_dec
    q_lat = pl.pallas_call(
        _qlat_body, grid=(MLA_HEADS,),
        in_specs=[pl.BlockSpec((n_rows, HEAD_PAD), lambda h: (soff // n_rows, h)),
                  pl.BlockSpec((1, HEAD_PAD, KV_RANK), lambda h: (h, 0, 0))],
        out_specs=pl.BlockSpec((1, n_rows, KV_RANK), lambda h: (h, 0, 0)),
        out_shape=jax.ShapeDtypeStruct((MLA_HEADS, n_rows, KV_RANK), BF16),
        compiler_params=_params("parallel"), name="mla_qlat")(q, w["wukt"])
    q_lat = q_lat.reshape(MLA_HEADS, n_dec, t_dec, KV_RANK).transpose(1, 0, 2, 3).reshape(n_dec, hq, KV_RANK)
    q_s = q[soff:soff + n_rows].reshape(n_dec, t_dec, MLA_HEADS, HEAD_PAD)[..., NOPE_DIM:QK_DIM]
    q_pe = q_s.transpose(0, 2, 1, 3).reshape(n_dec, hq, ROPE_DIM)
    c_new = c[soff:soff + n_rows].reshape(n_dec, t_dec, KV_RANK)
    k_new = kpe[soff:soff + n_rows].reshape(n_dec, t_dec, ROPE_DIM)
    n_pages, page = page_table.shape[1], lat_pool.shape[2]
    n_pg = math.gcd(DECODE_PAGES, n_pages)
    kr_pool_t = jnp.swapaxes(kr_pool, 2, 3)
    lat_spec = lambda i: pl.BlockSpec((1, 1, page, KV_RANK), lambda bb, j, pt: (m, pt[bb, j * n_pg + i], 0, 0))
    kr_spec = lambda i: pl.BlockSpec((1, 1, ROPE_DIM, page), lambda bb, j, pt: (m, pt[bb, j * n_pg + i], 0, 0))
    o_lat = pl.pallas_call(
        functools.partial(_decode_body, t_new=t_dec, n_pg=n_pg),
        grid_spec=pltpu.PrefetchScalarGridSpec(
            num_scalar_prefetch=1, grid=(n_dec, n_pages // n_pg),
            in_specs=[pl.BlockSpec((1, hq, KV_RANK), lambda bb, j, pt: (bb, 0, 0)),
                      pl.BlockSpec((1, hq, ROPE_DIM), lambda bb, j, pt: (bb, 0, 0))]
            + [lat_spec(i) for i in range(n_pg)] + [kr_spec(i) for i in range(n_pg)]
            + [pl.BlockSpec((1, t_dec, KV_RANK), lambda bb, j, pt: (bb, 0, 0)),
               pl.BlockSpec((1, t_dec, ROPE_DIM), lambda bb, j, pt: (bb, 0, 0))],
            out_specs=pl.BlockSpec((1, hq, KV_RANK), lambda bb, j, pt: (bb, 0, 0)),
            scratch_shapes=[pltpu.VMEM((hq, 1), F32), pltpu.VMEM((hq, 1), F32), pltpu.VMEM((hq, KV_RANK), F32)]),
        out_shape=jax.ShapeDtypeStruct((n_dec, hq, KV_RANK), BF16),
        compiler_params=_params("parallel", "arbitrary"), name="mla_decode")(
            page_table, q_lat, q_pe, *([lat_pool] * n_pg), *([kr_pool_t] * n_pg), c_new, k_new)
    o_lat = o_lat.reshape(n_dec, MLA_HEADS, t_dec, KV_RANK).transpose(0, 2, 1, 3).reshape(n_rows, MLA_HEADS * KV_RANK)
    o_samp = pl.pallas_call(
        _ouv_body, grid=(MLA_HEADS // 2,),
        in_specs=[pl.BlockSpec((n_rows, 2 * KV_RANK), lambda h: (0, h)),
                  pl.BlockSpec((1, 2 * KV_RANK, 2 * V_DIM), lambda h: (h, 0, 0))],
        out_specs=pl.BlockSpec((n_rows, 2 * V_DIM), lambda h: (0, h)),
        out_shape=jax.ShapeDtypeStruct((n_rows, MLA_HEADS * V_DIM), BF16),
        compiler_params=_params("parallel"), name="mla_ouv")(o_lat, w["wuv_pair"])

    return _assemble(lay, o_real, o_samp, o_meta), c, kpe


def _split_dot(a, sel):
    hi = a.astype(BF16)
    lo = (a - hi.astype(F32)).astype(BF16)
    return _dot(hi, sel) + _dot(lo, sel)


def _head_sum(x, gs, gb):
    return _split_dot(_split_dot(x, gs), gb)


def _softplus(z):
    return jnp.maximum(z, 0.0) + jnp.log(1.0 + jnp.exp(-jnp.abs(z)))


def _rw_proj_body(x_ref, xp_ref, mu_ref, wr_ref, wk_ref, wv_ref, w0_ref, w1_ref, w2_ref, a0_ref, a1_ref, a2_ref,
                  g1_ref, g2_ref, kk_ref, ka_ref, rk_ref, gs_ref, gb_ref,
                  r_o, d_o, k_o, v_o, kk_o, a_o, g_o, bonus_o):
    x = x_ref[...]
    xx = xp_ref[...] - x
    mix = lambda j: (x + xx * mu_ref[j:j + 1, :]).astype(BF16)
    w_lo = _dot(mix(1), w1_ref[...])
    a_lo = _dot(mix(4), a1_ref[...])
    g_lo = _dot(mix(5), g1_ref[...])
    r = _dot(mix(0), wr_ref[...])
    k = _dot(mix(2), wk_ref[...])
    v = _dot(mix(3), wv_ref[...])
    wl = w0_ref[...] + _dot(jnp.tanh(w_lo).astype(BF16), w2_ref[...])
    w = -_softplus(-wl) - 0.5
    a = _sigmoid(a0_ref[...] + _dot(a_lo.astype(BF16), a2_ref[...]))
    g = _dot(_sigmoid(g_lo).astype(BF16), g2_ref[...])
    gs, gb = gs_ref[...], gb_ref[...]
    kk = k * kk_ref[...]
    kk = kk * lax.rsqrt(jnp.maximum(_head_sum(kk * kk, gs, gb), 1e-24))
    k = k * (1.0 + (a - 1.0) * ka_ref[...])
    r_o[...] = r
    d_o[...] = -jnp.exp(w)
    k_o[...] = k
    v_o[...] = v
    kk_o[...] = kk
    a_o[...] = a
    g_o[...] = g
    bonus_o[...] = _head_sum(r * k * rk_ref[...], gs, gb) * v


def _rw_chunk_body(r_ref, d_ref, k_ref, v_ref, kk_ref, a_ref, s0_ref, y_ref, st_ref, s_sc, *, chunk, n_fac, group):
    ci = pl.program_id(1)

    @pl.when(ci == 0)
    def _init():
        s_sc[...] = s0_ref[...]

    row = lax.broadcasted_iota(jnp.int32, (chunk, chunk), 0)
    col = lax.broadcasted_iota(jnp.int32, (chunk, chunk), 1)
    strict, incl = col < row, col <= row
    rows = group * chunk
    if group == 1:
        same, below = None, incl
    else:
        rr = lax.broadcasted_iota(jnp.int32, (rows, rows), 0)
        cc = lax.broadcasted_iota(jnp.int32, (rows, rows), 1)
        same = (rr // chunk) == (cc // chunk)
        below = jnp.logical_and(same, cc <= rr)
    d_all, k_all, kk_all = d_ref[...], k_ref[...], kk_ref[...]
    cum = _dot(below.astype(F32), d_all, HI)
    if group == 1:
        cum_end = cum[chunk - 1:chunk, :]
    else:
        cum_end = _dot(same.astype(F32), d_all, HI)
    be_all = kk_all * a_ref[...]
    w_inv = jnp.exp(-cum)
    w_rem = jnp.exp(cum_end - cum)
    al_t_all = (-kk_all * jnp.exp(cum - d_all)).astype(BF16)
    r_t_all = r_ref[...] * jnp.exp(cum)
    be_h_all = (be_all * w_inv).astype(BF16)
    k_h_all = (k_all * w_inv).astype(BF16)
    be_e_all = (be_all * w_rem).astype(BF16)
    k_e_all = (k_all * w_rem).astype(BF16)
    w_end = jnp.exp(cum_end)
    v_all = v_ref[...].astype(BF16)
    zero = jnp.zeros((chunk, RW_HEAD), BF16)
    chains = [(g, h) for g in range(group) for h in range(RW_HEADS)]
    ids = range(len(chains))
    cut = lambda a, g, h: a[g * chunk:(g + 1) * chunk, h * RW_HEAD:(h + 1) * RW_HEAD]
    gram = [_dot_nt(jnp.concatenate([cut(al_t_all, *c), cut(r_t_all, *c).astype(BF16)], 0),
                    jnp.concatenate([cut(be_h_all, *c), cut(k_h_all, *c)], 0)) for c in chains]
    l_ab = [jnp.where(strict, g[:chunk, :chunk], 0.0).astype(BF16) for g in gram]
    l_ak = [jnp.where(strict, g[:chunk, chunk:], 0.0).astype(BF16) for g in gram]
    m_r = [jnp.concatenate([jnp.where(incl, g[chunk:, :chunk], 0.0),
                            jnp.where(incl, g[chunk:, chunk:], 0.0)], 1).astype(BF16) for g in gram]
    u = [jnp.concatenate([cut(al_t_all, *c).astype(F32), _dot(l_ak[i], cut(v_all, *c))], 1) for i, c in zip(ids, chains)]
    lp = l_ab
    for f in range(n_fac):
        u = [u[i] + _dot(lp[i], u[i].astype(BF16)) for i in ids]
        if f + 1 < n_fac:
            lp = [_dot(lp[i], lp[i]).astype(BF16) for i in ids]
    z = [jnp.concatenate([u[i].astype(BF16), jnp.concatenate([zero, cut(v_all, *c)], 1)], 0) for i, c in zip(ids, chains)]
    g = [_dot(m_r[i], z[i]) for i in ids]
    ad = [_dot_tn(z[i], jnp.concatenate([cut(be_e_all, *c), cut(k_e_all, *c)], 0)) for i, c in zip(ids, chains)]
    s_old = [s_sc[c[0], c[1]] for c in chains]
    s_bf = [s.astype(BF16) for s in s_old]
    y = [_dot_nt((cut(r_t_all, *c) + g[i][:, :RW_HEAD]).astype(BF16), s_bf[i]) + g[i][:, RW_HEAD:]
         for i, c in zip(ids, chains)]
    y_ref[...] = jnp.concatenate(
        [jnp.concatenate(y[gi * RW_HEADS:(gi + 1) * RW_HEADS], 1) for gi in range(group)], 0)
    for i, (gi, h) in zip(ids, chains):
        w_h = w_end[gi * chunk:gi * chunk + 1, h * RW_HEAD:(h + 1) * RW_HEAD]
        s_sc[gi, h] = s_old[i] * w_h + _dot(s_bf[i], ad[i][:RW_HEAD].astype(BF16)) + ad[i][RW_HEAD:]

    @pl.when(ci == pl.num_programs(1) - 1)
    def _fin():
        st_ref[...] = s_sc[...]


def _rw_scan(seqs, s0, n_seq, seq_len, row_off, chunk, group=1):
    nc = seq_len // chunk
    assert group == 1 or nc == 1
    assert n_seq % group == 0
    rows = group * chunk
    off = row_off // rows
    n_fac = max(1, math.ceil(math.log2(chunk)))
    row_spec = pl.BlockSpec((rows, D_MODEL), lambda b, c: (off + b * nc + c, 0))
    st_spec = pl.BlockSpec((group, RW_HEADS, RW_HEAD, RW_HEAD), lambda b, c: (b, 0, 0, 0))
    y, st = pl.pallas_call(
        functools.partial(_rw_chunk_body, chunk=chunk, n_fac=n_fac, group=group),
        grid=(n_seq // group, nc), in_specs=[row_spec] * 6 + [st_spec],
        out_specs=[pl.BlockSpec((rows, D_MODEL), lambda b, c: (b * nc + c, 0)), st_spec],
        out_shape=[jax.ShapeDtypeStruct((n_seq * seq_len, D_MODEL), F32),
                   jax.ShapeDtypeStruct((n_seq, RW_HEADS, RW_HEAD, RW_HEAD), F32)],
        scratch_shapes=[pltpu.VMEM((group, RW_HEADS, RW_HEAD, RW_HEAD), F32)],
        compiler_params=_params("parallel", "arbitrary"), name=f"rwkv_chunk{chunk}")(*seqs, s0)
    return y, st


def _rw_post_body(x_ref, y_ref, g_ref, bonus_ref, lg_ref, lb_ref, gs_ref, gb_ref, wo_ref, ng_ref, nb_ref, o_ref):
    y = y_ref[...]
    gs, gb = gs_ref[...], gb_ref[...]
    mean = _head_sum(y, gs, gb) * (1.0 / RW_HEAD)
    yc = y - mean
    var = _head_sum(yc * yc, gs, gb) * (1.0 / RW_HEAD)
    yn = yc * lax.rsqrt(var + GN_EPS) * lg_ref[...] + lb_ref[...]
    z = ((yn + bonus_ref[...]) * g_ref[...]).astype(BF16)
    o_ref[...] = _layer_norm(ALPHA * x_ref[...] + _dot(z, wo_ref[...]), ng_ref[...], nb_ref[...])


def _rwkv_layer(x, lay, m, p, shift_s, wkv_s, g, b):
    n_seq, seq, n_dec, t_dec = lay["B"], lay["S"], lay["Bd"], lay["Td"]
    soff, moff = lay["soff"], lay["moff"]
    n_rows = n_dec * t_dec
    xm = x[moff:moff + n_seq * N_META].reshape(n_seq, N_META, D_MODEL)
    xs = x[soff:soff + n_rows].reshape(n_dec, t_dec, D_MODEL)
    prev_s = jnp.concatenate([shift_s[m][:, None], xs[:, :-1]], 1).reshape(n_rows, D_MODEL)
    xprev = jnp.concatenate([jnp.zeros((1, D_MODEL), F32), x[:-1]], 0)
    xprev = lax.dynamic_update_slice(xprev, prev_s, (soff, 0))
    for bi in range(n_seq):
        xprev = lax.dynamic_update_slice(xprev, xm[bi, -1:], (bi * seq, 0))
        xprev = lax.dynamic_update_slice(xprev, jnp.zeros((1, D_MODEL), F32), (moff + bi * N_META, 0))
    x_last = jnp.stack([x[(bi + 1) * seq - 1] for bi in range(n_seq)])

    row = lambda a: a[m][None]
    gs = jnp.pad(jnp.repeat(jnp.eye(RW_HEADS, dtype=BF16), RW_HEAD, 0), ((0, 0), (0, LANES - RW_HEADS)))
    gb = gs.T
    bf = lambda a: a[m].astype(BF16)
    r, d, k, v, kk, a, gate, bonus = _rows_call(
        _rw_proj_body, "rwkv_proj", [x, xprev],
        [p['rw_mu'][m], bf(p['rw_wr']), bf(p['rw_wk']), bf(p['rw_wv']), row(p['rw_w0']), bf(p['rw_w1']),
         bf(p['rw_w2']), row(p['rw_a0']), bf(p['rw_a1']), bf(p['rw_a2']), bf(p['rw_g1']), bf(p['rw_g2']),
         row(p['rw_k_k']), row(p['rw_k_a']), p['rw_r_k'][m].reshape(1, D_MODEL), gs, gb],
        [(D_MODEL, F32)] * 8)
    seqs = (r, d, k, v, kk, a)
    zero_state = jnp.zeros((n_seq, RW_HEADS, RW_HEAD, RW_HEAD), F32)
    y_meta, st = _rw_scan(seqs, zero_state, n_seq, N_META, moff, N_META)
    y_real, st_p = _rw_scan(seqs, st, n_seq, seq, 0, min(RW_CHUNK, seq))
    t_pad = 8
    pad_rows = lambda z: jnp.pad(z[soff:soff + n_rows].reshape(n_dec, t_dec, D_MODEL),
                                 ((0, 0), (0, t_pad - t_dec), (0, 0))).reshape(n_dec * t_pad, D_MODEL)
    y_s, st_s = _rw_scan(tuple(pad_rows(z) for z in seqs), wkv_s[m], n_dec, t_pad, 0, t_pad,
                         group=math.gcd(RW_GROUP, n_dec))
    y_s = y_s.reshape(n_dec, t_pad, D_MODEL)[:, :t_dec].reshape(n_rows, D_MODEL)
    y_all = _assemble(lay, y_real, y_s, y_meta)
    x_new = _rows_call(
        _rw_post_body, "rwkv_post", [x, y_all, gate, bonus],
        [row(p['rw_lnx_g']), row(p['rw_lnx_b']), gs, gb, bf(p['rw_wo']), g, b], [(D_MODEL, F32)])[0]
    return x_new, st_p, x_last, st_s, xs[:, -1]


def _s5_disc_body(lre_ref, lim_ref, dt_ref, bre_ref, bim_ref, lre_s, lim_s, dt_s, bbr_o, bbi_o, ar_o, ai_o):
    def disc(lre, lim, dt):
        mag = jnp.exp(lre * dt)
        ab_re, ab_im = mag * jnp.cos(lim * dt), mag * jnp.sin(lim * dt)
        den = lre * lre + lim * lim
        nr = ab_re - 1.0
        return ab_re, ab_im, (nr * lre + ab_im * lim) / den, (ab_im * lre - nr * lim) / den

    _, _, co_re, co_im = disc(lre_ref[...], lim_ref[...], dt_ref[...])
    b_re, b_im = bre_ref[...], bim_ref[...]
    bbr_o[...] = co_re * b_re - co_im * b_im
    bbi_o[...] = co_re * b_im + co_im * b_re
    ab_re, ab_im, _, _ = disc(lre_s[...], lim_s[...], dt_s[...])
    ar_o[...] = ab_re
    ai_o[...] = ab_im


def _s5_input(u, bre_ref, bim_ref):
    ub = u.astype(BF16)
    w = S5_WIDTH // S5_BLOCKS
    cw = D_MODEL // S5_BLOCKS
    re = [_dot(ub[:, j * cw:(j + 1) * cw], bre_ref[j]) for j in range(S5_BLOCKS)]
    im = [_dot(ub[:, j * cw:(j + 1) * cw], bim_ref[j]) for j in range(S5_BLOCKS)]
    del w
    return re, im


def _s5_output(u, hre, him, cre_ref, cim_ref, dv_ref, wv_ref, wg_ref, g_ref, b_ref):
    w = S5_WIDTH // S5_BLOCKS
    hb_re, hb_im = hre.astype(BF16), him.astype(BF16)
    ys = [_dot(hb_re[:, j * w:(j + 1) * w], cre_ref[j]) - _dot(hb_im[:, j * w:(j + 1) * w], cim_ref[j])
          for j in range(S5_BLOCKS)]
    y = jnp.concatenate(ys, 1) + dv_ref[...] * u
    z = jax.nn.gelu(y).astype(BF16)
    out = _dot(z, wv_ref[...]) * _sigmoid(_dot(z, wg_ref[...]))
    return _layer_norm(ALPHA * u + out, g_ref[...], b_ref[...])


def _s5_seq_body(x_ref, h0r_ref, h0i_ref, ar_ref, ai_ref, bre_ref, bim_ref, cre_ref, cim_ref, dv_ref, wv_ref, wg_ref,
                 g_ref, b_ref, o_ref, htr_ref, hti_ref, hre_sc, him_sc, sr_sc, si_sc, *, tt):
    ti = pl.program_id(1)

    @pl.when(ti == 0)
    def _init():
        sr_sc[...] = h0r_ref[0]
        si_sc[...] = h0i_ref[0]

    u = x_ref[...]
    re, im = _s5_input(u, bre_ref, bim_ref)
    w = S5_WIDTH // S5_BLOCKS
    for j in range(S5_BLOCKS):
        hre_sc[:, j * w:(j + 1) * w] = re[j]
        him_sc[:, j * w:(j + 1) * w] = im[j]
    ar, ai = ar_ref[...], ai_ref[...]

    def step(t, carry):
        hr, hi = carry
        nr = ar * hr - ai * hi + hre_sc[pl.ds(t, 1), :]
        ni = ar * hi + ai * hr + him_sc[pl.ds(t, 1), :]
        hre_sc[pl.ds(t, 1), :] = nr
        him_sc[pl.ds(t, 1), :] = ni
        return nr, ni

    hr, hi = lax.fori_loop(0, tt, step, (sr_sc[...], si_sc[...]))
    sr_sc[...] = hr
    si_sc[...] = hi
    o_ref[...] = _s5_output(u, hre_sc[...], him_sc[...], cre_ref, cim_ref, dv_ref, wv_ref, wg_ref, g_ref, b_ref)

    @pl.when(ti == pl.num_programs(1) - 1)
    def _fin():
        htr_ref[0] = hr
        hti_ref[0] = hi


def _s5_step_body(x_ref, h0r_ref, h0i_ref, ar_ref, ai_ref, bre_ref, bim_ref, cre_ref, cim_ref, dv_ref, wv_ref, wg_ref,
                  g_ref, b_ref, o_ref, htr_ref, hti_ref, sr_sc, si_sc):
    t = pl.program_id(0)

    @pl.when(t == 0)
    def _init():
        sr_sc[...] = h0r_ref[...]
        si_sc[...] = h0i_ref[...]

    u = x_ref[0]
    re, im = _s5_input(u, bre_ref, bim_ref)
    bur, bui = jnp.concatenate(re, 1), jnp.concatenate(im, 1)
    ar, ai = ar_ref[...], ai_ref[...]
    hr, hi = sr_sc[...], si_sc[...]
    nr = ar * hr - ai * hi + bur
    ni = ar * hi + ai * hr + bui
    sr_sc[...] = nr
    si_sc[...] = ni
    o_ref[0] = _s5_output(u, nr, ni, cre_ref, cim_ref, dv_ref, wv_ref, wg_ref, g_ref, b_ref)

    @pl.when(t == pl.num_programs(0) - 1)
    def _fin():
        htr_ref[...] = nr
        hti_ref[...] = ni


def _s5_weights(m, p):
    rep = lambda a: jnp.repeat(a, S5_GROUP, axis=-1)
    lre, lim = p['s5_lam_re'][m], p['s5_lam_im'][m]
    dt = jnp.broadcast_to(jnp.exp(p['s5_log_dt'][m])[:, None], (S5_GROUPS, S5_STATE))
    flat = lambda a: a[m].reshape(S5_GROUPS, S5_STATE * S5_GROUP)
    n = S5_STATE * S5_GROUP
    bb_re, bb_im, ab_re, ab_im = pl.pallas_call(
        _s5_disc_body,
        out_shape=[jax.ShapeDtypeStruct((S5_GROUPS, n), F32)] * 2 + [jax.ShapeDtypeStruct((S5_GROUPS, S5_STATE), F32)] * 2,
        name="s5_disc")(rep(lre), rep(lim), rep(dt), flat(p['s5_b_re']), flat(p['s5_b_im']), lre, lim, dt)
    gpb = S5_GROUPS // S5_BLOCKS
    eye = jnp.eye(gpb, dtype=F32)

    def in_blocks(bb):
        t = bb.reshape(S5_BLOCKS, gpb, S5_STATE, S5_GROUP).transpose(0, 1, 3, 2)
        return jnp.einsum('jgsp,gh->jgshp', t, eye).reshape(S5_BLOCKS, gpb * S5_GROUP, gpb * S5_STATE).astype(BF16)

    def out_blocks(c):
        t = c.reshape(S5_BLOCKS, gpb, S5_GROUP, S5_STATE).transpose(0, 1, 3, 2)
        return jnp.einsum('jgps,gh->jgphs', t, eye).reshape(S5_BLOCKS, gpb * S5_STATE, gpb * S5_GROUP).astype(BF16)

    return dict(ar=ab_re.reshape(1, S5_WIDTH), ai=ab_im.reshape(1, S5_WIDTH),
                bre=in_blocks(bb_re), bim=in_blocks(bb_im),
                cre=out_blocks(p['s5_c_re'][m]), cim=out_blocks(p['s5_c_im'][m]),
                dv=p['s5_d'][m][None], wv=p['s5_wv'][m].astype(BF16), wg=p['s5_wg'][m].astype(BF16))


def _s5_seq(x, h0r, h0i, w, g, b, n_seq, seq_len, row_off, tt):
    nt = seq_len // tt
    off = row_off // tt
    consts = [w["ar"], w["ai"], w["bre"], w["bim"], w["cre"], w["cim"], w["dv"], w["wv"], w["wg"], g, b]
    st_spec = pl.BlockSpec((1, 1, S5_WIDTH), lambda bb, t: (bb, 0, 0))
    return pl.pallas_call(
        functools.partial(_s5_seq_body, tt=tt), grid=(n_seq, nt),
        in_specs=[pl.BlockSpec((tt, D_MODEL), lambda bb, t: (off + bb * nt + t, 0)), st_spec, st_spec]
        + [_const_spec(a, 2) for a in consts],
        out_specs=[pl.BlockSpec((tt, D_MODEL), lambda bb, t: (bb * nt + t, 0)), st_spec, st_spec],
        out_shape=[jax.ShapeDtypeStruct((n_seq * seq_len, D_MODEL), F32)]
        + [jax.ShapeDtypeStruct((n_seq, 1, S5_WIDTH), F32)] * 2,
        scratch_shapes=[pltpu.VMEM((tt, S5_WIDTH), F32), pltpu.VMEM((tt, S5_WIDTH), F32),
                        pltpu.VMEM((1, S5_WIDTH), F32), pltpu.VMEM((1, S5_WIDTH), F32)],
        compiler_params=_params("parallel", "arbitrary"), name=f"s5_seq{tt}")(x, h0r, h0i, *consts)


def _s5_layer(x, lay, m, p, re_s, im_s, g, b):
    n_seq, seq, n_dec, t_dec = lay["B"], lay["S"], lay["Bd"], lay["Td"]
    soff, moff = lay["soff"], lay["moff"]
    n_rows = n_dec * t_dec
    w = _s5_weights(m, p)
    zero = jnp.zeros((n_seq, 1, S5_WIDTH), F32)
    o_meta, hr, hi = _s5_seq(x, zero, zero, w, g, b, n_seq, N_META, moff, N_META)
    o_real, hr, hi = _s5_seq(x, hr, hi, w, g, b, n_seq, seq, 0, min(S5_TILE, seq))
    xs = x[soff:soff + n_rows].reshape(n_dec, t_dec, D_MODEL).transpose(1, 0, 2)
    consts = [w["ar"], w["ai"], w["bre"], w["bim"], w["cre"], w["cim"], w["dv"], w["wv"], w["wg"], g, b]
    st_spec = pl.BlockSpec((n_dec, S5_WIDTH), lambda t: (0, 0))
    o_s, sr, si = pl.pallas_call(
        _s5_step_body, grid=(t_dec,),
        in_specs=[pl.BlockSpec((1, n_dec, D_MODEL), lambda t: (t, 0, 0)), st_spec, st_spec]
        + [_const_spec(a, 1) for a in consts],
        out_specs=[pl.BlockSpec((1, n_dec, D_MODEL), lambda t: (t, 0, 0)), st_spec, st_spec],
        out_shape=[jax.ShapeDtypeStruct((t_dec, n_dec, D_MODEL), F32)]
        + [jax.ShapeDtypeStruct((n_dec, S5_WIDTH), F32)] * 2,
        scratch_shapes=[pltpu.VMEM((n_dec, S5_WIDTH), F32), pltpu.VMEM((n_dec, S5_WIDTH), F32)],
        compiler_params=_params("arbitrary"), name="s5_step")(
            xs, re_s[m].reshape(n_dec, S5_WIDTH), im_s[m].reshape(n_dec, S5_WIDTH), *consts)
    o_s = o_s.transpose(1, 0, 2).reshape(n_rows, D_MODEL)
    x_new = _assemble(lay, o_real, o_s, o_meta)
    shp = (S5_GROUPS, S5_STATE)
    return x_new, hr.reshape(n_seq, *shp), hi.reshape(n_seq, *shp), sr.reshape(n_dec, *shp), si.reshape(n_dec, *shp)


def _round_up(n, m):
    return (n + m - 1) // m * m


def kernel(x_prompt, x_sample, cache_mla_latent, cache_mla_krope, state_rwkv_wkv, state_rwkv_shift, state_s5_re, state_s5_im, page_table, meta_tokens, ln_g, ln_b, ffn_w1, ffn_w3, ffn_w2, mla_w_dq, mla_q_norm, mla_w_uq, mla_w_dkv, mla_kv_norm, mla_w_uk, mla_w_uv, mla_w_o, rw_mu, rw_wr, rw_wk, rw_wv, rw_w0, rw_w1, rw_w2, rw_a0, rw_a1, rw_a2, rw_g1, rw_g2, rw_k_k, rw_k_a, rw_r_k, rw_lnx_g, rw_lnx_b, rw_wo, s5_lam_re, s5_lam_im, s5_log_dt, s5_b_re, s5_b_im, s5_c_re, s5_c_im, s5_d, s5_wv, s5_wg):
    p = dict(rw_mu=rw_mu, rw_wr=rw_wr, rw_wk=rw_wk, rw_wv=rw_wv, rw_w0=rw_w0, rw_w1=rw_w1, rw_w2=rw_w2,
             rw_a0=rw_a0, rw_a1=rw_a1, rw_a2=rw_a2, rw_g1=rw_g1, rw_g2=rw_g2, rw_k_k=rw_k_k, rw_k_a=rw_k_a,
             rw_r_k=rw_r_k, rw_lnx_g=rw_lnx_g, rw_lnx_b=rw_lnx_b, rw_wo=rw_wo,
             s5_lam_re=s5_lam_re, s5_lam_im=s5_lam_im, s5_log_dt=s5_log_dt, s5_b_re=s5_b_re, s5_b_im=s5_b_im,
             s5_c_re=s5_c_re, s5_c_im=s5_c_im, s5_d=s5_d, s5_wv=s5_wv, s5_wg=s5_wg)
    n_seq, seq, _ = x_prompt.shape
    n_dec, t_dec, _ = x_sample.shape
    past = page_table.shape[1] * cache_mla_latent.shape[2]
    n_rows = n_dec * t_dec
    soff = _round_up(n_seq * seq, ROW_ALIGN)
    moff = _round_up(soff + n_rows, ROW_ALIGN)
    n_tot = _round_up(moff + n_seq * N_META, ROW_ALIGN)
    lay = dict(B=n_seq, S=seq, Bd=n_dec, Td=t_dec, soff=soff, moff=moff, n_tot=n_tot)
    assemble = functools.partial(_assemble, lay)

    meta = jnp.broadcast_to(meta_tokens[None], (n_seq, N_META, D_MODEL)).reshape(n_seq * N_META, D_MODEL)
    x = assemble(x_prompt.reshape(n_seq * seq, D_MODEL), x_sample.reshape(n_rows, D_MODEL), meta)
    pos = assemble(jnp.tile(N_META + jnp.arange(seq, dtype=jnp.int32), n_seq)[:, None],
                   jnp.tile(past + jnp.arange(t_dec, dtype=jnp.int32), n_dec)[:, None],
                   jnp.tile(jnp.arange(N_META, dtype=jnp.int32), n_seq)[:, None])[:, 0]
    tabs = _rope_tables(pos)

    stacked = lambda a: a.reshape((-1,) + a.shape[2:])
    w1_all, w3_all, w2_all = (stacked(a).astype(BF16) for a in (ffn_w1, ffn_w3, ffn_w2))
    g_all, b_all = (a.reshape(-1, 1, D_MODEL) for a in (ln_g, ln_b))

    lat_p, kr_p, lat_s, kr_s = [], [], [], []
    wkv_p = sh_p = wkv_s = sh_s = re_p = im_p = re_s = im_s = None
    stacks = (w1_all, w3_all, w2_all, g_all, b_all)
    ffn_a = lambda i: (2 * i, 3 * i)
    ffn_b = lambda i: (2 * i + 1, 3 * i + 2)
    x = _chain(x, stacks, [ffn_a(0)])
    for i in range(DEPTH):
        kind, m = i % 3, i // 3
        norm = lambda j: (ln_g[i, j][None], ln_b[i, j][None])
        tail = [ffn_b(i)] + ([ffn_a(i + 1)] if i + 1 < DEPTH else [])
        mix = None
        if kind == 0:
            w = _mla_weights(m, mla_w_dq, mla_q_norm, mla_w_uq, mla_w_dkv, mla_kv_norm, mla_w_uk, mla_w_uv, mla_w_o)
            o_all, c, kpe = _mla_layer(x, lay, tabs, w, m, cache_mla_latent, cache_mla_krope, page_table)
            mix = (o_all, w["wo"], 3 * i + 1)
            split = lambda a: (jnp.concatenate(
                [a[moff:moff + n_seq * N_META].reshape(n_seq, N_META, -1), a[:n_seq * seq].reshape(n_seq, seq, -1)], 1),
                a[soff:soff + n_rows].reshape(n_dec, t_dec, -1))
            cp, cs = split(c)
            kp, ks = split(kpe)
            lat_p.append(cp), lat_s.append(cs), kr_p.append(kp), kr_s.append(ks)
        elif kind == 1:
            x, wkv_p, sh_p, wkv_s, sh_s = _rwkv_layer(x, lay, m, p, state_rwkv_shift, state_rwkv_wkv, *norm(1))
        else:
            x, re_p, im_p, re_s, im_s = _s5_layer(x, lay, m, p, state_s5_re, state_s5_im, *norm(1))
        x = _chain(x, stacks, tail, mix)

    y_prompt = x[:n_seq * seq].reshape(n_seq, seq, D_MODEL)
    y_sample = x[soff:soff + n_rows].reshape(n_dec, t_dec, D_MODEL)
    return (y_prompt, y_sample, jnp.stack(lat_p), jnp.stack(kr_p), jnp.stack(lat_s), jnp.stack(kr_s),
            wkv_p[None], sh_p[None], wkv_s[None], sh_s[None], re_p[None], im_p[None], re_s[None], im_s[None])
```

```python
import functools
import math

import jax
import jax.numpy as jnp
from jax import lax
from jax.experimental import pallas as pl
from jax.experimental.pallas import tpu as pltpu

F32 = jnp.float32
BF16 = jnp.bfloat16
HI = lax.Precision.HIGHEST

D_MODEL = 1024
DEPTH = 4
N_META = 16
D_FF = 2816
ALPHA = (2 * DEPTH) ** 0.25
LN_EPS = 1e-5
RMS_EPS = 1e-6
MLA_HEADS = 16
Q_RANK = 768
KV_RANK = 256
NOPE_DIM = 64
ROPE_DIM = 32
QK_DIM = NOPE_DIM + ROPE_DIM
V_DIM = 64
ROPE_THETA = 10000.0
RW_HEAD = 64
RW_HEADS = D_MODEL // RW_HEAD
GN_EPS = 64e-5
S5_GROUP = 16
S5_GROUPS = D_MODEL // S5_GROUP
S5_STATE = 64
S5_WIDTH = S5_GROUPS * S5_STATE
S5_BLOCKS = 8

LANES = 128
HEAD_PAD = 128
ROW_TILE = 256
ROW_ALIGN = 256
ATTN_TILE = 1024
ATTN_SUB = 128
ATTN_AHEAD = 2
DECODE_PAGES = 64
RW_CHUNK = 64
RW_GROUP = 4
S5_TILE = 256
S5_UNROLL = 8
VMEM_LIMIT = 56 * 1024 * 1024
NEG = -1e30


def _dot(a, b, prec=None):
    return jnp.dot(a, b, preferred_element_type=F32, precision=prec)


def _dot_nt(a, b, prec=None):
    return lax.dot_general(a, b, (((1,), (1,)), ((), ())), preferred_element_type=F32, precision=prec)


def _dot_tn(a, b, prec=None):
    return lax.dot_general(a, b, (((0,), (0,)), ((), ())), preferred_element_type=F32, precision=prec)


def _layer_norm(y, g, b):
    mu = jnp.mean(y, -1, keepdims=True)
    yc = y - mu
    var = jnp.mean(yc * yc, -1, keepdims=True)
    return yc * lax.rsqrt(var + LN_EPS) * g + b


def _rms_norm(y, g):
    return y * lax.rsqrt(jnp.mean(y * y, -1, keepdims=True) + RMS_EPS) * g


def _sigmoid(x):
    return 1.0 / (1.0 + jnp.exp(-x))


def _params(*sem):
    return pltpu.CompilerParams(dimension_semantics=sem, vmem_limit_bytes=VMEM_LIMIT)


def _const_spec(a, n_grid):
    zeros = (0,) * a.ndim
    if n_grid == 1:
        imap = lambda i: zeros
    elif n_grid == 2:
        imap = lambda i, j: zeros
    else:
        imap = lambda i, j, k, l: zeros
    return pl.BlockSpec(a.shape, imap, pipeline_mode=pl.Buffered(1))


def _pick_spec(a, idx):
    tail = (0,) * (a.ndim - 1)
    return pl.BlockSpec((None,) + a.shape[1:], lambda i: (idx,) + tail, pipeline_mode=pl.Buffered(1))


def _rows_call(body, name, row_ins, const_ins, outs, tm=ROW_TILE):
    n = row_ins[0].shape[0]
    assert n % tm == 0
    in_specs = [pl.BlockSpec((tm, a.shape[1]), lambda i: (i, 0)) for a in row_ins]
    in_specs += [_pick_spec(*a) if isinstance(a, tuple) else _const_spec(a, 1) for a in const_ins]
    consts = [a[0] if isinstance(a, tuple) else a for a in const_ins]
    out_specs = [pl.BlockSpec((tm, w), lambda i: (i, 0)) for w, _ in outs]
    out_shape = [jax.ShapeDtypeStruct((n, w), dt) for w, dt in outs]
    res = pl.pallas_call(body, grid=(n // tm,), in_specs=in_specs, out_specs=out_specs, out_shape=out_shape,
                         compiler_params=_params("parallel"), name=name)(*row_ins, *consts)
    return res


def _assemble(lay, real, samp, meta):
    w = real.shape[1]
    gap = lambda n: jnp.zeros((n, w), real.dtype)
    return jnp.concatenate([real, gap(lay["soff"] - real.shape[0]), samp,
                            gap(lay["moff"] - lay["soff"] - samp.shape[0]), meta,
                            gap(lay["n_tot"] - lay["moff"] - meta.shape[0])], 0)


FF_CHUNK = 1408


def _ffn_stage(x, w1_ref, w3_ref, w2_ref, g_ref, b_ref):
    xb = x.astype(BF16)
    acc = None
    for c in range(D_FF // FF_CHUNK):
        sl = slice(c * FF_CHUNK, (c + 1) * FF_CHUNK)
        h1 = _dot(xb, w1_ref[:, sl])
        h3 = _dot(xb, w3_ref[:, sl])
        part = _dot((h1 * _sigmoid(h1) * h3).astype(BF16), w2_ref[sl, :])
        acc = part if acc is None else acc + part
    return _layer_norm(ALPHA * x + 0.5 * acc, g_ref[...], b_ref[...])


def _chain_body(*refs, n_ffn, has_mix):
    x = refs[0][...]
    pos = 1
    if has_mix:
        h_ref, w_ref, g_ref, b_ref = refs[1:5]
        x = _layer_norm(ALPHA * x + _dot(h_ref[...], w_ref[...]), g_ref[...], b_ref[...])
        pos = 5
    for _ in range(n_ffn):
        x = _ffn_stage(x, *refs[pos:pos + 5])
        pos += 5
    refs[pos][...] = x


def _chain(x, stacks, ffn_ids, mix=None):
    w1_all, w3_all, w2_all, g_all, b_all = stacks
    rows, consts = [x], []
    if mix is not None:
        h, wo, ni = mix
        rows.append(h)
        consts += [wo, (g_all, ni), (b_all, ni)]
    for wi, ni in ffn_ids:
        consts += [(w1_all, wi), (w3_all, wi), (w2_all, wi), (g_all, ni), (b_all, ni)]
    body = functools.partial(_chain_body, n_ffn=len(ffn_ids), has_mix=mix is not None)
    name = ("mix_" if mix is not None else "") + "ffn" * len(ffn_ids)
    return _rows_call(body, name, rows, consts, [(D_MODEL, F32)])[0]


def _mla_proj_body(x_ref, ct_ref, s1_ref, s2_ref, wdq_ref, qn_ref, wq_ref, wdkv_ref, kvn_ref, wk_ref, wv_ref,
                   q_ref, k_ref, v_ref, c_ref, kpe_ref):
    xb = x_ref[...].astype(BF16)
    ct, s1, s2 = ct_ref[...], s1_ref[...], s2_ref[...]
    dq = _dot(xb, wdq_ref[...])
    ckv = _dot(xb, wdkv_ref[...])
    cq = _rms_norm(dq, qn_ref[...])
    c = _rms_norm(ckv[:, :KV_RANK], kvn_ref[...])
    q = _dot(cq.astype(BF16), wq_ref[...])
    half = ROPE_DIM // 2
    k1, k2 = ckv[:, KV_RANK:KV_RANK + half], ckv[:, KV_RANK + half:]
    cos, sin = ct[:, NOPE_DIM:NOPE_DIM + half], s1[:, NOPE_DIM + half:QK_DIM]
    kpe = jnp.concatenate([k1 * cos - k2 * sin, k1 * sin + k2 * cos], -1)
    c_ref[...] = c
    kpe_ref[...] = kpe
    cb = c.astype(BF16)
    knope = _dot(cb, wk_ref[...])
    v_ref[...] = _dot(cb, wv_ref[...]).astype(BF16)
    tm = kpe.shape[0]
    kpe_slot = jnp.concatenate([jnp.zeros((tm, NOPE_DIM), F32), kpe, jnp.zeros((tm, HEAD_PAD - QK_DIM), F32)], -1)
    scale = QK_DIM ** -0.5 * math.log2(math.e)
    for h in range(MLA_HEADS):
        sl = slice(h * HEAD_PAD, (h + 1) * HEAD_PAD)
        blk = q[:, sl]
        rot = blk * ct + pltpu.roll(blk, half, 1) * s1 + pltpu.roll(blk, HEAD_PAD - half, 1) * s2
        q_ref[:, sl] = (rot * scale).astype(BF16)
        k_ref[:, sl] = (knope[:, sl] + kpe_slot).astype(BF16)


def _flash_body(qt_ref, kt_ref, *refs, tq, tk, sub, ahead, has_lead):
    if has_lead:
        q_ref, k_ref, v_ref, mk_ref, mv_ref, o_ref, m_sc, acc_sc = refs
    else:
        q_ref, k_ref, v_ref, o_ref, m_sc, acc_sc = refs
    t = pl.program_id(2)
    qi, ki = qt_ref[t], kt_ref[t]
    blocks = [(hh, r0) for r0 in range(0, tq, sub) for hh in range(2)]

    def q_blk(hh, r0):
        return q_ref[r0:r0 + sub, hh * HEAD_PAD:(hh + 1) * HEAD_PAD]

    def weights(hh, r0, s, first):
        rows = slice(r0, r0 + sub)
        width = s.shape[1]
        m_cur = jnp.max(s, -1, keepdims=True)
        if first:
            m_new = jnp.broadcast_to(m_cur, (sub, LANES))
        else:
            m_old = m_sc[hh, rows, :]
            m_new = jnp.maximum(m_old, m_cur)
        m_wide = jnp.concatenate([m_new] * (width // LANES), 1) if width >= LANES else m_new[:, :width]
        p = jnp.exp2(s - m_wide)
        corr = None if first else jnp.exp2(m_old - m_new)
        m_sc[hh, rows, :] = m_new
        return p.astype(BF16), corr

    def attend(score_fn, vals, first=False):
        lane = lax.broadcasted_iota(jnp.int32, vals.shape, 1)
        one = jnp.ones(vals.shape, BF16)
        vals_h = [jnp.where(lane < V_DIM, vals, one), jnp.where(lane < V_DIM, one, vals)]
        scores = {}
        for i in range(len(blocks) + ahead):
            if i < len(blocks):
                scores[i] = score_fn(*blocks[i])
            if i >= ahead:
                hh, r0 = blocks[i - ahead]
                p, corr = weights(hh, r0, scores.pop(i - ahead), first)
                rows = slice(r0, r0 + sub)
                pv = _dot(p, vals_h[hh][:p.shape[1]])
                acc_sc[hh, rows, :] = pv if first else corr * acc_sc[hh, rows, :] + pv

    @pl.when(ki == 0)
    def _init():
        if has_lead:
            attend(lambda hh, r0: _dot_nt(q_blk(hh, r0), mk_ref[:, hh * HEAD_PAD:(hh + 1) * HEAD_PAD]),
                   mv_ref[...], first=True)
        else:
            m_sc[...] = jnp.full(m_sc.shape, NEG, F32)
            acc_sc[...] = jnp.zeros(acc_sc.shape, F32)

    def step(masked):
        def score(hh, r0):
            kw = r0 + sub if masked else tk
            s = _dot_nt(q_blk(hh, r0), k_ref[:kw, hh * HEAD_PAD:(hh + 1) * HEAD_PAD])
            if masked:
                row = r0 + lax.broadcasted_iota(jnp.int32, (sub, kw), 0)
                col = lax.broadcasted_iota(jnp.int32, (sub, kw), 1)
                s = jnp.where(col <= row, s, NEG)
            return s
        attend(score, v_ref[...])

    @pl.when(ki < qi)
    def _below():
        step(False)

    @pl.when(ki == qi)
    def _diag():
        step(True)
        lane = lax.broadcasted_iota(jnp.int32, (tq, 2 * V_DIM), 1)
        a0, a1 = acc_sc[0], acc_sc[1]
        o0 = a0 / pltpu.roll(a0, V_DIM, 1)
        o1 = a1 / pltpu.roll(a1, V_DIM, 1)
        o_ref[...] = jnp.where(lane < V_DIM, o0, o1).astype(BF16)


def _flash(q, k, v, n_seq, seq_len, row_off, tile, lead_off=None):
    tq = tk = tile
    nq = seq_len // tq
    qo = row_off // tq
    has_lead = lead_off is not None
    pairs = [(i, j) for i in range(nq) for j in range(i + 1)]
    qt = jnp.array([i for i, _ in pairs], jnp.int32)
    kt = jnp.array([j for _, j in pairs], jnp.int32)
    in_specs = [
        pl.BlockSpec((tq, 2 * HEAD_PAD), lambda b, h, t, qt, kt: (qo + b * nq + qt[t], h)),
        pl.BlockSpec((tk, 2 * HEAD_PAD), lambda b, h, t, qt, kt: (qo + b * nq + kt[t], h)),
        pl.BlockSpec((tk, 2 * V_DIM), lambda b, h, t, qt, kt: (qo + b * nq + kt[t], h)),
    ]
    args = [q, k, v]
    if has_lead:
        lo = lead_off // N_META
        in_specs += [pl.BlockSpec((N_META, 2 * HEAD_PAD), lambda b, h, t, qt, kt: (lo + b, h)),
                     pl.BlockSpec((N_META, 2 * V_DIM), lambda b, h, t, qt, kt: (lo + b, h))]
        args += [k, v]
    return pl.pallas_call(
        functools.partial(_flash_body, tq=tq, tk=tk, sub=min(ATTN_SUB, tq), ahead=ATTN_AHEAD, has_lead=has_lead),
        grid_spec=pltpu.PrefetchScalarGridSpec(
            num_scalar_prefetch=2, grid=(n_seq, MLA_HEADS // 2, len(pairs)),
            in_specs=in_specs,
            out_specs=pl.BlockSpec((tq, 2 * V_DIM), lambda b, h, t, qt, kt: (b * nq + qt[t], h)),
            scratch_shapes=[pltpu.VMEM((2, tq, LANES), F32), pltpu.VMEM((2, tq, 2 * V_DIM), F32)]),
        out_shape=jax.ShapeDtypeStruct((n_seq * seq_len, MLA_HEADS * V_DIM), BF16),
        compiler_params=_params("parallel", "parallel", "arbitrary"),
        name="mla_flash_lead" if has_lead else "mla_flash")(qt, kt, *args)


def _qlat_body(q_ref, w_ref, o_ref):
    o_ref[0] = _dot(q_ref[...], w_ref[0]).astype(BF16)


def _decode_body(pt_ref, ql_ref, qp_ref, *refs, t_new, n_pg):
    lat_refs, kr_refs = refs[:n_pg], refs[n_pg:2 * n_pg]
    cn_ref, kn_ref, o_ref, m_sc, l_sc, acc_sc = refs[2 * n_pg:]
    j = pl.program_id(1)
    ql, qp = ql_ref[0], qp_ref[0]

    @pl.when(j == 0)
    def _init():
        m_sc[...] = jnp.full(m_sc.shape, NEG, F32)
        l_sc[...] = jnp.zeros(l_sc.shape, F32)
        acc_sc[...] = jnp.zeros(acc_sc.shape, F32)

    def update(s, vals):
        m_old = m_sc[...]
        m_new = jnp.maximum(m_old, jnp.max(s, -1, keepdims=True))
        p = jnp.exp2(s - m_new)
        corr = jnp.exp2(m_old - m_new)
        l_sc[...] = corr * l_sc[...] + jnp.sum(p, -1, keepdims=True)
        acc_sc[...] = corr * acc_sc[...] + _dot(p.astype(BF16), vals)
        m_sc[...] = m_new

    lat = jnp.concatenate([r[0, 0].astype(BF16) for r in lat_refs], 0)
    kr_t = jnp.concatenate([r[0, 0].astype(BF16) for r in kr_refs], 1)
    update(_dot_nt(ql, lat) + _dot(qp, kr_t), lat)

    @pl.when(j == pl.num_programs(1) - 1)
    def _fin():
        cn = cn_ref[0].astype(BF16)
        kn = kn_ref[0].astype(BF16)
        s = _dot_nt(ql, cn) + _dot_nt(qp, kn)
        rows = s.shape[0]
        t_q = lax.rem(lax.broadcasted_iota(jnp.int32, (rows, t_new), 0), t_new)
        t_k = lax.broadcasted_iota(jnp.int32, (rows, t_new), 1)
        update(jnp.where(t_k <= t_q, s, NEG), cn)
        o_ref[0] = (acc_sc[...] / l_sc[...]).astype(BF16)


def _ouv_body(o_ref, w_ref, out_ref):
    out_ref[...] = _dot(o_ref[...], w_ref[0]).astype(BF16)


def _rope_tables(pos):
    half = ROPE_DIM // 2
    inv = 1.0 / (ROPE_THETA ** (jnp.arange(half, dtype=F32) * (2.0 / ROPE_DIM)))
    ang = pos.astype(F32)[:, None] * inv
    c, s = jnp.cos(ang), jnp.sin(ang)
    n = pos.shape[0]
    one, zero = jnp.ones((n, NOPE_DIM), F32), jnp.zeros((n, NOPE_DIM), F32)
    pad = jnp.zeros((n, HEAD_PAD - QK_DIM), F32)
    z16 = jnp.zeros((n, half), F32)
    ct = jnp.concatenate([one, c, c, pad], 1)
    s1 = jnp.concatenate([zero, z16, s, pad], 1)
    s2 = jnp.concatenate([zero, -s, z16, pad], 1)
    return ct, s1, s2


def _mla_weights(m, w_dq, q_norm, w_uq, w_dkv, kv_norm, w_uk, w_uv, w_o):
    half = ROPE_DIM // 2
    wq = jnp.pad(w_uq[m], ((0, 0), (0, 0), (0, HEAD_PAD - QK_DIM))).reshape(Q_RANK, MLA_HEADS * HEAD_PAD)
    wk = jnp.pad(w_uk[m], ((0, 0), (0, 0), (0, HEAD_PAD - NOPE_DIM))).reshape(KV_RANK, MLA_HEADS * HEAD_PAD)
    wv = w_uv[m].reshape(KV_RANK, MLA_HEADS * V_DIM)
    wukt = jnp.pad(jnp.transpose(w_uk[m], (1, 2, 0)), ((0, 0), (0, HEAD_PAD - NOPE_DIM), (0, 0)))
    wuv_h = jnp.transpose(w_uv[m], (1, 0, 2)).reshape(MLA_HEADS // 2, 2, KV_RANK, V_DIM)
    eye2 = jnp.eye(2, dtype=F32)
    wuv_pair = jnp.einsum('pirv,ij->pirjv', wuv_h, eye2).reshape(MLA_HEADS // 2, 2 * KV_RANK, 2 * V_DIM)
    del half
    return dict(wdq=w_dq[m].astype(BF16), qn=q_norm[m][None], wq=wq.astype(BF16), wdkv=w_dkv[m].astype(BF16),
                kvn=kv_norm[m][None], wk=wk.astype(BF16), wv=wv.astype(BF16), wukt=wukt.astype(BF16),
                wuv_pair=wuv_pair.astype(BF16), wo=w_o[m].reshape(MLA_HEADS * V_DIM, D_MODEL).astype(BF16))


def _mla_layer(x, lay, tabs, w, m, lat_pool, kr_pool, page_table):
    n_seq, seq, n_dec, t_dec = lay["B"], lay["S"], lay["Bd"], lay["Td"]
    q, k, v, c, kpe = _rows_call(
        _mla_proj_body, "mla_proj", [x, *tabs],
        [w["wdq"], w["qn"], w["wq"], w["wdkv"], w["kvn"], w["wk"], w["wv"]],
        [(MLA_HEADS * HEAD_PAD, BF16), (MLA_HEADS * HEAD_PAD, BF16), (MLA_HEADS * V_DIM, BF16),
         (KV_RANK, F32), (ROPE_DIM, F32)])
    soff, moff = lay["soff"], lay["moff"]
    o_real = _flash(q, k, v, n_seq, seq, 0, min(ATTN_TILE, seq), lead_off=moff)
    o_meta = _flash(q, k, v, n_seq, N_META, moff, N_META)

    n_rows = n_dec * t_dec
    hq = MLA_HEADS * t_dec
    q_lat = pl.pallas_call(
        _qlat_body, grid=(MLA_HEADS,),
        in_specs=[pl.BlockSpec((n_rows, HEAD_PAD), lambda h: (soff // n_rows, h)),
                  pl.BlockSpec((1, HEAD_PAD, KV_RANK), lambda h: (h, 0, 0))],
        out_specs=pl.BlockSpec((1, n_rows, KV_RANK), lambda h: (h, 0, 0)),
        out_shape=jax.ShapeDtypeStruct((MLA_HEADS, n_rows, KV_RANK), BF16),
        compiler_params=_params("parallel"), name="mla_qlat")(q, w["wukt"])
    q_lat = q_lat.reshape(MLA_HEADS, n_dec, t_dec, KV_RANK).transpose(1, 0, 2, 3).reshape(n_dec, hq, KV_RANK)
    q_s = q[soff:soff + n_rows].reshape(n_dec, t_dec, MLA_HEADS, HEAD_PAD)[..., NOPE_DIM:QK_DIM]
    q_pe = q_s.transpose(0, 2, 1, 3).reshape(n_dec, hq, ROPE_DIM)
    c_new = c[soff:soff + n_rows].reshape(n_dec, t_dec, KV_RANK)
    k_new = kpe[soff:soff + n_rows].reshape(n_dec, t_dec, ROPE_DIM)
    n_pages, page = page_table.shape[1], lat_pool.shape[2]
    n_pg = math.gcd(DECODE_PAGES, n_pages)
    kr_pool_t = jnp.swapaxes(kr_pool, 2, 3)
    lat_spec = lambda i: pl.BlockSpec((1, 1, page, KV_RANK), lambda bb, j, pt: (m, pt[bb, j * n_pg + i], 0, 0))
    kr_spec = lambda i: pl.BlockSpec((1, 1, ROPE_DIM, page), lambda bb, j, pt: (m, pt[bb, j * n_pg + i], 0, 0))
    o_lat = pl.pallas_call(
        functools.partial(_decode_body, t_new=t_dec, n_pg=n_pg),
        grid_spec=pltpu.PrefetchScalarGridSpec(
            num_scalar_prefetch=1, grid=(n_dec, n_pages // n_pg),
            in_specs=[pl.BlockSpec((1, hq, KV_RANK), lambda bb, j, pt: (bb, 0, 0)),
                      pl.BlockSpec((1, hq, ROPE_DIM), lambda bb, j, pt: (bb, 0, 0))]
            + [lat_spec(i) for i in range(n_pg)] + [kr_spec(i) for i in range(n_pg)]
            + [pl.BlockSpec((1, t_dec, KV_RANK), lambda bb, j, pt: (bb, 0, 0)),
               pl.BlockSpec((1, t_dec, ROPE_DIM), lambda bb, j, pt: (bb, 0, 0))],
            out_specs=pl.BlockSpec((1, hq, KV_RANK), lambda bb, j, pt: (bb, 0, 0)),
            scratch_shapes=[pltpu.VMEM((hq, 1), F32), pltpu.VMEM((hq, 1), F32), pltpu.VMEM((hq, KV_RANK), F32)]),
        out_shape=jax.ShapeDtypeStruct((n_dec, hq, KV_RANK), BF16),
        compiler_params=_params("parallel", "arbitrary"), name="mla_decode")(
            page_table, q_lat, q_pe, *([lat_pool] * n_pg), *([kr_pool_t] * n_pg), c_new, k_new)
    o_lat = o_lat.reshape(n_dec, MLA_HEADS, t_dec, KV_RANK).transpose(0, 2, 1, 3).reshape(n_rows, MLA_HEADS * KV_RANK)
    o_samp = pl.pallas_call(
        _ouv_body, grid=(MLA_HEADS // 2,),
        in_specs=[pl.BlockSpec((n_rows, 2 * KV_RANK), lambda h: (0, h)),
                  pl.BlockSpec((1, 2 * KV_RANK, 2 * V_DIM), lambda h: (h, 0, 0))],
        out_specs=pl.BlockSpec((n_rows, 2 * V_DIM), lambda h: (0, h)),
        out_shape=jax.ShapeDtypeStruct((n_rows, MLA_HEADS * V_DIM), BF16),
        compiler_params=_params("parallel"), name="mla_ouv")(o_lat, w["wuv_pair"])

    return _assemble(lay, o_real, o_samp, o_meta), c, kpe


def _split_dot(a, sel):
    hi = a.astype(BF16)
    lo = (a - hi.astype(F32)).astype(BF16)
    return _dot(hi, sel) + _dot(lo, sel)


def _head_sum(x, gs, gb):
    return _split_dot(_split_dot(x, gs), gb)


def _softplus(z):
    return jnp.maximum(z, 0.0) + jnp.log(1.0 + jnp.exp(-jnp.abs(z)))


def _rw_proj_body(x_ref, xp_ref, mu_ref, wr_ref, wk_ref, wv_ref, w0_ref, w1_ref, w2_ref, a0_ref, a1_ref, a2_ref,
                  g1_ref, g2_ref, kk_ref, ka_ref, rk_ref, gs_ref, gb_ref,
                  r_o, d_o, k_o, v_o, kk_o, a_o, g_o, bonus_o):
    x = x_ref[...]
    xx = xp_ref[...] - x
    mix = lambda j: (x + xx * mu_ref[j:j + 1, :]).astype(BF16)
    w_lo = _dot(mix(1), w1_ref[...])
    a_lo = _dot(mix(4), a1_ref[...])
    g_lo = _dot(mix(5), g1_ref[...])
    r = _dot(mix(0), wr_ref[...])
    k = _dot(mix(2), wk_ref[...])
    v = _dot(mix(3), wv_ref[...])
    wl = w0_ref[...] + _dot(jnp.tanh(w_lo).astype(BF16), w2_ref[...])
    w = -_softplus(-wl) - 0.5
    a = _sigmoid(a0_ref[...] + _dot(a_lo.astype(BF16), a2_ref[...]))
    g = _dot(_sigmoid(g_lo).astype(BF16), g2_ref[...])
    gs, gb = gs_ref[...], gb_ref[...]
    kk = k * kk_ref[...]
    kk = kk * lax.rsqrt(jnp.maximum(_head_sum(kk * kk, gs, gb), 1e-24))
    k = k * (1.0 + (a - 1.0) * ka_ref[...])
    r_o[...] = r
    d_o[...] = -jnp.exp(w)
    k_o[...] = k
    v_o[...] = v
    kk_o[...] = kk
    a_o[...] = a
    g_o[...] = g
    bonus_o[...] = _head_sum(r * k * rk_ref[...], gs, gb) * v


def _rw_chunk_body(*refs, chunk, n_fac, group):
    s0_ref, y_ref, st_ref, s_sc = refs[6 * group:]
    load = lambda i: jnp.concatenate([refs[i * group + g][...] for g in range(group)], 0)
    ci = pl.program_id(1)

    @pl.when(ci == 0)
    def _init():
        s_sc[...] = s0_ref[...]

    row = lax.broadcasted_iota(jnp.int32, (chunk, chunk), 0)
    col = lax.broadcasted_iota(jnp.int32, (chunk, chunk), 1)
    strict, incl = col < row, col <= row
    rows = group * chunk
    if group == 1:
        same, below = None, incl
    else:
        rr = lax.broadcasted_iota(jnp.int32, (rows, rows), 0)
        cc = lax.broadcasted_iota(jnp.int32, (rows, rows), 1)
        same = (rr // chunk) == (cc // chunk)
        below = jnp.logical_and(same, cc <= rr)
    d_all, k_all, kk_all = load(1), load(2), load(4)
    cum = _dot(below.astype(F32), d_all, HI)
    if group == 1:
        cum_end = cum[chunk - 1:chunk, :]
    else:
        cum_end = _dot(same.astype(F32), d_all, HI)
    be_all = kk_all * load(5)
    w_inv = jnp.exp(-cum)
    w_rem = jnp.exp(cum_end - cum)
    al_t_all = (-kk_all * jnp.exp(cum - d_all)).astype(BF16)
    r_t_all = load(0) * jnp.exp(cum)
    be_h_all = (be_all * w_inv).astype(BF16)
    k_h_all = (k_all * w_inv).astype(BF16)
    be_e_all = (be_all * w_rem).astype(BF16)
    k_e_all = (k_all * w_rem).astype(BF16)
    w_end = jnp.exp(cum_end)
    v_all = load(3).astype(BF16)
    zero = jnp.zeros((chunk, RW_HEAD), BF16)
    chains = [(g, h) for g in range(group) for h in range(RW_HEADS)]
    ids = range(len(chains))
    cut = lambda a, g, h: a[g * chunk:(g + 1) * chunk, h * RW_HEAD:(h + 1) * RW_HEAD]
    gram = [_dot_nt(jnp.concatenate([cut(al_t_all, *c), cut(r_t_all, *c).astype(BF16)], 0),
                    jnp.concatenate([cut(be_h_all, *c), cut(k_h_all, *c)], 0)) for c in chains]
    l_ab = [jnp.where(strict, g[:chunk, :chunk], 0.0).astype(BF16) for g in gram]
    l_ak = [jnp.where(strict, g[:chunk, chunk:], 0.0).astype(BF16) for g in gram]
    m_r = [jnp.concatenate([jnp.where(incl, g[chunk:, :chunk], 0.0),
                            jnp.where(incl, g[chunk:, chunk:], 0.0)], 1).astype(BF16) for g in gram]
    u = [jnp.concatenate([cut(al_t_all, *c).astype(F32), _dot(l_ak[i], cut(v_all, *c))], 1) for i, c in zip(ids, chains)]
    lp = l_ab
    for f in range(n_fac):
        u = [u[i] + _dot(lp[i], u[i].astype(BF16)) for i in ids]
        if f + 1 < n_fac:
            lp = [_dot(lp[i], lp[i]).astype(BF16) for i in ids]
    z = [jnp.concatenate([u[i].astype(BF16), jnp.concatenate([zero, cut(v_all, *c)], 1)], 0) for i, c in zip(ids, chains)]
    g = [_dot(m_r[i], z[i]) for i in ids]
    ad = [_dot_tn(z[i], jnp.concatenate([cut(be_e_all, *c), cut(k_e_all, *c)], 0)) for i, c in zip(ids, chains)]
    s_old = [s_sc[c[0], c[1]] for c in chains]
    s_bf = [s.astype(BF16) for s in s_old]
    y = [_dot_nt((cut(r_t_all, *c) + g[i][:, :RW_HEAD]).astype(BF16), s_bf[i]) + g[i][:, RW_HEAD:]
         for i, c in zip(ids, chains)]
    for gi in range(group):
        y_ref[gi] = jnp.concatenate(y[gi * RW_HEADS:(gi + 1) * RW_HEADS], 1)
    for i, (gi, h) in zip(ids, chains):
        w_h = w_end[gi * chunk:gi * chunk + 1, h * RW_HEAD:(h + 1) * RW_HEAD]
        s_sc[gi, h] = s_old[i] * w_h + _dot(s_bf[i], ad[i][:RW_HEAD].astype(BF16)) + ad[i][RW_HEAD:]

    @pl.when(ci == pl.num_programs(1) - 1)
    def _fin():
        st_ref[...] = s_sc[...]


def _rw_scan(seqs, s0, n_seq, seq_len, row_off, chunk, group=1):
    nc = seq_len // chunk
    assert n_seq % group == 0
    off = row_off // chunk
    n_fac = max(1, math.ceil(math.log2(chunk)))
    row_spec = lambda g: pl.BlockSpec((chunk, D_MODEL), lambda b, c: (off + (b * group + g) * nc + c, 0))
    st_spec = pl.BlockSpec((group, RW_HEADS, RW_HEAD, RW_HEAD), lambda b, c: (b, 0, 0, 0))
    y, st = pl.pallas_call(
        functools.partial(_rw_chunk_body, chunk=chunk, n_fac=n_fac, group=group),
        grid=(n_seq // group, nc),
        in_specs=[row_spec(g) for _ in range(6) for g in range(group)] + [st_spec],
        out_specs=[pl.BlockSpec((group, chunk, D_MODEL), lambda b, c: (b, c, 0)), st_spec],
        out_shape=[jax.ShapeDtypeStruct((n_seq, seq_len, D_MODEL), F32),
                   jax.ShapeDtypeStruct((n_seq, RW_HEADS, RW_HEAD, RW_HEAD), F32)],
        scratch_shapes=[pltpu.VMEM((group, RW_HEADS, RW_HEAD, RW_HEAD), F32)],
        compiler_params=_params("parallel", "arbitrary"), name=f"rwkv_chunk{chunk}")(
            *[a for a in seqs for _ in range(group)], s0)
    return y.reshape(n_seq * seq_len, D_MODEL), st


def _rw_post_body(x_ref, y_ref, g_ref, bonus_ref, lg_ref, lb_ref, gs_ref, gb_ref, wo_ref, ng_ref, nb_ref, o_ref):
    y = y_ref[...]
    gs, gb = gs_ref[...], gb_ref[...]
    mean = _head_sum(y, gs, gb) * (1.0 / RW_HEAD)
    yc = y - mean
    var = _head_sum(yc * yc, gs, gb) * (1.0 / RW_HEAD)
    yn = yc * lax.rsqrt(var + GN_EPS) * lg_ref[...] + lb_ref[...]
    z = ((yn + bonus_ref[...]) * g_ref[...]).astype(BF16)
    o_ref[...] = _layer_norm(ALPHA * x_ref[...] + _dot(z, wo_ref[...]), ng_ref[...], nb_ref[...])


def _rwkv_layer(x, lay, m, p, shift_s, wkv_s, g, b):
    n_seq, seq, n_dec, t_dec = lay["B"], lay["S"], lay["Bd"], lay["Td"]
    soff, moff = lay["soff"], lay["moff"]
    n_rows = n_dec * t_dec
    xm = x[moff:moff + n_seq * N_META].reshape(n_seq, N_META, D_MODEL)
    xs = x[soff:soff + n_rows].reshape(n_dec, t_dec, D_MODEL)
    prev_s = jnp.concatenate([shift_s[m][:, None], xs[:, :-1]], 1).reshape(n_rows, D_MODEL)
    xprev = jnp.concatenate([jnp.zeros((1, D_MODEL), F32), x[:-1]], 0)
    xprev = lax.dynamic_update_slice(xprev, prev_s, (soff, 0))
    for bi in range(n_seq):
        xprev = lax.dynamic_update_slice(xprev, xm[bi, -1:], (bi * seq, 0))
        xprev = lax.dynamic_update_slice(xprev, jnp.zeros((1, D_MODEL), F32), (moff + bi * N_META, 0))
    x_last = jnp.stack([x[(bi + 1) * seq - 1] for bi in range(n_seq)])

    row = lambda a: a[m][None]
    gs = jnp.pad(jnp.repeat(jnp.eye(RW_HEADS, dtype=BF16), RW_HEAD, 0), ((0, 0), (0, LANES - RW_HEADS)))
    gb = gs.T
    bf = lambda a: a[m].astype(BF16)
    r, d, k, v, kk, a, gate, bonus = _rows_call(
        _rw_proj_body, "rwkv_proj", [x, xprev],
        [p['rw_mu'][m], bf(p['rw_wr']), bf(p['rw_wk']), bf(p['rw_wv']), row(p['rw_w0']), bf(p['rw_w1']),
         bf(p['rw_w2']), row(p['rw_a0']), bf(p['rw_a1']), bf(p['rw_a2']), bf(p['rw_g1']), bf(p['rw_g2']),
         row(p['rw_k_k']), row(p['rw_k_a']), p['rw_r_k'][m].reshape(1, D_MODEL), gs, gb],
        [(D_MODEL, F32)] * 8)
    seqs = (r, d, k, v, kk, a)
    zero_state = jnp.zeros((n_seq, RW_HEADS, RW_HEAD, RW_HEAD), F32)
    y_meta, st = _rw_scan(seqs, zero_state, n_seq, N_META, moff, N_META, group=n_seq)
    y_real, st_p = _rw_scan(seqs, st, n_seq, seq, 0, min(RW_CHUNK, seq), group=n_seq)
    t_pad = 8
    pad_rows = lambda z: jnp.pad(z[soff:soff + n_rows].reshape(n_dec, t_dec, D_MODEL),
                                 ((0, 0), (0, t_pad - t_dec), (0, 0))).reshape(n_dec * t_pad, D_MODEL)
    y_s, st_s = _rw_scan(tuple(pad_rows(z) for z in seqs), wkv_s[m], n_dec, t_pad, 0, t_pad,
                         group=math.gcd(RW_GROUP, n_dec))
    y_s = y_s.reshape(n_dec, t_pad, D_MODEL)[:, :t_dec].reshape(n_rows, D_MODEL)
    y_all = _assemble(lay, y_real, y_s, y_meta)
    x_new = _rows_call(
        _rw_post_body, "rwkv_post", [x, y_all, gate, bonus],
        [row(p['rw_lnx_g']), row(p['rw_lnx_b']), gs, gb, bf(p['rw_wo']), g, b], [(D_MODEL, F32)])[0]
    return x_new, st_p, x_last, st_s, xs[:, -1]


def _s5_disc_body(lre_ref, lim_ref, dt_ref, bre_ref, bim_ref, lre_s, lim_s, dt_s, bbr_o, bbi_o, ar_o, ai_o):
    def disc(lre, lim, dt):
        mag = jnp.exp(lre * dt)
        ab_re, ab_im = mag * jnp.cos(lim * dt), mag * jnp.sin(lim * dt)
        den = lre * lre + lim * lim
        nr = ab_re - 1.0
        return ab_re, ab_im, (nr * lre + ab_im * lim) / den, (ab_im * lre - nr * lim) / den

    _, _, co_re, co_im = disc(lre_ref[...], lim_ref[...], dt_ref[...])
    b_re, b_im = bre_ref[...], bim_ref[...]
    bbr_o[...] = co_re * b_re - co_im * b_im
    bbi_o[...] = co_re * b_im + co_im * b_re
    ab_re, ab_im, _, _ = disc(lre_s[...], lim_s[...], dt_s[...])
    ar_o[...] = ab_re
    ai_o[...] = ab_im


def _s5_input(u, bre_ref, bim_ref):
    ub = u.astype(BF16)
    w = S5_WIDTH // S5_BLOCKS
    cw = D_MODEL // S5_BLOCKS
    re = [_dot(ub[:, j * cw:(j + 1) * cw], bre_ref[j]) for j in range(S5_BLOCKS)]
    im = [_dot(ub[:, j * cw:(j + 1) * cw], bim_ref[j]) for j in range(S5_BLOCKS)]
    del w
    return re, im


def _s5_output(u, hre, him, cre_ref, cim_ref, dv_ref, wv_ref, wg_ref, g_ref, b_ref):
    w = S5_WIDTH // S5_BLOCKS
    hb_re, hb_im = hre.astype(BF16), him.astype(BF16)
    ys = [_dot(hb_re[:, j * w:(j + 1) * w], cre_ref[j]) - _dot(hb_im[:, j * w:(j + 1) * w], cim_ref[j])
          for j in range(S5_BLOCKS)]
    y = jnp.concatenate(ys, 1) + dv_ref[...] * u
    z = jax.nn.gelu(y).astype(BF16)
    out = _dot(z, wv_ref[...]) * _sigmoid(_dot(z, wg_ref[...]))
    return _layer_norm(ALPHA * u + out, g_ref[...], b_ref[...])


def _s5_seq_body(x_ref, h0r_ref, h0i_ref, ar_ref, ai_ref, bre_ref, bim_ref, cre_ref, cim_ref, dv_ref, wv_ref, wg_ref,
                 g_ref, b_ref, o_ref, htr_ref, hti_ref, hre_sc, him_sc, sr_sc, si_sc, *, tt):
    ti = pl.program_id(1)

    @pl.when(ti == 0)
    def _init():
        sr_sc[...] = h0r_ref[0]
        si_sc[...] = h0i_ref[0]

    u = x_ref[...]
    re, im = _s5_input(u, bre_ref, bim_ref)
    w = S5_WIDTH // S5_BLOCKS
    for j in range(S5_BLOCKS):
        hre_sc[:, j * w:(j + 1) * w] = re[j]
        him_sc[:, j * w:(j + 1) * w] = im[j]
    ar, ai = ar_ref[...], ai_ref[...]

    def step(t, carry):
        hr, hi = carry
        nr = ar * hr - ai * hi + hre_sc[pl.ds(t, 1), :]
        ni = ar * hi + ai * hr + him_sc[pl.ds(t, 1), :]
        hre_sc[pl.ds(t, 1), :] = nr
        him_sc[pl.ds(t, 1), :] = ni
        return nr, ni

    hr, hi = lax.fori_loop(0, tt, step, (sr_sc[...], si_sc[...]), unroll=S5_UNROLL)
    sr_sc[...] = hr
    si_sc[...] = hi
    o_ref[...] = _s5_output(u, hre_sc[...], him_sc[...], cre_ref, cim_ref, dv_ref, wv_ref, wg_ref, g_ref, b_ref)

    @pl.when(ti == pl.num_programs(1) - 1)
    def _fin():
        htr_ref[0] = hr
        hti_ref[0] = hi


def _s5_step_body(x_ref, h0r_ref, h0i_ref, ar_ref, ai_ref, bre_ref, bim_ref, cre_ref, cim_ref, dv_ref, wv_ref, wg_ref,
                  g_ref, b_ref, o_ref, htr_ref, hti_ref, sr_sc, si_sc):
    t = pl.program_id(0)

    @pl.when(t == 0)
    def _init():
        sr_sc[...] = h0r_ref[...]
        si_sc[...] = h0i_ref[...]

    u = x_ref[0]
    re, im = _s5_input(u, bre_ref, bim_ref)
    bur, bui = jnp.concatenate(re, 1), jnp.concatenate(im, 1)
    ar, ai = ar_ref[...], ai_ref[...]
    hr, hi = sr_sc[...], si_sc[...]
    nr = ar * hr - ai * hi + bur
    ni = ar * hi + ai * hr + bui
    sr_sc[...] = nr
    si_sc[...] = ni
    o_ref[0] = _s5_output(u, nr, ni, cre_ref, cim_ref, dv_ref, wv_ref, wg_ref, g_ref, b_ref)

    @pl.when(t == pl.num_programs(0) - 1)
    def _fin():
        htr_ref[...] = nr
        hti_ref[...] = ni


def _s5_weights(m, p):
    rep = lambda a: jnp.repeat(a, S5_GROUP, axis=-1)
    lre, lim = p['s5_lam_re'][m], p['s5_lam_im'][m]
    dt = jnp.broadcast_to(jnp.exp(p['s5_log_dt'][m])[:, None], (S5_GROUPS, S5_STATE))
    flat = lambda a: a[m].reshape(S5_GROUPS, S5_STATE * S5_GROUP)
    n = S5_STATE * S5_GROUP
    bb_re, bb_im, ab_re, ab_im = pl.pallas_call(
        _s5_disc_body,
        out_shape=[jax.ShapeDtypeStruct((S5_GROUPS, n), F32)] * 2 + [jax.ShapeDtypeStruct((S5_GROUPS, S5_STATE), F32)] * 2,
        name="s5_disc")(rep(lre), rep(lim), rep(dt), flat(p['s5_b_re']), flat(p['s5_b_im']), lre, lim, dt)
    gpb = S5_GROUPS // S5_BLOCKS
    eye = jnp.eye(gpb, dtype=F32)

    def in_blocks(bb):
        t = bb.reshape(S5_BLOCKS, gpb, S5_STATE, S5_GROUP).transpose(0, 1, 3, 2)
        return jnp.einsum('jgsp,gh->jgshp', t, eye).reshape(S5_BLOCKS, gpb * S5_GROUP, gpb * S5_STATE).astype(BF16)

    def out_blocks(c):
        t = c.reshape(S5_BLOCKS, gpb, S5_GROUP, S5_STATE).transpose(0, 1, 3, 2)
        return jnp.einsum('jgps,gh->jgphs', t, eye).reshape(S5_BLOCKS, gpb * S5_STATE, gpb * S5_GROUP).astype(BF16)

    return dict(ar=ab_re.reshape(1, S5_WIDTH), ai=ab_im.reshape(1, S5_WIDTH),
                bre=in_blocks(bb_re), bim=in_blocks(bb_im),
                cre=out_blocks(p['s5_c_re'][m]), cim=out_blocks(p['s5_c_im'][m]),
                dv=p['s5_d'][m][None], wv=p['s5_wv'][m].astype(BF16), wg=p['s5_wg'][m].astype(BF16))


def _s5_seq(x, h0r, h0i, w, g, b, n_seq, seq_len, row_off, tt):
    nt = seq_len // tt
    off = row_off // tt
    consts = [w["ar"], w["ai"], w["bre"], w["bim"], w["cre"], w["cim"], w["dv"], w["wv"], w["wg"], g, b]
    st_spec = pl.BlockSpec((1, 1, S5_WIDTH), lambda bb, t: (bb, 0, 0))
    return pl.pallas_call(
        functools.partial(_s5_seq_body, tt=tt), grid=(n_seq, nt),
        in_specs=[pl.BlockSpec((tt, D_MODEL), lambda bb, t: (off + bb * nt + t, 0)), st_spec, st_spec]
        + [_const_spec(a, 2) for a in consts],
        out_specs=[pl.BlockSpec((tt, D_MODEL), lambda bb, t: (bb * nt + t, 0)), st_spec, st_spec],
        out_shape=[jax.ShapeDtypeStruct((n_seq * seq_len, D_MODEL), F32)]
        + [jax.ShapeDtypeStruct((n_seq, 1, S5_WIDTH), F32)] * 2,
        scratch_shapes=[pltpu.VMEM((tt, S5_WIDTH), F32), pltpu.VMEM((tt, S5_WIDTH), F32),
                        pltpu.VMEM((1, S5_WIDTH), F32), pltpu.VMEM((1, S5_WIDTH), F32)],
        compiler_params=_params("parallel", "arbitrary"), name=f"s5_seq{tt}")(x, h0r, h0i, *consts)


def _s5_layer(x, lay, m, p, re_s, im_s, g, b):
    n_seq, seq, n_dec, t_dec = lay["B"], lay["S"], lay["Bd"], lay["Td"]
    soff, moff = lay["soff"], lay["moff"]
    n_rows = n_dec * t_dec
    w = _s5_weights(m, p)
    zero = jnp.zeros((n_seq, 1, S5_WIDTH), F32)
    o_meta, hr, hi = _s5_seq(x, zero, zero, w, g, b, n_seq, N_META, moff, N_META)
    o_real, hr, hi = _s5_seq(x, hr, hi, w, g, b, n_seq, seq, 0, min(S5_TILE, seq))
    xs = x[soff:soff + n_rows].reshape(n_dec, t_dec, D_MODEL).transpose(1, 0, 2)
    consts = [w["ar"], w["ai"], w["bre"], w["bim"], w["cre"], w["cim"], w["dv"], w["wv"], w["wg"], g, b]
    st_spec = pl.BlockSpec((n_dec, S5_WIDTH), lambda t: (0, 0))
    o_s, sr, si = pl.pallas_call(
        _s5_step_body, grid=(t_dec,),
        in_specs=[pl.BlockSpec((1, n_dec, D_MODEL), lambda t: (t, 0, 0)), st_spec, st_spec]
        + [_const_spec(a, 1) for a in consts],
        out_specs=[pl.BlockSpec((1, n_dec, D_MODEL), lambda t: (t, 0, 0)), st_spec, st_spec],
        out_shape=[jax.ShapeDtypeStruct((t_dec, n_dec, D_MODEL), F32)]
        + [jax.ShapeDtypeStruct((n_dec, S5_WIDTH), F32)] * 2,
        scratch_shapes=[pltpu.VMEM((n_dec, S5_WIDTH), F32), pltpu.VMEM((n_dec, S5_WIDTH), F32)],
        compiler_params=_params("arbitrary"), name="s5_step")(
            xs, re_s[m].reshape(n_dec, S5_WIDTH), im_s[m].reshape(n_dec, S5_WIDTH), *consts)
    o_s = o_s.transpose(1, 0, 2).reshape(n_rows, D_MODEL)
    x_new = _assemble(lay, o_real, o_s, o_meta)
    shp = (S5_GROUPS, S5_STATE)
    return x_new, hr.reshape(n_seq, *shp), hi.reshape(n_seq, *shp), sr.reshape(n_dec, *shp), si.reshape(n_dec, *shp)


def _round_up(n, m):
    return (n + m - 1) // m * m


def kernel(x_prompt, x_sample, cache_mla_latent, cache_mla_krope, state_rwkv_wkv, state_rwkv_shift, state_s5_re, state_s5_im, page_table, meta_tokens, ln_g, ln_b, ffn_w1, ffn_w3, ffn_w2, mla_w_dq, mla_q_norm, mla_w_uq, mla_w_dkv, mla_kv_norm, mla_w_uk, mla_w_uv, mla_w_o, rw_mu, rw_wr, rw_wk, rw_wv, rw_w0, rw_w1, rw_w2, rw_a0, rw_a1, rw_a2, rw_g1, rw_g2, rw_k_k, rw_k_a, rw_r_k, rw_lnx_g, rw_lnx_b, rw_wo, s5_lam_re, s5_lam_im, s5_log_dt, s5_b_re, s5_b_im, s5_c_re, s5_c_im, s5_d, s5_wv, s5_wg):
    p = dict(rw_mu=rw_mu, rw_wr=rw_wr, rw_wk=rw_wk, rw_wv=rw_wv, rw_w0=rw_w0, rw_w1=rw_w1, rw_w2=rw_w2,
             rw_a0=rw_a0, rw_a1=rw_a1, rw_a2=rw_a2, rw_g1=rw_g1, rw_g2=rw_g2, rw_k_k=rw_k_k, rw_k_a=rw_k_a,
             rw_r_k=rw_r_k, rw_lnx_g=rw_lnx_g, rw_lnx_b=rw_lnx_b, rw_wo=rw_wo,
             s5_lam_re=s5_lam_re, s5_lam_im=s5_lam_im, s5_log_dt=s5_log_dt, s5_b_re=s5_b_re, s5_b_im=s5_b_im,
             s5_c_re=s5_c_re, s5_c_im=s5_c_im, s5_d=s5_d, s5_wv=s5_wv, s5_wg=s5_wg)
    n_seq, seq, _ = x_prompt.shape
    n_dec, t_dec, _ = x_sample.shape
    past = page_table.shape[1] * cache_mla_latent.shape[2]
    n_rows = n_dec * t_dec
    soff = _round_up(n_seq * seq, ROW_ALIGN)
    moff = _round_up(soff + n_rows, ROW_ALIGN)
    n_tot = _round_up(moff + n_seq * N_META, ROW_ALIGN)
    lay = dict(B=n_seq, S=seq, Bd=n_dec, Td=t_dec, soff=soff, moff=moff, n_tot=n_tot)
    assemble = functools.partial(_assemble, lay)

    meta = jnp.broadcast_to(meta_tokens[None], (n_seq, N_META, D_MODEL)).reshape(n_seq * N_META, D_MODEL)
    x = assemble(x_prompt.reshape(n_seq * seq, D_MODEL), x_sample.reshape(n_rows, D_MODEL), meta)
    pos = assemble(jnp.tile(N_META + jnp.arange(seq, dtype=jnp.int32), n_seq)[:, None],
                   jnp.tile(past + jnp.arange(t_dec, dtype=jnp.int32), n_dec)[:, None],
                   jnp.tile(jnp.arange(N_META, dtype=jnp.int32), n_seq)[:, None])[:, 0]
    tabs = _rope_tables(pos)

    stacked = lambda a: a.reshape((-1,) + a.shape[2:])
    w1_all, w3_all, w2_all = (stacked(a).astype(BF16) for a in (ffn_w1, ffn_w3, ffn_w2))
    g_all, b_all = (a.reshape(-1, 1, D_MODEL) for a in (ln_g, ln_b))

    lat_p, kr_p, lat_s, kr_s = [], [], [], []
    wkv_p = sh_p = wkv_s = sh_s = re_p = im_p = re_s = im_s = None
    stacks = (w1_all, w3_all, w2_all, g_all, b_all)
    ffn_a = lambda i: (2 * i, 3 * i)
    ffn_b = lambda i: (2 * i + 1, 3 * i + 2)
    x = _chain(x, stacks, [ffn_a(0)])
    for i in range(DEPTH):
        kind, m = i % 3, i // 3
        norm = lambda j: (ln_g[i, j][None], ln_b[i, j][None])
        tail = [ffn_b(i)] + ([ffn_a(i + 1)] if i + 1 < DEPTH else [])
        mix = None
        if kind == 0:
            w = _mla_weights(m, mla_w_dq, mla_q_norm, mla_w_uq, mla_w_dkv, mla_kv_norm, mla_w_uk, mla_w_uv, mla_w_o)
            o_all, c, kpe = _mla_layer(x, lay, tabs, w, m, cache_mla_latent, cache_mla_krope, page_table)
            mix = (o_all, w["wo"], 3 * i + 1)
            split = lambda a: (jnp.concatenate(
                [a[moff:moff + n_seq * N_META].reshape(n_seq, N_META, -1), a[:n_seq * seq].reshape(n_seq, seq, -1)], 1),
                a[soff:soff + n_rows].reshape(n_dec, t_dec, -1))
            cp, cs = split(c)
            kp, ks = split(kpe)
            lat_p.append(cp), lat_s.append(cs), kr_p.append(kp), kr_s.append(ks)
        elif kind == 1:
            x, wkv_p, sh_p, wkv_s, sh_s = _rwkv_layer(x, lay, m, p, state_rwkv_shift, state_rwkv_wkv, *norm(1))
        else:
            x, re_p, im_p, re_s, im_s = _s5_layer(x, lay, m, p, state_s5_re, state_s5_im, *norm(1))
        x = _chain(x, stacks, tail, mix)

    y_prompt = x[:n_seq * seq].reshape(n_seq, seq, D_MODEL)
    y_sample = x[soff:soff + n_rows].reshape(n_dec, t_dec, D_MODEL)
    return (y_prompt, y_sample, jnp.stack(lat_p), jnp.stack(kr_p), jnp.stack(lat_s), jnp.stack(kr_s),
            wkv_p[None], sh_p[None], wkv_s[None], sh_s[None], re_p[None], im_p[None], re_s[None], im_s[None])
```

```python
import functools
import math

import jax
import jax.numpy as jnp
from jax import lax
from jax.experimental import pallas as pl
from jax.experimental.pallas import tpu as pltpu

F32 = jnp.float32
BF16 = jnp.bfloat16
HI = lax.Precision.HIGHEST

D_MODEL = 1024
DEPTH = 4
N_META = 16
D_FF = 2816
ALPHA = (2 * DEPTH) ** 0.25
LN_EPS = 1e-5
RMS_EPS = 1e-6
MLA_HEADS = 16
Q_RANK = 768
KV_RANK = 256
NOPE_DIM = 64
ROPE_DIM = 32
QK_DIM = NOPE_DIM + ROPE_DIM
V_DIM = 64
ROPE_THETA = 10000.0
RW_HEAD = 64
RW_HEADS = D_MODEL // RW_HEAD
GN_EPS = 64e-5
S5_GROUP = 16
S5_GROUPS = D_MODEL // S5_GROUP
S5_STATE = 64
S5_WIDTH = S5_GROUPS * S5_STATE
S5_BLOCKS = 8

LANES = 128
HEAD_PAD = 128
ROW_TILE = 256
ROW_ALIGN = 256
ATTN_TILE = 1024
ATTN_SUB = 256
ATTN_AHEAD = 2
DECODE_PAGES = 64
RW_CHUNK = 64
RW_GROUP = 4
S5_TILE = 256
S5_UNROLL = 8
VMEM_LIMIT = 56 * 1024 * 1024
NEG = -1e30


def _dot(a, b, prec=None):
    return jnp.dot(a, b, preferred_element_type=F32, precision=prec)


def _dot_nt(a, b, prec=None):
    return lax.dot_general(a, b, (((1,), (1,)), ((), ())), preferred_element_type=F32, precision=prec)


def _dot_tn(a, b, prec=None):
    return lax.dot_general(a, b, (((0,), (0,)), ((), ())), preferred_element_type=F32, precision=prec)


def _layer_norm(y, g, b):
    mu = jnp.mean(y, -1, keepdims=True)
    yc = y - mu
    var = jnp.mean(yc * yc, -1, keepdims=True)
    return yc * lax.rsqrt(var + LN_EPS) * g + b


def _rms_norm(y, g):
    return y * lax.rsqrt(jnp.mean(y * y, -1, keepdims=True) + RMS_EPS) * g


def _sigmoid(x):
    return 1.0 / (1.0 + jnp.exp(-x))


def _params(*sem):
    return pltpu.CompilerParams(dimension_semantics=sem, vmem_limit_bytes=VMEM_LIMIT)


def _const_spec(a, n_grid):
    zeros = (0,) * a.ndim
    if n_grid == 1:
        imap = lambda i: zeros
    elif n_grid == 2:
        imap = lambda i, j: zeros
    else:
        imap = lambda i, j, k, l: zeros
    return pl.BlockSpec(a.shape, imap, pipeline_mode=pl.Buffered(1))


def _pick_spec(a, idx):
    tail = (0,) * (a.ndim - 1)
    return pl.BlockSpec((None,) + a.shape[1:], lambda i: (idx,) + tail, pipeline_mode=pl.Buffered(1))


def _rows_call(body, name, row_ins, const_ins, outs, tm=ROW_TILE):
    n = row_ins[0].shape[0]
    assert n % tm == 0
    in_specs = [pl.BlockSpec((tm, a.shape[1]), lambda i: (i, 0)) for a in row_ins]
    in_specs += [_pick_spec(*a) if isinstance(a, tuple) else _const_spec(a, 1) for a in const_ins]
    consts = [a[0] if isinstance(a, tuple) else a for a in const_ins]
    out_specs = [pl.BlockSpec((tm, w), lambda i: (i, 0)) for w, _ in outs]
    out_shape = [jax.ShapeDtypeStruct((n, w), dt) for w, dt in outs]
    res = pl.pallas_call(body, grid=(n // tm,), in_specs=in_specs, out_specs=out_specs, out_shape=out_shape,
                         compiler_params=_params("parallel"), name=name)(*row_ins, *consts)
    return res


def _assemble(lay, real, samp, meta):
    w = real.shape[1]
    gap = lambda n: jnp.zeros((n, w), real.dtype)
    return jnp.concatenate([real, gap(lay["soff"] - real.shape[0]), samp,
                            gap(lay["moff"] - lay["soff"] - samp.shape[0]), meta,
                            gap(lay["n_tot"] - lay["moff"] - meta.shape[0])], 0)


FF_CHUNK = 1408


def _ffn_stage(x, w1_ref, w3_ref, w2_ref, g_ref, b_ref):
    xb = x.astype(BF16)
    acc = None
    for c in range(D_FF // FF_CHUNK):
        sl = slice(c * FF_CHUNK, (c + 1) * FF_CHUNK)
        h1 = _dot(xb, w1_ref[:, sl])
        h3 = _dot(xb, w3_ref[:, sl])
        part = _dot((h1 * _sigmoid(h1) * h3).astype(BF16), w2_ref[sl, :])
        acc = part if acc is None else acc + part
    return _layer_norm(ALPHA * x + 0.5 * acc, g_ref[...], b_ref[...])


def _proj_mix(x, h_ref, w_ref, g_ref, b_ref):
    return _layer_norm(ALPHA * x + _dot(h_ref[...], w_ref[...]), g_ref[...], b_ref[...])


def _chain_body(*refs, n_ffn, mix_fn, n_mix):
    x = refs[0][...]
    pos = 1
    if mix_fn is not None:
        x = mix_fn(x, *refs[1:1 + n_mix])
        pos += n_mix
    for _ in range(n_ffn):
        x = _ffn_stage(x, *refs[pos:pos + 5])
        pos += 5
    refs[pos][...] = x


def _chain(x, stacks, ffn_ids, mix=None):
    w1_all, w3_all, w2_all, g_all, b_all = stacks
    rows, consts, mix_fn, n_mix = [x], [], None, 0
    if mix is not None:
        mix_fn, mix_rows, mix_consts = mix
        rows += mix_rows
        consts += mix_consts
        n_mix = len(mix_rows) + len(mix_consts)
    for wi, ni in ffn_ids:
        consts += [(w1_all, wi), (w3_all, wi), (w2_all, wi), (g_all, ni), (b_all, ni)]
    body = functools.partial(_chain_body, n_ffn=len(ffn_ids), mix_fn=mix_fn, n_mix=n_mix)
    name = ("mix_" if mix is not None else "") + "ffn" * len(ffn_ids)
    return _rows_call(body, name, rows, consts, [(D_MODEL, F32)])[0]


def _mla_proj_body(x_ref, ct_ref, s1_ref, s2_ref, wdq_ref, qn_ref, wq_ref, wdkv_ref, kvn_ref, wk_ref, wv_ref,
                   q_ref, k_ref, v_ref, c_ref, kpe_ref):
    xb = x_ref[...].astype(BF16)
    ct, s1, s2 = ct_ref[...], s1_ref[...], s2_ref[...]
    dq = _dot(xb, wdq_ref[...])
    ckv = _dot(xb, wdkv_ref[...])
    cq = _rms_norm(dq, qn_ref[...])
    c = _rms_norm(ckv[:, :KV_RANK], kvn_ref[...])
    q = _dot(cq.astype(BF16), wq_ref[...])
    half = ROPE_DIM // 2
    k1, k2 = ckv[:, KV_RANK:KV_RANK + half], ckv[:, KV_RANK + half:]
    cos, sin = ct[:, NOPE_DIM:NOPE_DIM + half], s1[:, NOPE_DIM + half:QK_DIM]
    kpe = jnp.concatenate([k1 * cos - k2 * sin, k1 * sin + k2 * cos], -1)
    c_ref[...] = c
    kpe_ref[...] = kpe
    cb = c.astype(BF16)
    knope = _dot(cb, wk_ref[...])
    v_ref[...] = _dot(cb, wv_ref[...]).astype(BF16)
    tm = kpe.shape[0]
    kpe_slot = jnp.concatenate([jnp.zeros((tm, NOPE_DIM), F32), kpe, jnp.zeros((tm, HEAD_PAD - QK_DIM), F32)], -1)
    scale = QK_DIM ** -0.5 * math.log2(math.e)
    for h in range(MLA_HEADS):
        sl = slice(h * HEAD_PAD, (h + 1) * HEAD_PAD)
        blk = q[:, sl]
        rot = blk * ct + pltpu.roll(blk, half, 1) * s1 + pltpu.roll(blk, HEAD_PAD - half, 1) * s2
        q_ref[:, sl] = (rot * scale).astype(BF16)
        k_ref[:, sl] = (knope[:, sl] + kpe_slot).astype(BF16)


def _flash_body(qt_ref, kt_ref, *refs, tq, tk, sub, ahead, has_lead):
    if has_lead:
        q_ref, k_ref, v_ref, mk_ref, mv_ref, o_ref, m_sc, acc_sc = refs
    else:
        q_ref, k_ref, v_ref, o_ref, m_sc, acc_sc = refs
    t = pl.program_id(2)
    qi, ki = qt_ref[t], kt_ref[t]
    blocks = [(hh, r0) for r0 in range(0, tq, sub) for hh in range(2)]

    def q_blk(hh, r0):
        return q_ref[r0:r0 + sub, hh * HEAD_PAD:(hh + 1) * HEAD_PAD]

    def weights(hh, r0, s, first):
        rows = slice(r0, r0 + sub)
        width = s.shape[1]
        m_cur = jnp.max(s, -1, keepdims=True)
        if first:
            m_new = jnp.broadcast_to(m_cur, (sub, LANES))
        else:
            m_old = m_sc[hh, rows, :]
            m_new = jnp.maximum(m_old, m_cur)
        m_wide = jnp.concatenate([m_new] * (width // LANES), 1) if width >= LANES else m_new[:, :width]
        p = jnp.exp2(s - m_wide)
        corr = None if first else jnp.exp2(m_old - m_new)
        m_sc[hh, rows, :] = m_new
        return p.astype(BF16), corr

    def attend(score_fn, vals, first=False):
        lane = lax.broadcasted_iota(jnp.int32, vals.shape, 1)
        one = jnp.ones(vals.shape, BF16)
        vals_h = [jnp.where(lane < V_DIM, vals, one), jnp.where(lane < V_DIM, one, vals)]
        scores = {}
        for i in range(len(blocks) + ahead):
            if i < len(blocks):
                scores[i] = score_fn(*blocks[i])
            if i >= ahead:
                hh, r0 = blocks[i - ahead]
                p, corr = weights(hh, r0, scores.pop(i - ahead), first)
                rows = slice(r0, r0 + sub)
                pv = _dot(p, vals_h[hh][:p.shape[1]])
                acc_sc[hh, rows, :] = pv if first else corr * acc_sc[hh, rows, :] + pv

    @pl.when(ki == 0)
    def _init():
        if has_lead:
            attend(lambda hh, r0: _dot_nt(q_blk(hh, r0), mk_ref[:, hh * HEAD_PAD:(hh + 1) * HEAD_PAD]),
                   mv_ref[...], first=True)
        else:
            m_sc[...] = jnp.full(m_sc.shape, NEG, F32)
            acc_sc[...] = jnp.zeros(acc_sc.shape, F32)

    def step(masked):
        def score(hh, r0):
            kw = r0 + sub if masked else tk
            s = _dot_nt(q_blk(hh, r0), k_ref[:kw, hh * HEAD_PAD:(hh + 1) * HEAD_PAD])
            if masked:
                row = r0 + lax.broadcasted_iota(jnp.int32, (sub, kw), 0)
                col = lax.broadcasted_iota(jnp.int32, (sub, kw), 1)
                s = jnp.where(col <= row, s, NEG)
            return s
        attend(score, v_ref[...])

    @pl.when(ki < qi)
    def _below():
        step(False)

    @pl.when(ki == qi)
    def _diag():
        step(True)
        lane = lax.broadcasted_iota(jnp.int32, (tq, 2 * V_DIM), 1)
        a0, a1 = acc_sc[0], acc_sc[1]
        o0 = a0 / pltpu.roll(a0, V_DIM, 1)
        o1 = a1 / pltpu.roll(a1, V_DIM, 1)
        o_ref[...] = jnp.where(lane < V_DIM, o0, o1).astype(BF16)


def _flash(q, k, v, n_seq, seq_len, row_off, tile, lead_off=None):
    tq = tk = tile
    nq = seq_len // tq
    qo = row_off // tq
    has_lead = lead_off is not None
    pairs = [(i, j) for i in range(nq) for j in range(i + 1)]
    qt = jnp.array([i for i, _ in pairs], jnp.int32)
    kt = jnp.array([j for _, j in pairs], jnp.int32)
    in_specs = [
        pl.BlockSpec((tq, 2 * HEAD_PAD), lambda b, h, t, qt, kt: (qo + b * nq + qt[t], h)),
        pl.BlockSpec((tk, 2 * HEAD_PAD), lambda b, h, t, qt, kt: (qo + b * nq + kt[t], h)),
        pl.BlockSpec((tk, 2 * V_DIM), lambda b, h, t, qt, kt: (qo + b * nq + kt[t], h)),
    ]
    args = [q, k, v]
    if has_lead:
        lo = lead_off // N_META
        in_specs += [pl.BlockSpec((N_META, 2 * HEAD_PAD), lambda b, h, t, qt, kt: (lo + b, h)),
                     pl.BlockSpec((N_META, 2 * V_DIM), lambda b, h, t, qt, kt: (lo + b, h))]
        args += [k, v]
    return pl.pallas_call(
        functools.partial(_flash_body, tq=tq, tk=tk, sub=min(ATTN_SUB, tq), ahead=ATTN_AHEAD, has_lead=has_lead),
        grid_spec=pltpu.PrefetchScalarGridSpec(
            num_scalar_prefetch=2, grid=(n_seq, MLA_HEADS // 2, len(pairs)),
            in_specs=in_specs,
            out_specs=pl.BlockSpec((tq, 2 * V_DIM), lambda b, h, t, qt, kt: (b * nq + qt[t], h)),
            scratch_shapes=[pltpu.VMEM((2, tq, LANES), F32), pltpu.VMEM((2, tq, 2 * V_DIM), F32)]),
        out_shape=jax.ShapeDtypeStruct((n_seq * seq_len, MLA_HEADS * V_DIM), BF16),
        compiler_params=_params("parallel", "parallel", "arbitrary"),
        name="mla_flash_lead" if has_lead else "mla_flash")(qt, kt, *args)


def _qlat_body(q_ref, w_ref, o_ref):
    o_ref[0] = _dot(q_ref[...], w_ref[0]).astype(BF16)


def _decode_body(pt_ref, ql_ref, qp_ref, *refs, t_new, n_pg):
    lat_refs, kr_refs = refs[:n_pg], refs[n_pg:2 * n_pg]
    cn_ref, kn_ref, o_ref, m_sc, l_sc, acc_sc = refs[2 * n_pg:]
    j = pl.program_id(1)
    ql, qp = ql_ref[0], qp_ref[0]

    @pl.when(j == 0)
    def _init():
        m_sc[...] = jnp.full(m_sc.shape, NEG, F32)
        l_sc[...] = jnp.zeros(l_sc.shape, F32)
        acc_sc[...] = jnp.zeros(acc_sc.shape, F32)

    def update(s, vals):
        m_old = m_sc[...]
        m_new = jnp.maximum(m_old, jnp.max(s, -1, keepdims=True))
        p = jnp.exp2(s - m_new)
        corr = jnp.exp2(m_old - m_new)
        l_sc[...] = corr * l_sc[...] + jnp.sum(p, -1, keepdims=True)
        acc_sc[...] = corr * acc_sc[...] + _dot(p.astype(BF16), vals)
        m_sc[...] = m_new

    lat = jnp.concatenate([r[0, 0].astype(BF16) for r in lat_refs], 0)
    kr_t = jnp.concatenate([r[0, 0].astype(BF16) for r in kr_refs], 1)
    update(_dot_nt(ql, lat) + _dot(qp, kr_t), lat)

    @pl.when(j == pl.num_programs(1) - 1)
    def _fin():
        cn = cn_ref[0].astype(BF16)
        kn = kn_ref[0].astype(BF16)
        s = _dot_nt(ql, cn) + _dot_nt(qp, kn)
        rows = s.shape[0]
        t_q = lax.rem(lax.broadcasted_iota(jnp.int32, (rows, t_new), 0), t_new)
        t_k = lax.broadcasted_iota(jnp.int32, (rows, t_new), 1)
        update(jnp.where(t_k <= t_q, s, NEG), cn)
        o_ref[0] = (acc_sc[...] / l_sc[...]).astype(BF16)


def _ouv_body(o_ref, w_ref, out_ref):
    out_ref[...] = _dot(o_ref[...], w_ref[0]).astype(BF16)


def _rope_tables(pos):
    half = ROPE_DIM // 2
    inv = 1.0 / (ROPE_THETA ** (jnp.arange(half, dtype=F32) * (2.0 / ROPE_DIM)))
    ang = pos.astype(F32)[:, None] * inv
    c, s = jnp.cos(ang), jnp.sin(ang)
    n = pos.shape[0]
    one, zero = jnp.ones((n, NOPE_DIM), F32), jnp.zeros((n, NOPE_DIM), F32)
    pad = jnp.zeros((n, HEAD_PAD - QK_DIM), F32)
    z16 = jnp.zeros((n, half), F32)
    ct = jnp.concatenate([one, c, c, pad], 1)
    s1 = jnp.concatenate([zero, z16, s, pad], 1)
    s2 = jnp.concatenate([zero, -s, z16, pad], 1)
    return ct, s1, s2


def _mla_weights(m, w_dq, q_norm, w_uq, w_dkv, kv_norm, w_uk, w_uv, w_o):
    half = ROPE_DIM // 2
    wq = jnp.pad(w_uq[m], ((0, 0), (0, 0), (0, HEAD_PAD - QK_DIM))).reshape(Q_RANK, MLA_HEADS * HEAD_PAD)
    wk = jnp.pad(w_uk[m], ((0, 0), (0, 0), (0, HEAD_PAD - NOPE_DIM))).reshape(KV_RANK, MLA_HEADS * HEAD_PAD)
    wv = w_uv[m].reshape(KV_RANK, MLA_HEADS * V_DIM)
    wukt = jnp.pad(jnp.transpose(w_uk[m], (1, 2, 0)), ((0, 0), (0, HEAD_PAD - NOPE_DIM), (0, 0)))
    wuv_h = jnp.transpose(w_uv[m], (1, 0, 2)).reshape(MLA_HEADS // 2, 2, KV_RANK, V_DIM)
    eye2 = jnp.eye(2, dtype=F32)
    wuv_pair = jnp.einsum('pirv,ij->pirjv', wuv_h, eye2).reshape(MLA_HEADS // 2, 2 * KV_RANK, 2 * V_DIM)
    del half
    return dict(wdq=w_dq[m].astype(BF16), qn=q_norm[m][None], wq=wq.astype(BF16), wdkv=w_dkv[m].astype(BF16),
                kvn=kv_norm[m][None], wk=wk.astype(BF16), wv=wv.astype(BF16), wukt=wukt.astype(BF16),
                wuv_pair=wuv_pair.astype(BF16), wo=w_o[m].reshape(MLA_HEADS * V_DIM, D_MODEL).astype(BF16))


def _mla_layer(x, lay, tabs, w, m, lat_pool, kr_pool, page_table):
    n_seq, seq, n_dec, t_dec = lay["B"], lay["S"], lay["Bd"], lay["Td"]
    q, k, v, c, kpe = _rows_call(
        _mla_proj_body, "mla_proj", [x, *tabs],
        [w["wdq"], w["qn"], w["wq"], w["wdkv"], w["kvn"], w["wk"], w["wv"]],
        [(MLA_HEADS * HEAD_PAD, BF16), (MLA_HEADS * HEAD_PAD, BF16), (MLA_HEADS * V_DIM, BF16),
         (KV_RANK, F32), (ROPE_DIM, F32)])
    soff, moff = lay["soff"], lay["moff"]
    o_real = _flash(q, k, v, n_seq, seq, 0, min(ATTN_TILE, seq), lead_off=moff)
    o_meta = _flash(q, k, v, n_seq, N_META, moff, N_META)

    n_rows = n_dec * t_dec
    hq = MLA_HEADS * t_dec
    q_lat = pl.pallas_call(
        _qlat_body, grid=(MLA_HEADS,),
        in_specs=[pl.BlockSpec((n_rows, HEAD_PAD), lambda h: (soff // n_rows, h)),
                  pl.BlockSpec((1, HEAD_PAD, KV_RANK), lambda h: (h, 0, 0))],
        out_specs=pl.BlockSpec((1, n_rows, KV_RANK), lambda h: (h, 0, 0)),
        out_shape=jax.ShapeDtypeStruct((MLA_HEADS, n_rows, KV_RANK), BF16),
        compiler_params=_params("parallel"), name="mla_qlat")(q, w["wukt"])
    q_lat = q_lat.reshape(MLA_HEADS, n_dec, t_dec, KV_RANK).transpose(1, 0, 2, 3).reshape(n_dec, hq, KV_RANK)
    q_s = q[soff:soff + n_rows].reshape(n_dec, t_dec, MLA_HEADS, HEAD_PAD)[..., NOPE_DIM:QK_DIM]
    q_pe = q_s.transpose(0, 2, 1, 3).reshape(n_dec, hq, ROPE_DIM)
    c_new = c[soff:soff + n_rows].reshape(n_dec, t_dec, KV_RANK)
    k_new = kpe[soff:soff + n_rows].reshape(n_dec, t_dec, ROPE_DIM)
    n_pages, page = page_table.shape[1], lat_pool.shape[2]
    n_pg = math.gcd(DECODE_PAGES, n_pages)
    kr_pool_t = jnp.swapaxes(kr_pool, 2, 3)
    lat_spec = lambda i: pl.BlockSpec((1, 1, page, KV_RANK), lambda bb, j, pt: (m, pt[bb, j * n_pg + i], 0, 0))
    kr_spec = lambda i: pl.BlockSpec((1, 1, ROPE_DIM, page), lambda bb, j, pt: (m, pt[bb, j * n_pg + i], 0, 0))
    o_lat = pl.pallas_call(
        functools.partial(_decode_body, t_new=t_dec, n_pg=n_pg),
        grid_spec=pltpu.PrefetchScalarGridSpec(
            num_scalar_prefetch=1, grid=(n_dec, n_pages // n_pg),
            in_specs=[pl.BlockSpec((1, hq, KV_RANK), lambda bb, j, pt: (bb, 0, 0)),
                      pl.BlockSpec((1, hq, ROPE_DIM), lambda bb, j, pt: (bb, 0, 0))]
            + [lat_spec(i) for i in range(n_pg)] + [kr_spec(i) for i in range(n_pg)]
            + [pl.BlockSpec((1, t_dec, KV_RANK), lambda bb, j, pt: (bb, 0, 0)),
               pl.BlockSpec((1, t_dec, ROPE_DIM), lambda bb, j, pt: (bb, 0, 0))],
            out_specs=pl.BlockSpec((1, hq, KV_RANK), lambda bb, j, pt: (bb, 0, 0)),
            scratch_shapes=[pltpu.VMEM((hq, 1), F32), pltpu.VMEM((hq, 1), F32), pltpu.VMEM((hq, KV_RANK), F32)]),
        out_shape=jax.ShapeDtypeStruct((n_dec, hq, KV_RANK), BF16),
        compiler_params=_params("parallel", "arbitrary"), name="mla_decode")(
            page_table, q_lat, q_pe, *([lat_pool] * n_pg), *([kr_pool_t] * n_pg), c_new, k_new)
    o_lat = o_lat.reshape(n_dec, MLA_HEADS, t_dec, KV_RANK).transpose(0, 2, 1, 3).reshape(n_rows, MLA_HEADS * KV_RANK)
    o_samp = pl.pallas_call(
        _ouv_body, grid=(MLA_HEADS // 2,),
        in_specs=[pl.BlockSpec((n_rows, 2 * KV_RANK), lambda h: (0, h)),
                  pl.BlockSpec((1, 2 * KV_RANK, 2 * V_DIM), lambda h: (h, 0, 0))],
        out_specs=pl.BlockSpec((n_rows, 2 * V_DIM), lambda h: (0, h)),
        out_shape=jax.ShapeDtypeStruct((n_rows, MLA_HEADS * V_DIM), BF16),
        compiler_params=_params("parallel"), name="mla_ouv")(o_lat, w["wuv_pair"])

    return _assemble(lay, o_real, o_samp, o_meta), c, kpe


def _split_dot(a, sel):
    hi = a.astype(BF16)
    lo = (a - hi.astype(F32)).astype(BF16)
    return _dot(hi, sel) + _dot(lo, sel)


def _head_sum(x, gs, gb):
    return _split_dot(_split_dot(x, gs), gb)


def _softplus(z):
    return jnp.maximum(z, 0.0) + jnp.log(1.0 + jnp.exp(-jnp.abs(z)))


def _rw_proj_body(x_ref, xp_ref, mu_ref, wr_ref, wk_ref, wv_ref, w0_ref, w1_ref, w2_ref, a0_ref, a1_ref, a2_ref,
                  g1_ref, g2_ref, kk_ref, ka_ref, rk_ref, gs_ref, gb_ref,
                  r_o, d_o, k_o, v_o, kk_o, a_o, g_o, bonus_o):
    x = x_ref[...]
    xx = xp_ref[...] - x
    mix = lambda j: (x + xx * mu_ref[j:j + 1, :]).astype(BF16)
    w_lo = _dot(mix(1), w1_ref[...])
    a_lo = _dot(mix(4), a1_ref[...])
    g_lo = _dot(mix(5), g1_ref[...])
    r = _dot(mix(0), wr_ref[...])
    k = _dot(mix(2), wk_ref[...])
    v = _dot(mix(3), wv_ref[...])
    wl = w0_ref[...] + _dot(jnp.tanh(w_lo).astype(BF16), w2_ref[...])
    w = -_softplus(-wl) - 0.5
    a = _sigmoid(a0_ref[...] + _dot(a_lo.astype(BF16), a2_ref[...]))
    g = _dot(_sigmoid(g_lo).astype(BF16), g2_ref[...])
    gs, gb = gs_ref[...], gb_ref[...]
    kk = k * kk_ref[...]
    kk = kk * lax.rsqrt(jnp.maximum(_head_sum(kk * kk, gs, gb), 1e-24))
    k = k * (1.0 + (a - 1.0) * ka_ref[...])
    r_o[...] = r
    d_o[...] = -jnp.exp(w)
    k_o[...] = k
    v_o[...] = v
    kk_o[...] = kk
    a_o[...] = a
    g_o[...] = g
    bonus_o[...] = _head_sum(r * k * rk_ref[...], gs, gb) * v


def _rw_chunk_body(*refs, chunk, n_fac, group):
    s0_ref, y_ref, st_ref, s_sc = refs[6 * group:]
    load = lambda i: jnp.concatenate([refs[i * group + g][...] for g in range(group)], 0)
    ci = pl.program_id(1)

    @pl.when(ci == 0)
    def _init():
        s_sc[...] = s0_ref[...]

    row = lax.broadcasted_iota(jnp.int32, (chunk, chunk), 0)
    col = lax.broadcasted_iota(jnp.int32, (chunk, chunk), 1)
    strict, incl = col < row, col <= row
    rows = group * chunk
    if group == 1:
        same, below = None, incl
    else:
        rr = lax.broadcasted_iota(jnp.int32, (rows, rows), 0)
        cc = lax.broadcasted_iota(jnp.int32, (rows, rows), 1)
        same = (rr // chunk) == (cc // chunk)
        below = jnp.logical_and(same, cc <= rr)
    d_all, k_all, kk_all = load(1), load(2), load(4)
    cum = _dot(below.astype(F32), d_all, HI)
    if group == 1:
        cum_end = cum[chunk - 1:chunk, :]
    else:
        cum_end = _dot(same.astype(F32), d_all, HI)
    be_all = kk_all * load(5)
    w_inv = jnp.exp(-cum)
    w_rem = jnp.exp(cum_end - cum)
    al_t_all = (-kk_all * jnp.exp(cum - d_all)).astype(BF16)
    r_t_all = load(0) * jnp.exp(cum)
    be_h_all = (be_all * w_inv).astype(BF16)
    k_h_all = (k_all * w_inv).astype(BF16)
    be_e_all = (be_all * w_rem).astype(BF16)
    k_e_all = (k_all * w_rem).astype(BF16)
    w_end = jnp.exp(cum_end)
    v_all = load(3).astype(BF16)
    zero = jnp.zeros((chunk, RW_HEAD), BF16)
    chains = [(g, h) for g in range(group) for h in range(RW_HEADS)]
    ids = range(len(chains))
    cut = lambda a, g, h: a[g * chunk:(g + 1) * chunk, h * RW_HEAD:(h + 1) * RW_HEAD]
    gram = [_dot_nt(jnp.concatenate([cut(al_t_all, *c), cut(r_t_all, *c).astype(BF16)], 0),
                    jnp.concatenate([cut(be_h_all, *c), cut(k_h_all, *c)], 0)) for c in chains]
    l_ab = [jnp.where(strict, g[:chunk, :chunk], 0.0).astype(BF16) for g in gram]
    l_ak = [jnp.where(strict, g[:chunk, chunk:], 0.0).astype(BF16) for g in gram]
    m_r = [jnp.concatenate([jnp.where(incl, g[chunk:, :chunk], 0.0),
                            jnp.where(incl, g[chunk:, chunk:], 0.0)], 1).astype(BF16) for g in gram]
    u = [jnp.concatenate([cut(al_t_all, *c).astype(F32), _dot(l_ak[i], cut(v_all, *c))], 1) for i, c in zip(ids, chains)]
    lp = l_ab
    for f in range(n_fac):
        u = [u[i] + _dot(lp[i], u[i].astype(BF16)) for i in ids]
        if f + 1 < n_fac:
            lp = [_dot(lp[i], lp[i]).astype(BF16) for i in ids]
    z = [jnp.concatenate([u[i].astype(BF16), jnp.concatenate([zero, cut(v_all, *c)], 1)], 0) for i, c in zip(ids, chains)]
    g = [_dot(m_r[i], z[i]) for i in ids]
    ad = [_dot_tn(z[i], jnp.concatenate([cut(be_e_all, *c), cut(k_e_all, *c)], 0)) for i, c in zip(ids, chains)]
    s_old = [s_sc[c[0], c[1]] for c in chains]
    s_bf = [s.astype(BF16) for s in s_old]
    y = [_dot_nt((cut(r_t_all, *c) + g[i][:, :RW_HEAD]).astype(BF16), s_bf[i]) + g[i][:, RW_HEAD:]
         for i, c in zip(ids, chains)]
    for gi in range(group):
        y_ref[gi] = jnp.concatenate(y[gi * RW_HEADS:(gi + 1) * RW_HEADS], 1)
    for i, (gi, h) in zip(ids, chains):
        w_h = w_end[gi * chunk:gi * chunk + 1, h * RW_HEAD:(h + 1) * RW_HEAD]
        s_sc[gi, h] = s_old[i] * w_h + _dot(s_bf[i], ad[i][:RW_HEAD].astype(BF16)) + ad[i][RW_HEAD:]

    @pl.when(ci == pl.num_programs(1) - 1)
    def _fin():
        st_ref[...] = s_sc[...]


def _rw_scan(seqs, s0, n_seq, seq_len, row_off, chunk, group=1):
    nc = seq_len // chunk
    assert n_seq % group == 0
    off = row_off // chunk
    n_fac = max(1, math.ceil(math.log2(chunk)))
    row_spec = lambda g: pl.BlockSpec((chunk, D_MODEL), lambda b, c: (off + (b * group + g) * nc + c, 0))
    st_spec = pl.BlockSpec((group, RW_HEADS, RW_HEAD, RW_HEAD), lambda b, c: (b, 0, 0, 0))
    y, st = pl.pallas_call(
        functools.partial(_rw_chunk_body, chunk=chunk, n_fac=n_fac, group=group),
        grid=(n_seq // group, nc),
        in_specs=[row_spec(g) for _ in range(6) for g in range(group)] + [st_spec],
        out_specs=[pl.BlockSpec((group, chunk, D_MODEL), lambda b, c: (b, c, 0)), st_spec],
        out_shape=[jax.ShapeDtypeStruct((n_seq, seq_len, D_MODEL), F32),
                   jax.ShapeDtypeStruct((n_seq, RW_HEADS, RW_HEAD, RW_HEAD), F32)],
        scratch_shapes=[pltpu.VMEM((group, RW_HEADS, RW_HEAD, RW_HEAD), F32)],
        compiler_params=_params("parallel", "arbitrary"), name=f"rwkv_chunk{chunk}")(
            *[a for a in seqs for _ in range(group)], s0)
    return y.reshape(n_seq * seq_len, D_MODEL), st


def _rw_mix(x, y_ref, g_ref, bonus_ref, lg_ref, lb_ref, gs_ref, gb_ref, wo_ref, ng_ref, nb_ref):
    y = y_ref[...]
    gs, gb = gs_ref[...], gb_ref[...]
    mean = _head_sum(y, gs, gb) * (1.0 / RW_HEAD)
    yc = y - mean
    var = _head_sum(yc * yc, gs, gb) * (1.0 / RW_HEAD)
    yn = yc * lax.rsqrt(var + GN_EPS) * lg_ref[...] + lb_ref[...]
    z = ((yn + bonus_ref[...]) * g_ref[...]).astype(BF16)
    return _layer_norm(ALPHA * x + _dot(z, wo_ref[...]), ng_ref[...], nb_ref[...])


def _rwkv_layer(x, lay, m, p, shift_s, wkv_s, g, b):
    n_seq, seq, n_dec, t_dec = lay["B"], lay["S"], lay["Bd"], lay["Td"]
    soff, moff = lay["soff"], lay["moff"]
    n_rows = n_dec * t_dec
    xm = x[moff:moff + n_seq * N_META].reshape(n_seq, N_META, D_MODEL)
    xs = x[soff:soff + n_rows].reshape(n_dec, t_dec, D_MODEL)
    prev_s = jnp.concatenate([shift_s[m][:, None], xs[:, :-1]], 1).reshape(n_rows, D_MODEL)
    xprev = jnp.concatenate([jnp.zeros((1, D_MODEL), F32), x[:-1]], 0)
    xprev = lax.dynamic_update_slice(xprev, prev_s, (soff, 0))
    for bi in range(n_seq):
        xprev = lax.dynamic_update_slice(xprev, xm[bi, -1:], (bi * seq, 0))
        xprev = lax.dynamic_update_slice(xprev, jnp.zeros((1, D_MODEL), F32), (moff + bi * N_META, 0))
    x_last = jnp.stack([x[(bi + 1) * seq - 1] for bi in range(n_seq)])

    row = lambda a: a[m][None]
    gs = jnp.pad(jnp.repeat(jnp.eye(RW_HEADS, dtype=BF16), RW_HEAD, 0), ((0, 0), (0, LANES - RW_HEADS)))
    gb = gs.T
    bf = lambda a: a[m].astype(BF16)
    r, d, k, v, kk, a, gate, bonus = _rows_call(
        _rw_proj_body, "rwkv_proj", [x, xprev],
        [p['rw_mu'][m], bf(p['rw_wr']), bf(p['rw_wk']), bf(p['rw_wv']), row(p['rw_w0']), bf(p['rw_w1']),
         bf(p['rw_w2']), row(p['rw_a0']), bf(p['rw_a1']), bf(p['rw_a2']), bf(p['rw_g1']), bf(p['rw_g2']),
         row(p['rw_k_k']), row(p['rw_k_a']), p['rw_r_k'][m].reshape(1, D_MODEL), gs, gb],
        [(D_MODEL, F32)] * 8)
    seqs = (r, d, k, v, kk, a)
    zero_state = jnp.zeros((n_seq, RW_HEADS, RW_HEAD, RW_HEAD), F32)
    y_meta, st = _rw_scan(seqs, zero_state, n_seq, N_META, moff, N_META, group=n_seq)
    y_real, st_p = _rw_scan(seqs, st, n_seq, seq, 0, min(RW_CHUNK, seq), group=n_seq)
    t_pad = 8
    pad_rows = lambda z: jnp.pad(z[soff:soff + n_rows].reshape(n_dec, t_dec, D_MODEL),
                                 ((0, 0), (0, t_pad - t_dec), (0, 0))).reshape(n_dec * t_pad, D_MODEL)
    y_s, st_s = _rw_scan(tuple(pad_rows(z) for z in seqs), wkv_s[m], n_dec, t_pad, 0, t_pad,
                         group=math.gcd(RW_GROUP, n_dec))
    y_s = y_s.reshape(n_dec, t_pad, D_MODEL)[:, :t_dec].reshape(n_rows, D_MODEL)
    y_all = _assemble(lay, y_real, y_s, y_meta)
    mix = (_rw_mix, [y_all, gate, bonus], [row(p['rw_lnx_g']), row(p['rw_lnx_b']), gs, gb, bf(p['rw_wo']), g, b])
    return mix, st_p, x_last, st_s, xs[:, -1]


def _s5_disc_body(lre_ref, lim_ref, dt_ref, bre_ref, bim_ref, lre_s, lim_s, dt_s, bbr_o, bbi_o, ar_o, ai_o):
    def disc(lre, lim, dt):
        mag = jnp.exp(lre * dt)
        ab_re, ab_im = mag * jnp.cos(lim * dt), mag * jnp.sin(lim * dt)
        den = lre * lre + lim * lim
        nr = ab_re - 1.0
        return ab_re, ab_im, (nr * lre + ab_im * lim) / den, (ab_im * lre - nr * lim) / den

    _, _, co_re, co_im = disc(lre_ref[...], lim_ref[...], dt_ref[...])
    b_re, b_im = bre_ref[...], bim_ref[...]
    bbr_o[...] = co_re * b_re - co_im * b_im
    bbi_o[...] = co_re * b_im + co_im * b_re
    ab_re, ab_im, _, _ = disc(lre_s[...], lim_s[...], dt_s[...])
    ar_o[...] = ab_re
    ai_o[...] = ab_im


def _s5_input(u, bre_ref, bim_ref):
    ub = u.astype(BF16)
    w = S5_WIDTH // S5_BLOCKS
    cw = D_MODEL // S5_BLOCKS
    re = [_dot(ub[:, j * cw:(j + 1) * cw], bre_ref[j]) for j in range(S5_BLOCKS)]
    im = [_dot(ub[:, j * cw:(j + 1) * cw], bim_ref[j]) for j in range(S5_BLOCKS)]
    del w
    return re, im


def _s5_output(u, hre, him, cre_ref, cim_ref, dv_ref, wv_ref, wg_ref, g_ref, b_ref):
    w = S5_WIDTH // S5_BLOCKS
    hb_re, hb_im = hre.astype(BF16), him.astype(BF16)
    ys = [_dot(hb_re[:, j * w:(j + 1) * w], cre_ref[j]) - _dot(hb_im[:, j * w:(j + 1) * w], cim_ref[j])
          for j in range(S5_BLOCKS)]
    y = jnp.concatenate(ys, 1) + dv_ref[...] * u
    z = jax.nn.gelu(y).astype(BF16)
    out = _dot(z, wv_ref[...]) * _sigmoid(_dot(z, wg_ref[...]))
    return _layer_norm(ALPHA * u + out, g_ref[...], b_ref[...])


def _s5_seq_body(x_ref, h0r_ref, h0i_ref, ar_ref, ai_ref, bre_ref, bim_ref, cre_ref, cim_ref, dv_ref, wv_ref, wg_ref,
                 g_ref, b_ref, o_ref, htr_ref, hti_ref, hre_sc, him_sc, sr_sc, si_sc, *, tt):
    ti = pl.program_id(1)

    @pl.when(ti == 0)
    def _init():
        sr_sc[...] = h0r_ref[0]
        si_sc[...] = h0i_ref[0]

    u = x_ref[...]
    re, im = _s5_input(u, bre_ref, bim_ref)
    w = S5_WIDTH // S5_BLOCKS
    for j in range(S5_BLOCKS):
        hre_sc[:, j * w:(j + 1) * w] = re[j]
        him_sc[:, j * w:(j + 1) * w] = im[j]
    ar, ai = ar_ref[...], ai_ref[...]

    def step(t, carry):
        hr, hi = carry
        nr = ar * hr - ai * hi + hre_sc[pl.ds(t, 1), :]
        ni = ar * hi + ai * hr + him_sc[pl.ds(t, 1), :]
        hre_sc[pl.ds(t, 1), :] = nr
        him_sc[pl.ds(t, 1), :] = ni
        return nr, ni

    hr, hi = lax.fori_loop(0, tt, step, (sr_sc[...], si_sc[...]), unroll=S5_UNROLL)
    sr_sc[...] = hr
    si_sc[...] = hi
    o_ref[...] = _s5_output(u, hre_sc[...], him_sc[...], cre_ref, cim_ref, dv_ref, wv_ref, wg_ref, g_ref, b_ref)

    @pl.when(ti == pl.num_programs(1) - 1)
    def _fin():
        htr_ref[0] = hr
        hti_ref[0] = hi


def _s5_step_body(x_ref, h0r_ref, h0i_ref, ar_ref, ai_ref, bre_ref, bim_ref, cre_ref, cim_ref, dv_ref, wv_ref, wg_ref,
                  g_ref, b_ref, o_ref, htr_ref, hti_ref, sr_sc, si_sc):
    t = pl.program_id(0)

    @pl.when(t == 0)
    def _init():
        sr_sc[...] = h0r_ref[...]
        si_sc[...] = h0i_ref[...]

    u = x_ref[0]
    re, im = _s5_input(u, bre_ref, bim_ref)
    bur, bui = jnp.concatenate(re, 1), jnp.concatenate(im, 1)
    ar, ai = ar_ref[...], ai_ref[...]
    hr, hi = sr_sc[...], si_sc[...]
    nr = ar * hr - ai * hi + bur
    ni = ar * hi + ai * hr + bui
    sr_sc[...] = nr
    si_sc[...] = ni
    o_ref[0] = _s5_output(u, nr, ni, cre_ref, cim_ref, dv_ref, wv_ref, wg_ref, g_ref, b_ref)

    @pl.when(t == pl.num_programs(0) - 1)
    def _fin():
        htr_ref[...] = nr
        hti_ref[...] = ni


def _s5_weights(m, p):
    rep = lambda a: jnp.repeat(a, S5_GROUP, axis=-1)
    lre, lim = p['s5_lam_re'][m], p['s5_lam_im'][m]
    dt = jnp.broadcast_to(jnp.exp(p['s5_log_dt'][m])[:, None], (S5_GROUPS, S5_STATE))
    flat = lambda a: a[m].reshape(S5_GROUPS, S5_STATE * S5_GROUP)
    n = S5_STATE * S5_GROUP
    bb_re, bb_im, ab_re, ab_im = pl.pallas_call(
        _s5_disc_body,
        out_shape=[jax.ShapeDtypeStruct((S5_GROUPS, n), F32)] * 2 + [jax.ShapeDtypeStruct((S5_GROUPS, S5_STATE), F32)] * 2,
        name="s5_disc")(rep(lre), rep(lim), rep(dt), flat(p['s5_b_re']), flat(p['s5_b_im']), lre, lim, dt)
    gpb = S5_GROUPS // S5_BLOCKS
    eye = jnp.eye(gpb, dtype=F32)

    def in_blocks(bb):
        t = bb.reshape(S5_BLOCKS, gpb, S5_STATE, S5_GROUP).transpose(0, 1, 3, 2)
        return jnp.einsum('jgsp,gh->jgshp', t, eye).reshape(S5_BLOCKS, gpb * S5_GROUP, gpb * S5_STATE).astype(BF16)

    def out_blocks(c):
        t = c.reshape(S5_BLOCKS, gpb, S5_GROUP, S5_STATE).transpose(0, 1, 3, 2)
        return jnp.einsum('jgps,gh->jgphs', t, eye).reshape(S5_BLOCKS, gpb * S5_STATE, gpb * S5_GROUP).astype(BF16)

    return dict(ar=ab_re.reshape(1, S5_WIDTH), ai=ab_im.reshape(1, S5_WIDTH),
                bre=in_blocks(bb_re), bim=in_blocks(bb_im),
                cre=out_blocks(p['s5_c_re'][m]), cim=out_blocks(p['s5_c_im'][m]),
                dv=p['s5_d'][m][None], wv=p['s5_wv'][m].astype(BF16), wg=p['s5_wg'][m].astype(BF16))


def _s5_seq(x, h0r, h0i, w, g, b, n_seq, seq_len, row_off, tt):
    nt = seq_len // tt
    off = row_off // tt
    consts = [w["ar"], w["ai"], w["bre"], w["bim"], w["cre"], w["cim"], w["dv"], w["wv"], w["wg"], g, b]
    st_spec = pl.BlockSpec((1, 1, S5_WIDTH), lambda bb, t: (bb, 0, 0))
    return pl.pallas_call(
        functools.partial(_s5_seq_body, tt=tt), grid=(n_seq, nt),
        in_specs=[pl.BlockSpec((tt, D_MODEL), lambda bb, t: (off + bb * nt + t, 0)), st_spec, st_spec]
        + [_const_spec(a, 2) for a in consts],
        out_specs=[pl.BlockSpec((tt, D_MODEL), lambda bb, t: (bb * nt + t, 0)), st_spec, st_spec],
        out_shape=[jax.ShapeDtypeStruct((n_seq * seq_len, D_MODEL), F32)]
        + [jax.ShapeDtypeStruct((n_seq, 1, S5_WIDTH), F32)] * 2,
        scratch_shapes=[pltpu.VMEM((tt, S5_WIDTH), F32), pltpu.VMEM((tt, S5_WIDTH), F32),
                        pltpu.VMEM((1, S5_WIDTH), F32), pltpu.VMEM((1, S5_WIDTH), F32)],
        compiler_params=_params("parallel", "arbitrary"), name=f"s5_seq{tt}")(x, h0r, h0i, *consts)


def _s5_layer(x, lay, m, p, re_s, im_s, g, b):
    n_seq, seq, n_dec, t_dec = lay["B"], lay["S"], lay["Bd"], lay["Td"]
    soff, moff = lay["soff"], lay["moff"]
    n_rows = n_dec * t_dec
    w = _s5_weights(m, p)
    zero = jnp.zeros((n_seq, 1, S5_WIDTH), F32)
    o_meta, hr, hi = _s5_seq(x, zero, zero, w, g, b, n_seq, N_META, moff, N_META)
    o_real, hr, hi = _s5_seq(x, hr, hi, w, g, b, n_seq, seq, 0, min(S5_TILE, seq))
    xs = x[soff:soff + n_rows].reshape(n_dec, t_dec, D_MODEL).transpose(1, 0, 2)
    consts = [w["ar"], w["ai"], w["bre"], w["bim"], w["cre"], w["cim"], w["dv"], w["wv"], w["wg"], g, b]
    st_spec = pl.BlockSpec((n_dec, S5_WIDTH), lambda t: (0, 0))
    o_s, sr, si = pl.pallas_call(
        _s5_step_body, grid=(t_dec,),
        in_specs=[pl.BlockSpec((1, n_dec, D_MODEL), lambda t: (t, 0, 0)), st_spec, st_spec]
        + [_const_spec(a, 1) for a in consts],
        out_specs=[pl.BlockSpec((1, n_dec, D_MODEL), lambda t: (t, 0, 0)), st_spec, st_spec],
        out_shape=[jax.ShapeDtypeStruct((t_dec, n_dec, D_MODEL), F32)]
        + [jax.ShapeDtypeStruct((n_dec, S5_WIDTH), F32)] * 2,
        scratch_shapes=[pltpu.VMEM((n_dec, S5_WIDTH), F32), pltpu.VMEM((n_dec, S5_WIDTH), F32)],
        compiler_params=_params("arbitrary"), name="s5_step")(
            xs, re_s[m].reshape(n_dec, S5_WIDTH), im_s[m].reshape(n_dec, S5_WIDTH), *consts)
    o_s = o_s.transpose(1, 0, 2).reshape(n_rows, D_MODEL)
    x_new = _assemble(lay, o_real, o_s, o_meta)
    shp = (S5_GROUPS, S5_STATE)
    return x_new, hr.reshape(n_seq, *shp), hi.reshape(n_seq, *shp), sr.reshape(n_dec, *shp), si.reshape(n_dec, *shp)


def _round_up(n, m):
    return (n + m - 1) // m * m


def kernel(x_prompt, x_sample, cache_mla_latent, cache_mla_krope, state_rwkv_wkv, state_rwkv_shift, state_s5_re, state_s5_im, page_table, meta_tokens, ln_g, ln_b, ffn_w1, ffn_w3, ffn_w2, mla_w_dq, mla_q_norm, mla_w_uq, mla_w_dkv, mla_kv_norm, mla_w_uk, mla_w_uv, mla_w_o, rw_mu, rw_wr, rw_wk, rw_wv, rw_w0, rw_w1, rw_w2, rw_a0, rw_a1, rw_a2, rw_g1, rw_g2, rw_k_k, rw_k_a, rw_r_k, rw_lnx_g, rw_lnx_b, rw_wo, s5_lam_re, s5_lam_im, s5_log_dt, s5_b_re, s5_b_im, s5_c_re, s5_c_im, s5_d, s5_wv, s5_wg):
    p = dict(rw_mu=rw_mu, rw_wr=rw_wr, rw_wk=rw_wk, rw_wv=rw_wv, rw_w0=rw_w0, rw_w1=rw_w1, rw_w2=rw_w2,
             rw_a0=rw_a0, rw_a1=rw_a1, rw_a2=rw_a2, rw_g1=rw_g1, rw_g2=rw_g2, rw_k_k=rw_k_k, rw_k_a=rw_k_a,
             rw_r_k=rw_r_k, rw_lnx_g=rw_lnx_g, rw_lnx_b=rw_lnx_b, rw_wo=rw_wo,
             s5_lam_re=s5_lam_re, s5_lam_im=s5_lam_im, s5_log_dt=s5_log_dt, s5_b_re=s5_b_re, s5_b_im=s5_b_im,
             s5_c_re=s5_c_re, s5_c_im=s5_c_im, s5_d=s5_d, s5_wv=s5_wv, s5_wg=s5_wg)
    n_seq, seq, _ = x_prompt.shape
    n_dec, t_dec, _ = x_sample.shape
    past = page_table.shape[1] * cache_mla_latent.shape[2]
    n_rows = n_dec * t_dec
    soff = _round_up(n_seq * seq, ROW_ALIGN)
    moff = _round_up(soff + n_rows, ROW_ALIGN)
    n_tot = _round_up(moff + n_seq * N_META, ROW_ALIGN)
    lay = dict(B=n_seq, S=seq, Bd=n_dec, Td=t_dec, soff=soff, moff=moff, n_tot=n_tot)
    assemble = functools.partial(_assemble, lay)

    meta = jnp.broadcast_to(meta_tokens[None], (n_seq, N_META, D_MODEL)).reshape(n_seq * N_META, D_MODEL)
    x = assemble(x_prompt.reshape(n_seq * seq, D_MODEL), x_sample.reshape(n_rows, D_MODEL), meta)
    pos = assemble(jnp.tile(N_META + jnp.arange(seq, dtype=jnp.int32), n_seq)[:, None],
                   jnp.tile(past + jnp.arange(t_dec, dtype=jnp.int32), n_dec)[:, None],
                   jnp.tile(jnp.arange(N_META, dtype=jnp.int32), n_seq)[:, None])[:, 0]
    tabs = _rope_tables(pos)

    stacked = lambda a: a.reshape((-1,) + a.shape[2:])
    w1_all, w3_all, w2_all = (stacked(a).astype(BF16) for a in (ffn_w1, ffn_w3, ffn_w2))
    g_all, b_all = (a.reshape(-1, 1, D_MODEL) for a in (ln_g, ln_b))

    lat_p, kr_p, lat_s, kr_s = [], [], [], []
    wkv_p = sh_p = wkv_s = sh_s = re_p = im_p = re_s = im_s = None
    stacks = (w1_all, w3_all, w2_all, g_all, b_all)
    ffn_a = lambda i: (2 * i, 3 * i)
    ffn_b = lambda i: (2 * i + 1, 3 * i + 2)
    x = _chain(x, stacks, [ffn_a(0)])
    for i in range(DEPTH):
        kind, m = i % 3, i // 3
        norm = lambda j: (ln_g[i, j][None], ln_b[i, j][None])
        tail = [ffn_b(i)] + ([ffn_a(i + 1)] if i + 1 < DEPTH else [])
        mix = None
        if kind == 0:
            w = _mla_weights(m, mla_w_dq, mla_q_norm, mla_w_uq, mla_w_dkv, mla_kv_norm, mla_w_uk, mla_w_uv, mla_w_o)
            o_all, c, kpe = _mla_layer(x, lay, tabs, w, m, cache_mla_latent, cache_mla_krope, page_table)
            mix = (_proj_mix, [o_all], [w["wo"], (g_all, 3 * i + 1), (b_all, 3 * i + 1)])
            split = lambda a: (jnp.concatenate(
                [a[moff:moff + n_seq * N_META].reshape(n_seq, N_META, -1), a[:n_seq * seq].reshape(n_seq, seq, -1)], 1),
                a[soff:soff + n_rows].reshape(n_dec, t_dec, -1))
            cp, cs = split(c)
            kp, ks = split(kpe)
            lat_p.append(cp), lat_s.append(cs), kr_p.append(kp), kr_s.append(ks)
        elif kind == 1:
            mix, wkv_p, sh_p, wkv_s, sh_s = _rwkv_layer(x, lay, m, p, state_rwkv_shift, state_rwkv_wkv, *norm(1))
        else:
            x, re_p, im_p, re_s, im_s = _s5_layer(x, lay, m, p, state_s5_re, state_s5_im, *norm(1))
        x = _chain(x, stacks, tail, mix)

    y_prompt = x[:n_seq * seq].reshape(n_seq, seq, D_MODEL)
    y_sample = x[soff:soff + n_rows].reshape(n_dec, t_dec, D_MODEL)
    return (y_prompt, y_sample, jnp.stack(lat_p), jnp.stack(kr_p), jnp.stack(lat_s), jnp.stack(kr_s),
            wkv_p[None], sh_p[None], wkv_s[None], sh_s[None], re_p[None], im_p[None], re_s[None], im_s[None])
```

```python
import functools
import math

import jax
import jax.numpy as jnp
from jax import lax
from jax.experimental import pallas as pl
from jax.experimental.pallas import tpu as pltpu

F32 = jnp.float32
BF16 = jnp.bfloat16
HI = lax.Precision.HIGHEST

D_MODEL = 1024
DEPTH = 4
N_META = 16
D_FF = 2816
ALPHA = (2 * DEPTH) ** 0.25
LN_EPS = 1e-5
RMS_EPS = 1e-6
MLA_HEADS = 16
Q_RANK = 768
KV_RANK = 256
NOPE_DIM = 64
ROPE_DIM = 32
QK_DIM = NOPE_DIM + ROPE_DIM
V_DIM = 64
ROPE_THETA = 10000.0
RW_HEAD = 64
RW_HEADS = D_MODEL // RW_HEAD
GN_EPS = 64e-5
S5_GROUP = 16
S5_GROUPS = D_MODEL // S5_GROUP
S5_STATE = 64
S5_WIDTH = S5_GROUPS * S5_STATE
S5_BLOCKS = 8

LANES = 128
HEAD_PAD = 128
ROW_TILE = 256
ROW_ALIGN = 256
ATTN_TILE = 1024
ATTN_SUB = 256
ATTN_AHEAD = 2
DECODE_PAGES = 64
RW_CHUNK = 64
RW_GROUP = 4
S5_TILE = 256
S5_UNROLL = 8
VMEM_LIMIT = 56 * 1024 * 1024
NEG = -1e30


def _dot(a, b, prec=None):
    return jnp.dot(a, b, preferred_element_type=F32, precision=prec)


def _dot_nt(a, b, prec=None):
    return lax.dot_general(a, b, (((1,), (1,)), ((), ())), preferred_element_type=F32, precision=prec)


def _dot_tn(a, b, prec=None):
    return lax.dot_general(a, b, (((0,), (0,)), ((), ())), preferred_element_type=F32, precision=prec)


def _layer_norm(y, g, b):
    mu = jnp.mean(y, -1, keepdims=True)
    yc = y - mu
    var = jnp.mean(yc * yc, -1, keepdims=True)
    return yc * lax.rsqrt(var + LN_EPS) * g + b


def _rms_norm(y, g):
    return y * lax.rsqrt(jnp.mean(y * y, -1, keepdims=True) + RMS_EPS) * g


def _sigmoid(x):
    return 1.0 / (1.0 + jnp.exp(-x))


def _params(*sem):
    return pltpu.CompilerParams(dimension_semantics=sem, vmem_limit_bytes=VMEM_LIMIT)


def _const_spec(a, n_grid):
    zeros = (0,) * a.ndim
    if n_grid == 1:
        imap = lambda i: zeros
    elif n_grid == 2:
        imap = lambda i, j: zeros
    else:
        imap = lambda i, j, k, l: zeros
    return pl.BlockSpec(a.shape, imap, pipeline_mode=pl.Buffered(1))


def _pick_spec(a, idx):
    tail = (0,) * (a.ndim - 1)
    return pl.BlockSpec((None,) + a.shape[1:], lambda i: (idx,) + tail, pipeline_mode=pl.Buffered(1))


def _rows_call(body, name, row_ins, const_ins, outs, tm=ROW_TILE):
    n = row_ins[0].shape[0]
    assert n % tm == 0
    in_specs = [pl.BlockSpec((tm, a.shape[1]), lambda i: (i, 0)) for a in row_ins]
    in_specs += [_pick_spec(*a) if isinstance(a, tuple) else _const_spec(a, 1) for a in const_ins]
    consts = [a[0] if isinstance(a, tuple) else a for a in const_ins]
    out_specs = [pl.BlockSpec((tm, w), lambda i: (i, 0)) for w, _ in outs]
    out_shape = [jax.ShapeDtypeStruct((n, w), dt) for w, dt in outs]
    res = pl.pallas_call(body, grid=(n // tm,), in_specs=in_specs, out_specs=out_specs, out_shape=out_shape,
                         compiler_params=_params("parallel"), name=name)(*row_ins, *consts)
    return res


def _assemble(lay, real, samp, meta):
    w = real.shape[1]
    gap = lambda n: jnp.zeros((n, w), real.dtype)
    return jnp.concatenate([real, gap(lay["soff"] - real.shape[0]), samp,
                            gap(lay["moff"] - lay["soff"] - samp.shape[0]), meta,
                            gap(lay["n_tot"] - lay["moff"] - meta.shape[0])], 0)


FF_CHUNK = 1408


def _ffn_stage(x, w1_ref, w3_ref, w2_ref, g_ref, b_ref):
    xb = x.astype(BF16)
    acc = None
    for c in range(D_FF // FF_CHUNK):
        sl = slice(c * FF_CHUNK, (c + 1) * FF_CHUNK)
        h1 = _dot(xb, w1_ref[:, sl])
        h3 = _dot(xb, w3_ref[:, sl])
        part = _dot((h1 * _sigmoid(h1) * h3).astype(BF16), w2_ref[sl, :])
        acc = part if acc is None else acc + part
    return _layer_norm(ALPHA * x + 0.5 * acc, g_ref[...], b_ref[...])


def _proj_mix(x, h_ref, w_ref, g_ref, b_ref):
    return _layer_norm(ALPHA * x + _dot(h_ref[...], w_ref[...]), g_ref[...], b_ref[...])


def _chain_body(*refs, n_ffn, mix_fn, n_mix):
    x = refs[0][...]
    pos = 1
    if mix_fn is not None:
        x = mix_fn(x, *refs[1:1 + n_mix])
        pos += n_mix
    for _ in range(n_ffn):
        x = _ffn_stage(x, *refs[pos:pos + 5])
        pos += 5
    refs[pos][...] = x


def _chain(x, stacks, ffn_ids, mix=None):
    w1_all, w3_all, w2_all, g_all, b_all = stacks
    rows, consts, mix_fn, n_mix = [x], [], None, 0
    if mix is not None:
        mix_fn, mix_rows, mix_consts = mix
        rows += mix_rows
        consts += mix_consts
        n_mix = len(mix_rows) + len(mix_consts)
    for wi, ni in ffn_ids:
        consts += [(w1_all, wi), (w3_all, wi), (w2_all, wi), (g_all, ni), (b_all, ni)]
    body = functools.partial(_chain_body, n_ffn=len(ffn_ids), mix_fn=mix_fn, n_mix=n_mix)
    name = ("mix_" if mix is not None else "") + "ffn" * len(ffn_ids)
    return _rows_call(body, name, rows, consts, [(D_MODEL, F32)])[0]


def _mla_proj_body(x_ref, ct_ref, s1_ref, s2_ref, wdq_ref, qn_ref, wq_ref, wdkv_ref, kvn_ref, wk_ref, wv_ref,
                   q_ref, k_ref, v_ref, c_ref, kpe_ref):
    xb = x_ref[...].astype(BF16)
    ct, s1, s2 = ct_ref[...], s1_ref[...], s2_ref[...]
    dq = _dot(xb, wdq_ref[...])
    ckv = _dot(xb, wdkv_ref[...])
    cq = _rms_norm(dq, qn_ref[...])
    c = _rms_norm(ckv[:, :KV_RANK], kvn_ref[...])
    q = _dot(cq.astype(BF16), wq_ref[...])
    half = ROPE_DIM // 2
    k1, k2 = ckv[:, KV_RANK:KV_RANK + half], ckv[:, KV_RANK + half:]
    cos, sin = ct[:, NOPE_DIM:NOPE_DIM + half], s1[:, NOPE_DIM + half:QK_DIM]
    kpe = jnp.concatenate([k1 * cos - k2 * sin, k1 * sin + k2 * cos], -1)
    c_ref[...] = c
    kpe_ref[...] = kpe
    cb = c.astype(BF16)
    knope = _dot(cb, wk_ref[...])
    v_ref[...] = _dot(cb, wv_ref[...]).astype(BF16)
    tm = kpe.shape[0]
    kpe_slot = jnp.concatenate([jnp.zeros((tm, NOPE_DIM), F32), kpe, jnp.zeros((tm, HEAD_PAD - QK_DIM), F32)], -1)
    scale = QK_DIM ** -0.5 * math.log2(math.e)
    for h in range(MLA_HEADS):
        sl = slice(h * HEAD_PAD, (h + 1) * HEAD_PAD)
        blk = q[:, sl]
        rot = blk * ct + pltpu.roll(blk, half, 1) * s1 + pltpu.roll(blk, HEAD_PAD - half, 1) * s2
        q_ref[:, sl] = (rot * scale).astype(BF16)
        k_ref[:, sl] = (knope[:, sl] + kpe_slot).astype(BF16)


def _flash_body(qt_ref, kt_ref, *refs, tq, tk, sub, ahead, has_lead):
    if has_lead:
        q_ref, k_ref, v_ref, mk_ref, mv_ref, o_ref, m_sc, acc_sc = refs
    else:
        q_ref, k_ref, v_ref, o_ref, m_sc, acc_sc = refs
    t = pl.program_id(2)
    qi, ki = qt_ref[t], kt_ref[t]
    blocks = [(hh, r0) for r0 in range(0, tq, sub) for hh in range(2)]

    def q_blk(hh, r0):
        return q_ref[r0:r0 + sub, hh * HEAD_PAD:(hh + 1) * HEAD_PAD]

    def weights(hh, r0, s, first):
        rows = slice(r0, r0 + sub)
        width = s.shape[1]
        m_cur = jnp.max(s, -1, keepdims=True)
        if first:
            m_new = jnp.broadcast_to(m_cur, (sub, LANES))
        else:
            m_old = m_sc[hh, rows, :]
            m_new = jnp.maximum(m_old, m_cur)
        m_wide = jnp.concatenate([m_new] * (width // LANES), 1) if width >= LANES else m_new[:, :width]
        p = jnp.exp2(s - m_wide)
        corr = None if first else jnp.exp2(m_old - m_new)
        m_sc[hh, rows, :] = m_new
        return p.astype(BF16), corr

    def attend(score_fn, vals, first=False):
        lane = lax.broadcasted_iota(jnp.int32, vals.shape, 1)
        one = jnp.ones(vals.shape, BF16)
        vals_h = [jnp.where(lane < V_DIM, vals, one), jnp.where(lane < V_DIM, one, vals)]
        scores = {}
        for i in range(len(blocks) + ahead):
            if i < len(blocks):
                scores[i] = score_fn(*blocks[i])
            if i >= ahead:
                hh, r0 = blocks[i - ahead]
                p, corr = weights(hh, r0, scores.pop(i - ahead), first)
                rows = slice(r0, r0 + sub)
                pv = _dot(p, vals_h[hh][:p.shape[1]])
                acc_sc[hh, rows, :] = pv if first else corr * acc_sc[hh, rows, :] + pv

    @pl.when(ki == 0)
    def _init():
        if has_lead:
            attend(lambda hh, r0: _dot_nt(q_blk(hh, r0), mk_ref[:, hh * HEAD_PAD:(hh + 1) * HEAD_PAD]),
                   mv_ref[...], first=True)
        else:
            m_sc[...] = jnp.full(m_sc.shape, NEG, F32)
            acc_sc[...] = jnp.zeros(acc_sc.shape, F32)

    def step(masked):
        def score(hh, r0):
            kw = r0 + sub if masked else tk
            s = _dot_nt(q_blk(hh, r0), k_ref[:kw, hh * HEAD_PAD:(hh + 1) * HEAD_PAD])
            if masked:
                row = r0 + lax.broadcasted_iota(jnp.int32, (sub, kw), 0)
                col = lax.broadcasted_iota(jnp.int32, (sub, kw), 1)
                s = jnp.where(col <= row, s, NEG)
            return s
        attend(score, v_ref[...])

    @pl.when(ki < qi)
    def _below():
        step(False)

    @pl.when(ki == qi)
    def _diag():
        step(True)
        lane = lax.broadcasted_iota(jnp.int32, (tq, 2 * V_DIM), 1)
        a0, a1 = acc_sc[0], acc_sc[1]
        o0 = a0 / pltpu.roll(a0, V_DIM, 1)
        o1 = a1 / pltpu.roll(a1, V_DIM, 1)
        o_ref[...] = jnp.where(lane < V_DIM, o0, o1).astype(BF16)


def _flash(q, k, v, n_seq, seq_len, row_off, tile, lead_off=None):
    tq = tk = tile
    nq = seq_len // tq
    qo = row_off // tq
    has_lead = lead_off is not None
    pairs = [(i, j) for i in range(nq) for j in range(i + 1)]
    qt = jnp.array([i for i, _ in pairs], jnp.int32)
    kt = jnp.array([j for _, j in pairs], jnp.int32)
    in_specs = [
        pl.BlockSpec((tq, 2 * HEAD_PAD), lambda b, h, t, qt, kt: (qo + b * nq + qt[t], h)),
        pl.BlockSpec((tk, 2 * HEAD_PAD), lambda b, h, t, qt, kt: (qo + b * nq + kt[t], h)),
        pl.BlockSpec((tk, 2 * V_DIM), lambda b, h, t, qt, kt: (qo + b * nq + kt[t], h)),
    ]
    args = [q, k, v]
    if has_lead:
        lo = lead_off // N_META
        in_specs += [pl.BlockSpec((N_META, 2 * HEAD_PAD), lambda b, h, t, qt, kt: (lo + b, h)),
                     pl.BlockSpec((N_META, 2 * V_DIM), lambda b, h, t, qt, kt: (lo + b, h))]
        args += [k, v]
    return pl.pallas_call(
        functools.partial(_flash_body, tq=tq, tk=tk, sub=min(ATTN_SUB, tq), ahead=ATTN_AHEAD, has_lead=has_lead),
        grid_spec=pltpu.PrefetchScalarGridSpec(
            num_scalar_prefetch=2, grid=(n_seq, MLA_HEADS // 2, len(pairs)),
            in_specs=in_specs,
            out_specs=pl.BlockSpec((tq, 2 * V_DIM), lambda b, h, t, qt, kt: (b * nq + qt[t], h)),
            scratch_shapes=[pltpu.VMEM((2, tq, LANES), F32), pltpu.VMEM((2, tq, 2 * V_DIM), F32)]),
        out_shape=jax.ShapeDtypeStruct((n_seq * seq_len, MLA_HEADS * V_DIM), BF16),
        compiler_params=_params("parallel", "parallel", "arbitrary"),
        name="mla_flash_lead" if has_lead else "mla_flash")(qt, kt, *args)


def _qlat_body(q_ref, w_ref, o_ref):
    o_ref[0] = _dot(q_ref[...], w_ref[0]).astype(BF16)


def _decode_body(pt_ref, ql_ref, qp_ref, *refs, t_new, n_pg):
    lat_refs, kr_refs = refs[:n_pg], refs[n_pg:2 * n_pg]
    cn_ref, kn_ref, o_ref, m_sc, l_sc, acc_sc = refs[2 * n_pg:]
    j = pl.program_id(1)
    ql, qp = ql_ref[0], qp_ref[0]

    @pl.when(j == 0)
    def _init():
        m_sc[...] = jnp.full(m_sc.shape, NEG, F32)
        l_sc[...] = jnp.zeros(l_sc.shape, F32)
        acc_sc[...] = jnp.zeros(acc_sc.shape, F32)

    def update(s, vals):
        m_old = m_sc[...]
        m_new = jnp.maximum(m_old, jnp.max(s, -1, keepdims=True))
        p = jnp.exp2(s - m_new)
        corr = jnp.exp2(m_old - m_new)
        l_sc[...] = corr * l_sc[...] + jnp.sum(p, -1, keepdims=True)
        acc_sc[...] = corr * acc_sc[...] + _dot(p.astype(BF16), vals)
        m_sc[...] = m_new

    lat = jnp.concatenate([r[0, 0].astype(BF16) for r in lat_refs], 0)
    kr_t = jnp.concatenate([r[0, 0].astype(BF16) for r in kr_refs], 1)
    update(_dot_nt(ql, lat) + _dot(qp, kr_t), lat)

    @pl.when(j == pl.num_programs(1) - 1)
    def _fin():
        cn = cn_ref[0].astype(BF16)
        kn = kn_ref[0].astype(BF16)
        s = _dot_nt(ql, cn) + _dot_nt(qp, kn)
        rows = s.shape[0]
        t_q = lax.rem(lax.broadcasted_iota(jnp.int32, (rows, t_new), 0), t_new)
        t_k = lax.broadcasted_iota(jnp.int32, (rows, t_new), 1)
        update(jnp.where(t_k <= t_q, s, NEG), cn)
        o_ref[0] = (acc_sc[...] / l_sc[...]).astype(BF16)


def _ouv_body(o_ref, w_ref, out_ref):
    out_ref[...] = _dot(o_ref[...], w_ref[0]).astype(BF16)


def _rope_tables(pos):
    half = ROPE_DIM // 2
    inv = 1.0 / (ROPE_THETA ** (jnp.arange(half, dtype=F32) * (2.0 / ROPE_DIM)))
    ang = pos.astype(F32)[:, None] * inv
    c, s = jnp.cos(ang), jnp.sin(ang)
    n = pos.shape[0]
    one, zero = jnp.ones((n, NOPE_DIM), F32), jnp.zeros((n, NOPE_DIM), F32)
    pad = jnp.zeros((n, HEAD_PAD - QK_DIM), F32)
    z16 = jnp.zeros((n, half), F32)
    ct = jnp.concatenate([one, c, c, pad], 1)
    s1 = jnp.concatenate([zero, z16, s, pad], 1)
    s2 = jnp.concatenate([zero, -s, z16, pad], 1)
    return ct, s1, s2


def _mla_weights(m, w_dq, q_norm, w_uq, w_dkv, kv_norm, w_uk, w_uv, w_o):
    wq = jnp.pad(w_uq[m], ((0, 0), (0, 0), (0, HEAD_PAD - QK_DIM))).reshape(Q_RANK, MLA_HEADS * HEAD_PAD)
    wk = jnp.pad(w_uk[m], ((0, 0), (0, 0), (0, HEAD_PAD - NOPE_DIM))).reshape(KV_RANK, MLA_HEADS * HEAD_PAD)
    wv = w_uv[m].reshape(KV_RANK, MLA_HEADS * V_DIM)
    wukt = jnp.pad(jnp.transpose(w_uk[m], (1, 2, 0)), ((0, 0), (0, HEAD_PAD - NOPE_DIM), (0, 0)))
    wuv_h = jnp.transpose(w_uv[m], (1, 0, 2)).reshape(MLA_HEADS // 2, 2, KV_RANK, V_DIM)
    eye2 = jnp.eye(2, dtype=F32)
    wuv_pair = jnp.einsum('pirv,ij->pirjv', wuv_h, eye2).reshape(MLA_HEADS // 2, 2 * KV_RANK, 2 * V_DIM)
    return dict(wdq=w_dq[m].astype(BF16), qn=q_norm[m][None], wq=wq.astype(BF16), wdkv=w_dkv[m].astype(BF16),
                kvn=kv_norm[m][None], wk=wk.astype(BF16), wv=wv.astype(BF16), wukt=wukt.astype(BF16),
                wuv_pair=wuv_pair.astype(BF16), wo=w_o[m].reshape(MLA_HEADS * V_DIM, D_MODEL).astype(BF16))


def _mla_layer(x, lay, tabs, w, m, lat_pool, kr_pool, page_table):
    n_seq, seq, n_dec, t_dec = lay["B"], lay["S"], lay["Bd"], lay["Td"]
    q, k, v, c, kpe = _rows_call(
        _mla_proj_body, "mla_proj", [x, *tabs],
        [w["wdq"], w["qn"], w["wq"], w["wdkv"], w["kvn"], w["wk"], w["wv"]],
        [(MLA_HEADS * HEAD_PAD, BF16), (MLA_HEADS * HEAD_PAD, BF16), (MLA_HEADS * V_DIM, BF16),
         (KV_RANK, F32), (ROPE_DIM, F32)])
    soff, moff = lay["soff"], lay["moff"]
    o_real = _flash(q, k, v, n_seq, seq, 0, min(ATTN_TILE, seq), lead_off=moff)
    o_meta = _flash(q, k, v, n_seq, N_META, moff, N_META)

    n_rows = n_dec * t_dec
    hq = MLA_HEADS * t_dec
    q_lat = pl.pallas_call(
        _qlat_body, grid=(MLA_HEADS,),
        in_specs=[pl.BlockSpec((n_rows, HEAD_PAD), lambda h: (soff // n_rows, h)),
                  pl.BlockSpec((1, HEAD_PAD, KV_RANK), lambda h: (h, 0, 0))],
        out_specs=pl.BlockSpec((1, n_rows, KV_RANK), lambda h: (h, 0, 0)),
        out_shape=jax.ShapeDtypeStruct((MLA_HEADS, n_rows, KV_RANK), BF16),
        compiler_params=_params("parallel"), name="mla_qlat")(q, w["wukt"])
    q_lat = q_lat.reshape(MLA_HEADS, n_dec, t_dec, KV_RANK).transpose(1, 0, 2, 3).reshape(n_dec, hq, KV_RANK)
    q_s = q[soff:soff + n_rows].reshape(n_dec, t_dec, MLA_HEADS, HEAD_PAD)[..., NOPE_DIM:QK_DIM]
    q_pe = q_s.transpose(0, 2, 1, 3).reshape(n_dec, hq, ROPE_DIM)
    c_new = c[soff:soff + n_rows].reshape(n_dec, t_dec, KV_RANK)
    k_new = kpe[soff:soff + n_rows].reshape(n_dec, t_dec, ROPE_DIM)
    n_pages, page = page_table.shape[1], lat_pool.shape[2]
    n_pg = math.gcd(DECODE_PAGES, n_pages)
    kr_pool_t = jnp.swapaxes(kr_pool, 2, 3)
    lat_spec = lambda i: pl.BlockSpec((1, 1, page, KV_RANK), lambda bb, j, pt: (m, pt[bb, j * n_pg + i], 0, 0))
    kr_spec = lambda i: pl.BlockSpec((1, 1, ROPE_DIM, page), lambda bb, j, pt: (m, pt[bb, j * n_pg + i], 0, 0))
    o_lat = pl.pallas_call(
        functools.partial(_decode_body, t_new=t_dec, n_pg=n_pg),
        grid_spec=pltpu.PrefetchScalarGridSpec(
            num_scalar_prefetch=1, grid=(n_dec, n_pages // n_pg),
            in_specs=[pl.BlockSpec((1, hq, KV_RANK), lambda bb, j, pt: (bb, 0, 0)),
                      pl.BlockSpec((1, hq, ROPE_DIM), lambda bb, j, pt: (bb, 0, 0))]
            + [lat_spec(i) for i in range(n_pg)] + [kr_spec(i) for i in range(n_pg)]
            + [pl.BlockSpec((1, t_dec, KV_RANK), lambda bb, j, pt: (bb, 0, 0)),
               pl.BlockSpec((1, t_dec, ROPE_DIM), lambda bb, j, pt: (bb, 0, 0))],
            out_specs=pl.BlockSpec((1, hq, KV_RANK), lambda bb, j, pt: (bb, 0, 0)),
            scratch_shapes=[pltpu.VMEM((hq, 1), F32), pltpu.VMEM((hq, 1), F32), pltpu.VMEM((hq, KV_RANK), F32)]),
        out_shape=jax.ShapeDtypeStruct((n_dec, hq, KV_RANK), BF16),
        compiler_params=_params("parallel", "arbitrary"), name="mla_decode")(
            page_table, q_lat, q_pe, *([lat_pool] * n_pg), *([kr_pool_t] * n_pg), c_new, k_new)
    o_lat = o_lat.reshape(n_dec, MLA_HEADS, t_dec, KV_RANK).transpose(0, 2, 1, 3).reshape(n_rows, MLA_HEADS * KV_RANK)
    o_samp = pl.pallas_call(
        _ouv_body, grid=(MLA_HEADS // 2,),
        in_specs=[pl.BlockSpec((n_rows, 2 * KV_RANK), lambda h: (0, h)),
                  pl.BlockSpec((1, 2 * KV_RANK, 2 * V_DIM), lambda h: (h, 0, 0))],
        out_specs=pl.BlockSpec((n_rows, 2 * V_DIM), lambda h: (0, h)),
        out_shape=jax.ShapeDtypeStruct((n_rows, MLA_HEADS * V_DIM), BF16),
        compiler_params=_params("parallel"), name="mla_ouv")(o_lat, w["wuv_pair"])

    return _assemble(lay, o_real, o_samp, o_meta), c, kpe


def _split_dot(a, sel):
    hi = a.astype(BF16)
    lo = (a - hi.astype(F32)).astype(BF16)
    return _dot(hi, sel) + _dot(lo, sel)


def _head_sum(x, gs, gb):
    return _split_dot(_split_dot(x, gs), gb)


def _softplus(z):
    return jnp.maximum(z, 0.0) + jnp.log(1.0 + jnp.exp(-jnp.abs(z)))


def _rw_proj_body(x_ref, xp_ref, mu_ref, wr_ref, wk_ref, wv_ref, w0_ref, w1_ref, w2_ref, a0_ref, a1_ref, a2_ref,
                  g1_ref, g2_ref, kk_ref, ka_ref, rk_ref, gs_ref, gb_ref,
                  r_o, d_o, k_o, v_o, kk_o, a_o, g_o, bonus_o):
    x = x_ref[...]
    xx = xp_ref[...] - x
    mix = lambda j: (x + xx * mu_ref[j:j + 1, :]).astype(BF16)
    w_lo = _dot(mix(1), w1_ref[...])
    a_lo = _dot(mix(4), a1_ref[...])
    g_lo = _dot(mix(5), g1_ref[...])
    r = _dot(mix(0), wr_ref[...])
    k = _dot(mix(2), wk_ref[...])
    v = _dot(mix(3), wv_ref[...])
    wl = w0_ref[...] + _dot(jnp.tanh(w_lo).astype(BF16), w2_ref[...])
    w = -_softplus(-wl) - 0.5
    a = _sigmoid(a0_ref[...] + _dot(a_lo.astype(BF16), a2_ref[...]))
    g = _dot(_sigmoid(g_lo).astype(BF16), g2_ref[...])
    gs, gb = gs_ref[...], gb_ref[...]
    kk = k * kk_ref[...]
    kk = kk * lax.rsqrt(jnp.maximum(_head_sum(kk * kk, gs, gb), 1e-24))
    k = k * (1.0 + (a - 1.0) * ka_ref[...])
    r_o[...] = r
    d_o[...] = -jnp.exp(w)
    k_o[...] = k
    v_o[...] = v
    kk_o[...] = kk
    a_o[...] = a
    g_o[...] = g
    bonus_o[...] = _head_sum(r * k * rk_ref[...], gs, gb) * v


def _rw_chunk_body(*refs, chunk, n_fac, group):
    s0_ref, y_ref, st_ref, s_sc = refs[6 * group:]
    load = lambda i: jnp.concatenate([refs[i * group + g][...] for g in range(group)], 0)
    ci = pl.program_id(1)

    @pl.when(ci == 0)
    def _init():
        s_sc[...] = s0_ref[...]

    row = lax.broadcasted_iota(jnp.int32, (chunk, chunk), 0)
    col = lax.broadcasted_iota(jnp.int32, (chunk, chunk), 1)
    strict, incl = col < row, col <= row
    rows = group * chunk
    if group == 1:
        same, below = None, incl
    else:
        rr = lax.broadcasted_iota(jnp.int32, (rows, rows), 0)
        cc = lax.broadcasted_iota(jnp.int32, (rows, rows), 1)
        same = (rr // chunk) == (cc // chunk)
        below = jnp.logical_and(same, cc <= rr)
    d_all, k_all, kk_all = load(1), load(2), load(4)
    cum = _dot(below.astype(F32), d_all, HI)
    if group == 1:
        cum_end = cum[chunk - 1:chunk, :]
    else:
        cum_end = _dot(same.astype(F32), d_all, HI)
    be_all = kk_all * load(5)
    w_inv = jnp.exp(-cum)
    w_rem = jnp.exp(cum_end - cum)
    al_t_all = (-kk_all * jnp.exp(cum - d_all)).astype(BF16)
    r_t_all = load(0) * jnp.exp(cum)
    be_h_all = (be_all * w_inv).astype(BF16)
    k_h_all = (k_all * w_inv).astype(BF16)
    be_e_all = (be_all * w_rem).astype(BF16)
    k_e_all = (k_all * w_rem).astype(BF16)
    w_end = jnp.exp(cum_end)
    v_all = load(3).astype(BF16)
    zero = jnp.zeros((chunk, RW_HEAD), BF16)
    chains = [(g, h) for g in range(group) for h in range(RW_HEADS)]
    ids = range(len(chains))
    cut = lambda a, g, h: a[g * chunk:(g + 1) * chunk, h * RW_HEAD:(h + 1) * RW_HEAD]
    gram = [_dot_nt(jnp.concatenate([cut(al_t_all, *c), cut(r_t_all, *c).astype(BF16)], 0),
                    jnp.concatenate([cut(be_h_all, *c), cut(k_h_all, *c)], 0)) for c in chains]
    l_ab = [jnp.where(strict, g[:chunk, :chunk], 0.0).astype(BF16) for g in gram]
    l_ak = [jnp.where(strict, g[:chunk, chunk:], 0.0).astype(BF16) for g in gram]
    m_r = [jnp.concatenate([jnp.where(incl, g[chunk:, :chunk], 0.0),
                            jnp.where(incl, g[chunk:, chunk:], 0.0)], 1).astype(BF16) for g in gram]
    u = [jnp.concatenate([cut(al_t_all, *c).astype(F32), _dot(l_ak[i], cut(v_all, *c))], 1) for i, c in zip(ids, chains)]
    lp = l_ab
    for f in range(n_fac):
        u = [u[i] + _dot(lp[i], u[i].astype(BF16)) for i in ids]
        if f + 1 < n_fac:
            lp = [_dot(lp[i], lp[i]).astype(BF16) for i in ids]
    z = [jnp.concatenate([u[i].astype(BF16), jnp.concatenate([zero, cut(v_all, *c)], 1)], 0) for i, c in zip(ids, chains)]
    g = [_dot(m_r[i], z[i]) for i in ids]
    ad = [_dot_tn(z[i], jnp.concatenate([cut(be_e_all, *c), cut(k_e_all, *c)], 0)) for i, c in zip(ids, chains)]
    s_old = [s_sc[c[0], c[1]] for c in chains]
    s_bf = [s.astype(BF16) for s in s_old]
    y = [_dot_nt((cut(r_t_all, *c) + g[i][:, :RW_HEAD]).astype(BF16), s_bf[i]) + g[i][:, RW_HEAD:]
         for i, c in zip(ids, chains)]
    for gi in range(group):
        y_ref[gi] = jnp.concatenate(y[gi * RW_HEADS:(gi + 1) * RW_HEADS], 1)
    for i, (gi, h) in zip(ids, chains):
        w_h = w_end[gi * chunk:gi * chunk + 1, h * RW_HEAD:(h + 1) * RW_HEAD]
        s_sc[gi, h] = s_old[i] * w_h + _dot(s_bf[i], ad[i][:RW_HEAD].astype(BF16)) + ad[i][RW_HEAD:]

    @pl.when(ci == pl.num_programs(1) - 1)
    def _fin():
        st_ref[...] = s_sc[...]


def _rw_scan(seqs, s0, n_seq, seq_len, row_off, chunk, group=1):
    nc = seq_len // chunk
    assert n_seq % group == 0
    off = row_off // chunk
    n_fac = max(1, math.ceil(math.log2(chunk)))
    row_spec = lambda g: pl.BlockSpec((chunk, D_MODEL), lambda b, c: (off + (b * group + g) * nc + c, 0))
    st_spec = pl.BlockSpec((group, RW_HEADS, RW_HEAD, RW_HEAD), lambda b, c: (b, 0, 0, 0))
    y, st = pl.pallas_call(
        functools.partial(_rw_chunk_body, chunk=chunk, n_fac=n_fac, group=group),
        grid=(n_seq // group, nc),
        in_specs=[row_spec(g) for _ in range(6) for g in range(group)] + [st_spec],
        out_specs=[pl.BlockSpec((group, chunk, D_MODEL), lambda b, c: (b, c, 0)), st_spec],
        out_shape=[jax.ShapeDtypeStruct((n_seq, seq_len, D_MODEL), F32),
                   jax.ShapeDtypeStruct((n_seq, RW_HEADS, RW_HEAD, RW_HEAD), F32)],
        scratch_shapes=[pltpu.VMEM((group, RW_HEADS, RW_HEAD, RW_HEAD), F32)],
        compiler_params=_params("parallel", "arbitrary"), name=f"rwkv_chunk{chunk}")(
            *[a for a in seqs for _ in range(group)], s0)
    return y.reshape(n_seq * seq_len, D_MODEL), st


def _rw_mix(x, y_ref, g_ref, bonus_ref, lg_ref, lb_ref, gs_ref, gb_ref, wo_ref, ng_ref, nb_ref):
    y = y_ref[...]
    gs, gb = gs_ref[...], gb_ref[...]
    mean = _head_sum(y, gs, gb) * (1.0 / RW_HEAD)
    yc = y - mean
    var = _head_sum(yc * yc, gs, gb) * (1.0 / RW_HEAD)
    yn = yc * lax.rsqrt(var + GN_EPS) * lg_ref[...] + lb_ref[...]
    z = ((yn + bonus_ref[...]) * g_ref[...]).astype(BF16)
    return _layer_norm(ALPHA * x + _dot(z, wo_ref[...]), ng_ref[...], nb_ref[...])


def _rwkv_layer(x, lay, m, p, shift_s, wkv_s, g, b):
    n_seq, seq, n_dec, t_dec = lay["B"], lay["S"], lay["Bd"], lay["Td"]
    soff, moff = lay["soff"], lay["moff"]
    n_rows = n_dec * t_dec
    xm = x[moff:moff + n_seq * N_META].reshape(n_seq, N_META, D_MODEL)
    xs = x[soff:soff + n_rows].reshape(n_dec, t_dec, D_MODEL)
    prev_s = jnp.concatenate([shift_s[m][:, None], xs[:, :-1]], 1).reshape(n_rows, D_MODEL)
    xprev = jnp.concatenate([jnp.zeros((1, D_MODEL), F32), x[:-1]], 0)
    xprev = lax.dynamic_update_slice(xprev, prev_s, (soff, 0))
    for bi in range(n_seq):
        xprev = lax.dynamic_update_slice(xprev, xm[bi, -1:], (bi * seq, 0))
        xprev = lax.dynamic_update_slice(xprev, jnp.zeros((1, D_MODEL), F32), (moff + bi * N_META, 0))
    x_last = jnp.stack([x[(bi + 1) * seq - 1] for bi in range(n_seq)])

    row = lambda a: a[m][None]
    gs = jnp.pad(jnp.repeat(jnp.eye(RW_HEADS, dtype=BF16), RW_HEAD, 0), ((0, 0), (0, LANES - RW_HEADS)))
    gb = gs.T
    bf = lambda a: a[m].astype(BF16)
    r, d, k, v, kk, a, gate, bonus = _rows_call(
        _rw_proj_body, "rwkv_proj", [x, xprev],
        [p['rw_mu'][m], bf(p['rw_wr']), bf(p['rw_wk']), bf(p['rw_wv']), row(p['rw_w0']), bf(p['rw_w1']),
         bf(p['rw_w2']), row(p['rw_a0']), bf(p['rw_a1']), bf(p['rw_a2']), bf(p['rw_g1']), bf(p['rw_g2']),
         row(p['rw_k_k']), row(p['rw_k_a']), p['rw_r_k'][m].reshape(1, D_MODEL), gs, gb],
        [(D_MODEL, F32)] * 8)
    seqs = (r, d, k, v, kk, a)
    zero_state = jnp.zeros((n_seq, RW_HEADS, RW_HEAD, RW_HEAD), F32)
    y_meta, st = _rw_scan(seqs, zero_state, n_seq, N_META, moff, N_META, group=n_seq)
    y_real, st_p = _rw_scan(seqs, st, n_seq, seq, 0, min(RW_CHUNK, seq), group=n_seq)
    t_pad = 8
    pad_rows = lambda z: jnp.pad(z[soff:soff + n_rows].reshape(n_dec, t_dec, D_MODEL),
                                 ((0, 0), (0, t_pad - t_dec), (0, 0))).reshape(n_dec * t_pad, D_MODEL)
    y_s, st_s = _rw_scan(tuple(pad_rows(z) for z in seqs), wkv_s[m], n_dec, t_pad, 0, t_pad,
                         group=math.gcd(RW_GROUP, n_dec))
    y_s = y_s.reshape(n_dec, t_pad, D_MODEL)[:, :t_dec].reshape(n_rows, D_MODEL)
    y_all = _assemble(lay, y_real, y_s, y_meta)
    mix = (_rw_mix, [y_all, gate, bonus], [row(p['rw_lnx_g']), row(p['rw_lnx_b']), gs, gb, bf(p['rw_wo']), g, b])
    return mix, st_p, x_last, st_s, xs[:, -1]


def _s5_disc_body(lre_ref, lim_ref, dt_ref, bre_ref, bim_ref, lre_s, lim_s, dt_s, bbr_o, bbi_o, ar_o, ai_o):
    def disc(lre, lim, dt):
        mag = jnp.exp(lre * dt)
        ab_re, ab_im = mag * jnp.cos(lim * dt), mag * jnp.sin(lim * dt)
        den = lre * lre + lim * lim
        nr = ab_re - 1.0
        return ab_re, ab_im, (nr * lre + ab_im * lim) / den, (ab_im * lre - nr * lim) / den

    _, _, co_re, co_im = disc(lre_ref[...], lim_ref[...], dt_ref[...])
    b_re, b_im = bre_ref[...], bim_ref[...]
    bbr_o[...] = co_re * b_re - co_im * b_im
    bbi_o[...] = co_re * b_im + co_im * b_re
    ab_re, ab_im, _, _ = disc(lre_s[...], lim_s[...], dt_s[...])
    ar_o[...] = ab_re
    ai_o[...] = ab_im


def _s5_input(u, bre_ref, bim_ref):
    ub = u.astype(BF16)
    cw = D_MODEL // S5_BLOCKS
    re = [_dot(ub[:, j * cw:(j + 1) * cw], bre_ref[j]) for j in range(S5_BLOCKS)]
    im = [_dot(ub[:, j * cw:(j + 1) * cw], bim_ref[j]) for j in range(S5_BLOCKS)]
    return re, im


def _s5_output(u, hre, him, cre_ref, cim_ref, dv_ref, wv_ref, wg_ref, g_ref, b_ref):
    w = S5_WIDTH // S5_BLOCKS
    hb_re, hb_im = hre.astype(BF16), him.astype(BF16)
    ys = [_dot(hb_re[:, j * w:(j + 1) * w], cre_ref[j]) - _dot(hb_im[:, j * w:(j + 1) * w], cim_ref[j])
          for j in range(S5_BLOCKS)]
    y = jnp.concatenate(ys, 1) + dv_ref[...] * u
    z = jax.nn.gelu(y).astype(BF16)
    out = _dot(z, wv_ref[...]) * _sigmoid(_dot(z, wg_ref[...]))
    return _layer_norm(ALPHA * u + out, g_ref[...], b_ref[...])


def _s5_seq_body(x_ref, h0r_ref, h0i_ref, ar_ref, ai_ref, bre_ref, bim_ref, cre_ref, cim_ref, dv_ref, wv_ref, wg_ref,
                 g_ref, b_ref, o_ref, htr_ref, hti_ref, hre_sc, him_sc, sr_sc, si_sc, *, tt):
    ti = pl.program_id(1)

    @pl.when(ti == 0)
    def _init():
        sr_sc[...] = h0r_ref[0]
        si_sc[...] = h0i_ref[0]

    u = x_ref[...]
    re, im = _s5_input(u, bre_ref, bim_ref)
    w = S5_WIDTH // S5_BLOCKS
    for j in range(S5_BLOCKS):
        hre_sc[:, j * w:(j + 1) * w] = re[j]
        him_sc[:, j * w:(j + 1) * w] = im[j]
    ar, ai = ar_ref[...], ai_ref[...]

    def step(t, carry):
        hr, hi = carry
        nr = ar * hr - ai * hi + hre_sc[pl.ds(t, 1), :]
        ni = ar * hi + ai * hr + him_sc[pl.ds(t, 1), :]
        hre_sc[pl.ds(t, 1), :] = nr
        him_sc[pl.ds(t, 1), :] = ni
        return nr, ni

    hr, hi = lax.fori_loop(0, tt, step, (sr_sc[...], si_sc[...]), unroll=S5_UNROLL)
    sr_sc[...] = hr
    si_sc[...] = hi
    o_ref[...] = _s5_output(u, hre_sc[...], him_sc[...], cre_ref, cim_ref, dv_ref, wv_ref, wg_ref, g_ref, b_ref)

    @pl.when(ti == pl.num_programs(1) - 1)
    def _fin():
        htr_ref[0] = hr
        hti_ref[0] = hi


def _s5_step_body(x_ref, h0r_ref, h0i_ref, ar_ref, ai_ref, bre_ref, bim_ref, cre_ref, cim_ref, dv_ref, wv_ref, wg_ref,
                  g_ref, b_ref, o_ref, htr_ref, hti_ref, sr_sc, si_sc):
    t = pl.program_id(0)

    @pl.when(t == 0)
    def _init():
        sr_sc[...] = h0r_ref[...]
        si_sc[...] = h0i_ref[...]

    u = x_ref[0]
    re, im = _s5_input(u, bre_ref, bim_ref)
    bur, bui = jnp.concatenate(re, 1), jnp.concatenate(im, 1)
    ar, ai = ar_ref[...], ai_ref[...]
    hr, hi = sr_sc[...], si_sc[...]
    nr = ar * hr - ai * hi + bur
    ni = ar * hi + ai * hr + bui
    sr_sc[...] = nr
    si_sc[...] = ni
    o_ref[0] = _s5_output(u, nr, ni, cre_ref, cim_ref, dv_ref, wv_ref, wg_ref, g_ref, b_ref)

    @pl.when(t == pl.num_programs(0) - 1)
    def _fin():
        htr_ref[...] = nr
        hti_ref[...] = ni


def _s5_weights(m, p):
    rep = lambda a: jnp.repeat(a, S5_GROUP, axis=-1)
    lre, lim = p['s5_lam_re'][m], p['s5_lam_im'][m]
    dt = jnp.broadcast_to(jnp.exp(p['s5_log_dt'][m])[:, None], (S5_GROUPS, S5_STATE))
    flat = lambda a: a[m].reshape(S5_GROUPS, S5_STATE * S5_GROUP)
    n = S5_STATE * S5_GROUP
    bb_re, bb_im, ab_re, ab_im = pl.pallas_call(
        _s5_disc_body,
        out_shape=[jax.ShapeDtypeStruct((S5_GROUPS, n), F32)] * 2 + [jax.ShapeDtypeStruct((S5_GROUPS, S5_STATE), F32)] * 2,
        name="s5_disc")(rep(lre), rep(lim), rep(dt), flat(p['s5_b_re']), flat(p['s5_b_im']), lre, lim, dt)
    gpb = S5_GROUPS // S5_BLOCKS
    eye = jnp.eye(gpb, dtype=F32)

    def in_blocks(bb):
        t = bb.reshape(S5_BLOCKS, gpb, S5_STATE, S5_GROUP).transpose(0, 1, 3, 2)
        return jnp.einsum('jgsp,gh->jgshp', t, eye).reshape(S5_BLOCKS, gpb * S5_GROUP, gpb * S5_STATE).astype(BF16)

    def out_blocks(c):
        t = c.reshape(S5_BLOCKS, gpb, S5_GROUP, S5_STATE).transpose(0, 1, 3, 2)
        return jnp.einsum('jgps,gh->jgphs', t, eye).reshape(S5_BLOCKS, gpb * S5_STATE, gpb * S5_GROUP).astype(BF16)

    return dict(ar=ab_re.reshape(1, S5_WIDTH), ai=ab_im.reshape(1, S5_WIDTH),
                bre=in_blocks(bb_re), bim=in_blocks(bb_im),
                cre=out_blocks(p['s5_c_re'][m]), cim=out_blocks(p['s5_c_im'][m]),
                dv=p['s5_d'][m][None], wv=p['s5_wv'][m].astype(BF16), wg=p['s5_wg'][m].astype(BF16))


def _s5_seq(x, h0r, h0i, w, g, b, n_seq, seq_len, row_off, tt):
    nt = seq_len // tt
    off = row_off // tt
    consts = [w["ar"], w["ai"], w["bre"], w["bim"], w["cre"], w["cim"], w["dv"], w["wv"], w["wg"], g, b]
    st_spec = pl.BlockSpec((1, 1, S5_WIDTH), lambda bb, t: (bb, 0, 0))
    return pl.pallas_call(
        functools.partial(_s5_seq_body, tt=tt), grid=(n_seq, nt),
        in_specs=[pl.BlockSpec((tt, D_MODEL), lambda bb, t: (off + bb * nt + t, 0)), st_spec, st_spec]
        + [_const_spec(a, 2) for a in consts],
        out_specs=[pl.BlockSpec((tt, D_MODEL), lambda bb, t: (bb * nt + t, 0)), st_spec, st_spec],
        out_shape=[jax.ShapeDtypeStruct((n_seq * seq_len, D_MODEL), F32)]
        + [jax.ShapeDtypeStruct((n_seq, 1, S5_WIDTH), F32)] * 2,
        scratch_shapes=[pltpu.VMEM((tt, S5_WIDTH), F32), pltpu.VMEM((tt, S5_WIDTH), F32),
                        pltpu.VMEM((1, S5_WIDTH), F32), pltpu.VMEM((1, S5_WIDTH), F32)],
        compiler_params=_params("parallel", "arbitrary"), name=f"s5_seq{tt}")(x, h0r, h0i, *consts)


def _s5_layer(x, lay, m, p, re_s, im_s, g, b):
    n_seq, seq, n_dec, t_dec = lay["B"], lay["S"], lay["Bd"], lay["Td"]
    soff, moff = lay["soff"], lay["moff"]
    n_rows = n_dec * t_dec
    w = _s5_weights(m, p)
    zero = jnp.zeros((n_seq, 1, S5_WIDTH), F32)
    o_meta, hr, hi = _s5_seq(x, zero, zero, w, g, b, n_seq, N_META, moff, N_META)
    o_real, hr, hi = _s5_seq(x, hr, hi, w, g, b, n_seq, seq, 0, min(S5_TILE, seq))
    xs = x[soff:soff + n_rows].reshape(n_dec, t_dec, D_MODEL).transpose(1, 0, 2)
    consts = [w["ar"], w["ai"], w["bre"], w["bim"], w["cre"], w["cim"], w["dv"], w["wv"], w["wg"], g, b]
    st_spec = pl.BlockSpec((n_dec, S5_WIDTH), lambda t: (0, 0))
    o_s, sr, si = pl.pallas_call(
        _s5_step_body, grid=(t_dec,),
        in_specs=[pl.BlockSpec((1, n_dec, D_MODEL), lambda t: (t, 0, 0)), st_spec, st_spec]
        + [_const_spec(a, 1) for a in consts],
        out_specs=[pl.BlockSpec((1, n_dec, D_MODEL), lambda t: (t, 0, 0)), st_spec, st_spec],
        out_shape=[jax.ShapeDtypeStruct((t_dec, n_dec, D_MODEL), F32)]
        + [jax.ShapeDtypeStruct((n_dec, S5_WIDTH), F32)] * 2,
        scratch_shapes=[pltpu.VMEM((n_dec, S5_WIDTH), F32), pltpu.VMEM((n_dec, S5_WIDTH), F32)],
        compiler_params=_params("arbitrary"), name="s5_step")(
            xs, re_s[m].reshape(n_dec, S5_WIDTH), im_s[m].reshape(n_dec, S5_WIDTH), *consts)
    o_s = o_s.transpose(1, 0, 2).reshape(n_rows, D_MODEL)
    x_new = _assemble(lay, o_real, o_s, o_meta)
    shp = (S5_GROUPS, S5_STATE)
    return x_new, hr.reshape(n_seq, *shp), hi.reshape(n_seq, *shp), sr.reshape(n_dec, *shp), si.reshape(n_dec, *shp)


def _round_up(n, m):
    return (n + m - 1) // m * m


def kernel(x_prompt, x_sample, cache_mla_latent, cache_mla_krope, state_rwkv_wkv, state_rwkv_shift, state_s5_re, state_s5_im, page_table, meta_tokens, ln_g, ln_b, ffn_w1, ffn_w3, ffn_w2, mla_w_dq, mla_q_norm, mla_w_uq, mla_w_dkv, mla_kv_norm, mla_w_uk, mla_w_uv, mla_w_o, rw_mu, rw_wr, rw_wk, rw_wv, rw_w0, rw_w1, rw_w2, rw_a0, rw_a1, rw_a2, rw_g1, rw_g2, rw_k_k, rw_k_a, rw_r_k, rw_lnx_g, rw_lnx_b, rw_wo, s5_lam_re, s5_lam_im, s5_log_dt, s5_b_re, s5_b_im, s5_c_re, s5_c_im, s5_d, s5_wv, s5_wg):
    p = dict(rw_mu=rw_mu, rw_wr=rw_wr, rw_wk=rw_wk, rw_wv=rw_wv, rw_w0=rw_w0, rw_w1=rw_w1, rw_w2=rw_w2,
             rw_a0=rw_a0, rw_a1=rw_a1, rw_a2=rw_a2, rw_g1=rw_g1, rw_g2=rw_g2, rw_k_k=rw_k_k, rw_k_a=rw_k_a,
             rw_r_k=rw_r_k, rw_lnx_g=rw_lnx_g, rw_lnx_b=rw_lnx_b, rw_wo=rw_wo,
             s5_lam_re=s5_lam_re, s5_lam_im=s5_lam_im, s5_log_dt=s5_log_dt, s5_b_re=s5_b_re, s5_b_im=s5_b_im,
             s5_c_re=s5_c_re, s5_c_im=s5_c_im, s5_d=s5_d, s5_wv=s5_wv, s5_wg=s5_wg)
    n_seq, seq, _ = x_prompt.shape
    n_dec, t_dec, _ = x_sample.shape
    past = page_table.shape[1] * cache_mla_latent.shape[2]
    n_rows = n_dec * t_dec
    soff = _round_up(n_seq * seq, ROW_ALIGN)
    moff = _round_up(soff + n_rows, ROW_ALIGN)
    n_tot = _round_up(moff + n_seq * N_META, ROW_ALIGN)
    lay = dict(B=n_seq, S=seq, Bd=n_dec, Td=t_dec, soff=soff, moff=moff, n_tot=n_tot)
    assemble = functools.partial(_assemble, lay)

    meta = jnp.broadcast_to(meta_tokens[None], (n_seq, N_META, D_MODEL)).reshape(n_seq * N_META, D_MODEL)
    x = assemble(x_prompt.reshape(n_seq * seq, D_MODEL), x_sample.reshape(n_rows, D_MODEL), meta)
    pos = assemble(jnp.tile(N_META + jnp.arange(seq, dtype=jnp.int32), n_seq)[:, None],
                   jnp.tile(past + jnp.arange(t_dec, dtype=jnp.int32), n_dec)[:, None],
                   jnp.tile(jnp.arange(N_META, dtype=jnp.int32), n_seq)[:, None])[:, 0]
    tabs = _rope_tables(pos)

    stacked = lambda a: a.reshape((-1,) + a.shape[2:])
    w1_all, w3_all, w2_all = (stacked(a).astype(BF16) for a in (ffn_w1, ffn_w3, ffn_w2))
    g_all, b_all = (a.reshape(-1, 1, D_MODEL) for a in (ln_g, ln_b))

    lat_p, kr_p, lat_s, kr_s = [], [], [], []
    wkv_p = sh_p = wkv_s = sh_s = re_p = im_p = re_s = im_s = None
    stacks = (w1_all, w3_all, w2_all, g_all, b_all)
    ffn_a = lambda i: (2 * i, 3 * i)
    ffn_b = lambda i: (2 * i + 1, 3 * i + 2)
    x = _chain(x, stacks, [ffn_a(0)])
    for i in range(DEPTH):
        kind, m = i % 3, i // 3
        norm = lambda j: (ln_g[i, j][None], ln_b[i, j][None])
        tail = [ffn_b(i)] + ([ffn_a(i + 1)] if i + 1 < DEPTH else [])
        mix = None
        if kind == 0:
            w = _mla_weights(m, mla_w_dq, mla_q_norm, mla_w_uq, mla_w_dkv, mla_kv_norm, mla_w_uk, mla_w_uv, mla_w_o)
            o_all, c, kpe = _mla_layer(x, lay, tabs, w, m, cache_mla_latent, cache_mla_krope, page_table)
            mix = (_proj_mix, [o_all], [w["wo"], (g_all, 3 * i + 1), (b_all, 3 * i + 1)])
            split = lambda a: (jnp.concatenate(
                [a[moff:moff + n_seq * N_META].reshape(n_seq, N_META, -1), a[:n_seq * seq].reshape(n_seq, seq, -1)], 1),
                a[soff:soff + n_rows].reshape(n_dec, t_dec, -1))
            cp, cs = split(c)
            kp, ks = split(kpe)
            lat_p.append(cp), lat_s.append(cs), kr_p.append(kp), kr_s.append(ks)
        elif kind == 1:
            mix, wkv_p, sh_p, wkv_s, sh_s = _rwkv_layer(x, lay, m, p, state_rwkv_shift, state_rwkv_wkv, *norm(1))
        else:
            x, re_p, im_p, re_s, im_s = _s5_layer(x, lay, m, p, state_s5_re, state_s5_im, *norm(1))
        x = _chain(x, stacks, tail, mix)

    y_prompt = x[:n_seq * seq].reshape(n_seq, seq, D_MODEL)
    y_sample = x[soff:soff + n_rows].reshape(n_dec, t_dec, D_MODEL)
    return (y_prompt, y_sample, jnp.stack(lat_p), jnp.stack(kr_p), jnp.stack(lat_s), jnp.stack(kr_s),
            wkv_p[None], sh_p[None], wkv_s[None], sh_s[None], re_p[None], im_p[None], re_s[None], im_s[None])
```

```python
import functools
import math

import jax
import jax.numpy as jnp
from jax import lax
from jax.experimental import pallas as pl
from jax.experimental.pallas import tpu as pltpu

F32 = jnp.float32
BF16 = jnp.bfloat16
HI = lax.Precision.HIGHEST

D_MODEL = 1024
DEPTH = 4
N_META = 16
D_FF = 2816
ALPHA = (2 * DEPTH) ** 0.25
LN_EPS = 1e-5
RMS_EPS = 1e-6
MLA_HEADS = 16
Q_RANK = 768
KV_RANK = 256
NOPE_DIM = 64
ROPE_DIM = 32
QK_DIM = NOPE_DIM + ROPE_DIM
V_DIM = 64
ROPE_THETA = 10000.0
RW_HEAD = 64
RW_HEADS = D_MODEL // RW_HEAD
GN_EPS = 64e-5
S5_GROUP = 16
S5_GROUPS = D_MODEL // S5_GROUP
S5_STATE = 64
S5_WIDTH = S5_GROUPS * S5_STATE
S5_BLOCKS = 8

LANES = 128
HEAD_PAD = 128
ROW_TILE = 256
ROW_ALIGN = 256
ATTN_TILE = 1024
ATTN_SUB = 256
ATTN_AHEAD = 2
DECODE_PAGES = 64
RW_CHUNK = 64
RW_GROUP = 4
S5_TILE = 256
S5_UNROLL = 8
VMEM_LIMIT = 56 * 1024 * 1024
NEG = -1e30


def _dot(a, b, prec=None):
    return jnp.dot(a, b, preferred_element_type=F32, precision=prec)


def _dot_nt(a, b, prec=None):
    return lax.dot_general(a, b, (((1,), (1,)), ((), ())), preferred_element_type=F32, precision=prec)


def _dot_tn(a, b, prec=None):
    return lax.dot_general(a, b, (((0,), (0,)), ((), ())), preferred_element_type=F32, precision=prec)


def _layer_norm(y, g, b):
    mu = jnp.mean(y, -1, keepdims=True)
    yc = y - mu
    var = jnp.mean(yc * yc, -1, keepdims=True)
    return yc * lax.rsqrt(var + LN_EPS) * g + b


def _rms_norm(y, g):
    return y * lax.rsqrt(jnp.mean(y * y, -1, keepdims=True) + RMS_EPS) * g


def _sigmoid(x):
    return 1.0 / (1.0 + jnp.exp(-x))


def _params(*sem):
    return pltpu.CompilerParams(dimension_semantics=sem, vmem_limit_bytes=VMEM_LIMIT)


def _const_spec(a, n_grid):
    zeros = (0,) * a.ndim
    if n_grid == 1:
        imap = lambda i: zeros
    elif n_grid == 2:
        imap = lambda i, j: zeros
    else:
        imap = lambda i, j, k, l: zeros
    return pl.BlockSpec(a.shape, imap, pipeline_mode=pl.Buffered(1))


def _pick_spec(a, idx):
    tail = (0,) * (a.ndim - 1)
    return pl.BlockSpec((None,) + a.shape[1:], lambda i: (idx,) + tail, pipeline_mode=pl.Buffered(1))


def _rows_call(body, name, row_ins, const_ins, outs, tm=ROW_TILE):
    n = row_ins[0].shape[0]
    assert n % tm == 0
    in_specs = [pl.BlockSpec((tm, a.shape[1]), lambda i: (i, 0)) for a in row_ins]
    in_specs += [_pick_spec(*a) if isinstance(a, tuple) else _const_spec(a, 1) for a in const_ins]
    consts = [a[0] if isinstance(a, tuple) else a for a in const_ins]
    out_specs = [pl.BlockSpec((tm, w), lambda i: (i, 0)) for w, _ in outs]
    out_shape = [jax.ShapeDtypeStruct((n, w), dt) for w, dt in outs]
    res = pl.pallas_call(body, grid=(n // tm,), in_specs=in_specs, out_specs=out_specs, out_shape=out_shape,
                         compiler_params=_params("parallel"), name=name)(*row_ins, *consts)
    return res


def _assemble(lay, real, samp, meta):
    w = real.shape[1]
    gap = lambda n: jnp.zeros((n, w), real.dtype)
    return jnp.concatenate([real, gap(lay["soff"] - real.shape[0]), samp,
                            gap(lay["moff"] - lay["soff"] - samp.shape[0]), meta,
                            gap(lay["n_tot"] - lay["moff"] - meta.shape[0])], 0)


FF_CHUNK = 1408


def _ffn_stage(x, w1_ref, w3_ref, w2_ref, g_ref, b_ref):
    xb = x.astype(BF16)
    acc = None
    for c in range(D_FF // FF_CHUNK):
        sl = slice(c * FF_CHUNK, (c + 1) * FF_CHUNK)
        h1 = _dot(xb, w1_ref[:, sl])
        h3 = _dot(xb, w3_ref[:, sl])
        part = _dot((h1 * _sigmoid(h1) * h3).astype(BF16), w2_ref[sl, :])
        acc = part if acc is None else acc + part
    return _layer_norm(ALPHA * x + 0.5 * acc, g_ref[...], b_ref[...])


def _proj_mix(x, h_ref, w_ref, g_ref, b_ref):
    return _layer_norm(ALPHA * x + _dot(h_ref[...], w_ref[...]), g_ref[...], b_ref[...])


def _chain_body(*refs, n_ffn, mix_fn, n_mix):
    x = refs[0][...]
    pos = 1
    if mix_fn is not None:
        x = mix_fn(x, *refs[1:1 + n_mix])
        pos += n_mix
    for _ in range(n_ffn):
        x = _ffn_stage(x, *refs[pos:pos + 5])
        pos += 5
    refs[pos][...] = x


def _chain(x, stacks, ffn_ids, mix=None):
    w1_all, w3_all, w2_all, g_all, b_all = stacks
    rows, consts, mix_fn, n_mix = [x], [], None, 0
    if mix is not None:
        mix_fn, mix_rows, mix_consts = mix
        rows += mix_rows
        consts += mix_consts
        n_mix = len(mix_rows) + len(mix_consts)
    for wi, ni in ffn_ids:
        consts += [(w1_all, wi), (w3_all, wi), (w2_all, wi), (g_all, ni), (b_all, ni)]
    body = functools.partial(_chain_body, n_ffn=len(ffn_ids), mix_fn=mix_fn, n_mix=n_mix)
    name = ("mix_" if mix is not None else "") + "ffn" * len(ffn_ids)
    return _rows_call(body, name, rows, consts, [(D_MODEL, F32)])[0]


def _mla_proj_body(x_ref, ct_ref, s1_ref, s2_ref, wdq_ref, qn_ref, wq_ref, wdkv_ref, kvn_ref, wk_ref, wv_ref,
                   q_ref, k_ref, v_ref, c_ref, kpe_ref):
    xb = x_ref[...].astype(BF16)
    ct, s1, s2 = ct_ref[...], s1_ref[...], s2_ref[...]
    dq = _dot(xb, wdq_ref[...])
    ckv = _dot(xb, wdkv_ref[...])
    cq = _rms_norm(dq, qn_ref[...])
    c = _rms_norm(ckv[:, :KV_RANK], kvn_ref[...])
    q = _dot(cq.astype(BF16), wq_ref[...])
    half = ROPE_DIM // 2
    k1, k2 = ckv[:, KV_RANK:KV_RANK + half], ckv[:, KV_RANK + half:]
    cos, sin = ct[:, NOPE_DIM:NOPE_DIM + half], s1[:, NOPE_DIM + half:QK_DIM]
    kpe = jnp.concatenate([k1 * cos - k2 * sin, k1 * sin + k2 * cos], -1)
    c_ref[...] = c
    kpe_ref[...] = kpe
    cb = c.astype(BF16)
    knope = _dot(cb, wk_ref[...])
    v_ref[...] = _dot(cb, wv_ref[...]).astype(BF16)
    tm = kpe.shape[0]
    kpe_slot = jnp.concatenate([jnp.zeros((tm, NOPE_DIM), F32), kpe, jnp.zeros((tm, HEAD_PAD - QK_DIM), F32)], -1)
    scale = QK_DIM ** -0.5 * math.log2(math.e)
    for h in range(MLA_HEADS):
        sl = slice(h * HEAD_PAD, (h + 1) * HEAD_PAD)
        blk = q[:, sl]
        rot = blk * ct + pltpu.roll(blk, half, 1) * s1 + pltpu.roll(blk, HEAD_PAD - half, 1) * s2
        q_ref[:, sl] = (rot * scale).astype(BF16)
        k_ref[:, sl] = (knope[:, sl] + kpe_slot).astype(BF16)


def _flash_body(qt_ref, kt_ref, *refs, tq, tk, sub, ahead, has_lead):
    if has_lead:
        q_ref, k_ref, v_ref, mk_ref, mv_ref, o_ref, m_sc, acc_sc = refs
    else:
        q_ref, k_ref, v_ref, o_ref, m_sc, acc_sc = refs
    t = pl.program_id(2)
    qi, ki = qt_ref[t], kt_ref[t]
    blocks = [(hh, r0) for r0 in range(0, tq, sub) for hh in range(2)]

    def q_blk(hh, r0):
        return q_ref[r0:r0 + sub, hh * HEAD_PAD:(hh + 1) * HEAD_PAD]

    def weights(hh, r0, s, first):
        rows = slice(r0, r0 + sub)
        width = s.shape[1]
        m_cur = jnp.max(s, -1, keepdims=True)
        if first:
            m_new = jnp.broadcast_to(m_cur, (sub, LANES))
        else:
            m_old = m_sc[hh, rows, :]
            m_new = jnp.maximum(m_old, m_cur)
        m_wide = jnp.concatenate([m_new] * (width // LANES), 1) if width >= LANES else m_new[:, :width]
        p = jnp.exp2(s - m_wide)
        corr = None if first else jnp.exp2(m_old - m_new)
        m_sc[hh, rows, :] = m_new
        return p.astype(BF16), corr

    def attend(score_fn, vals, first=False):
        lane = lax.broadcasted_iota(jnp.int32, vals.shape, 1)
        one = jnp.ones(vals.shape, BF16)
        vals_h = [jnp.where(lane < V_DIM, vals, one), jnp.where(lane < V_DIM, one, vals)]
        scores = {}
        for i in range(len(blocks) + ahead):
            if i < len(blocks):
                scores[i] = score_fn(*blocks[i])
            if i >= ahead:
                hh, r0 = blocks[i - ahead]
                p, corr = weights(hh, r0, scores.pop(i - ahead), first)
                rows = slice(r0, r0 + sub)
                pv = _dot(p, vals_h[hh][:p.shape[1]])
                acc_sc[hh, rows, :] = pv if first else corr * acc_sc[hh, rows, :] + pv

    @pl.when(ki == 0)
    def _init():
        if has_lead:
            attend(lambda hh, r0: _dot_nt(q_blk(hh, r0), mk_ref[:, hh * HEAD_PAD:(hh + 1) * HEAD_PAD]),
                   mv_ref[...], first=True)
        else:
            m_sc[...] = jnp.full(m_sc.shape, NEG, F32)
            acc_sc[...] = jnp.zeros(acc_sc.shape, F32)

    def step(masked):
        def score(hh, r0):
            kw = r0 + sub if masked else tk
            s = _dot_nt(q_blk(hh, r0), k_ref[:kw, hh * HEAD_PAD:(hh + 1) * HEAD_PAD])
            if masked:
                row = r0 + lax.broadcasted_iota(jnp.int32, (sub, kw), 0)
                col = lax.broadcasted_iota(jnp.int32, (sub, kw), 1)
                s = jnp.where(col <= row, s, NEG)
            return s
        attend(score, v_ref[...])

    @pl.when(ki < qi)
    def _below():
        step(False)

    @pl.when(ki == qi)
    def _diag():
        step(True)
        lane = lax.broadcasted_iota(jnp.int32, (tq, 2 * V_DIM), 1)
        a0, a1 = acc_sc[0], acc_sc[1]
        o0 = a0 / pltpu.roll(a0, V_DIM, 1)
        o1 = a1 / pltpu.roll(a1, V_DIM, 1)
        o_ref[...] = jnp.where(lane < V_DIM, o0, o1).astype(BF16)


def _flash(q, k, v, n_seq, seq_len, row_off, tile, lead_off=None):
    tq = tk = tile
    nq = seq_len // tq
    qo = row_off // tq
    has_lead = lead_off is not None
    pairs = [(i, j) for i in range(nq) for j in range(i + 1)]
    qt = jnp.array([i for i, _ in pairs], jnp.int32)
    kt = jnp.array([j for _, j in pairs], jnp.int32)
    in_specs = [
        pl.BlockSpec((tq, 2 * HEAD_PAD), lambda b, h, t, qt, kt: (qo + b * nq + qt[t], h)),
        pl.BlockSpec((tk, 2 * HEAD_PAD), lambda b, h, t, qt, kt: (qo + b * nq + kt[t], h)),
        pl.BlockSpec((tk, 2 * V_DIM), lambda b, h, t, qt, kt: (qo + b * nq + kt[t], h)),
    ]
    args = [q, k, v]
    if has_lead:
        lo = lead_off // N_META
        in_specs += [pl.BlockSpec((N_META, 2 * HEAD_PAD), lambda b, h, t, qt, kt: (lo + b, h)),
                     pl.BlockSpec((N_META, 2 * V_DIM), lambda b, h, t, qt, kt: (lo + b, h))]
        args += [k, v]
    return pl.pallas_call(
        functools.partial(_flash_body, tq=tq, tk=tk, sub=min(ATTN_SUB, tq), ahead=ATTN_AHEAD, has_lead=has_lead),
        grid_spec=pltpu.PrefetchScalarGridSpec(
            num_scalar_prefetch=2, grid=(n_seq, MLA_HEADS // 2, len(pairs)),
            in_specs=in_specs,
            out_specs=pl.BlockSpec((tq, 2 * V_DIM), lambda b, h, t, qt, kt: (b * nq + qt[t], h)),
            scratch_shapes=[pltpu.VMEM((2, tq, LANES), F32), pltpu.VMEM((2, tq, 2 * V_DIM), F32)]),
        out_shape=jax.ShapeDtypeStruct((n_seq * seq_len, MLA_HEADS * V_DIM), BF16),
        compiler_params=_params("parallel", "parallel", "arbitrary"),
        name="mla_flash_lead" if has_lead else "mla_flash")(qt, kt, *args)


def _qlat_body(q_ref, w_ref, o_ref):
    o_ref[0] = _dot(q_ref[...], w_ref[0]).astype(BF16)


def _decode_body(pt_ref, ql_ref, qp_ref, *refs, t_new, n_pg):
    lat_refs, kr_refs = refs[:n_pg], refs[n_pg:2 * n_pg]
    cn_ref, kn_ref, o_ref, m_sc, l_sc, acc_sc = refs[2 * n_pg:]
    j = pl.program_id(1)
    ql, qp = ql_ref[0], qp_ref[0]

    @pl.when(j == 0)
    def _init():
        m_sc[...] = jnp.full(m_sc.shape, NEG, F32)
        l_sc[...] = jnp.zeros(l_sc.shape, F32)
        acc_sc[...] = jnp.zeros(acc_sc.shape, F32)

    def update(s, vals):
        m_old = m_sc[...]
        m_new = jnp.maximum(m_old, jnp.max(s, -1, keepdims=True))
        p = jnp.exp2(s - m_new)
        corr = jnp.exp2(m_old - m_new)
        l_sc[...] = corr * l_sc[...] + jnp.sum(p, -1, keepdims=True)
        acc_sc[...] = corr * acc_sc[...] + _dot(p.astype(BF16), vals)
        m_sc[...] = m_new

    lat = jnp.concatenate([r[0, 0].astype(BF16) for r in lat_refs], 0)
    kr_t = jnp.concatenate([r[0, 0].astype(BF16) for r in kr_refs], 1)
    update(_dot_nt(ql, lat) + _dot(qp, kr_t), lat)

    @pl.when(j == pl.num_programs(1) - 1)
    def _fin():
        cn = cn_ref[0].astype(BF16)
        kn = kn_ref[0].astype(BF16)
        s = _dot_nt(ql, cn) + _dot_nt(qp, kn)
        rows = s.shape[0]
        t_q = lax.rem(lax.broadcasted_iota(jnp.int32, (rows, t_new), 0), t_new)
        t_k = lax.broadcasted_iota(jnp.int32, (rows, t_new), 1)
        update(jnp.where(t_k <= t_q, s, NEG), cn)
        o_ref[0] = (acc_sc[...] / l_sc[...]).astype(BF16)


def _ouv_body(o_ref, w_ref, out_ref):
    out_ref[...] = _dot(o_ref[...], w_ref[0]).astype(BF16)


def _rope_tables(pos):
    half = ROPE_DIM // 2
    inv = 1.0 / (ROPE_THETA ** (jnp.arange(half, dtype=F32) * (2.0 / ROPE_DIM)))
    ang = pos.astype(F32)[:, None] * inv
    c, s = jnp.cos(ang), jnp.sin(ang)
    n = pos.shape[0]
    one, zero = jnp.ones((n, NOPE_DIM), F32), jnp.zeros((n, NOPE_DIM), F32)
    pad = jnp.zeros((n, HEAD_PAD - QK_DIM), F32)
    z16 = jnp.zeros((n, half), F32)
    ct = jnp.concatenate([one, c, c, pad], 1)
    s1 = jnp.concatenate([zero, z16, s, pad], 1)
    s2 = jnp.concatenate([zero, -s, z16, pad], 1)
    return ct, s1, s2


def _mla_weights(m, w_dq, q_norm, w_uq, w_dkv, kv_norm, w_uk, w_uv, w_o):
    wq = jnp.pad(w_uq[m], ((0, 0), (0, 0), (0, HEAD_PAD - QK_DIM))).reshape(Q_RANK, MLA_HEADS * HEAD_PAD)
    wk = jnp.pad(w_uk[m], ((0, 0), (0, 0), (0, HEAD_PAD - NOPE_DIM))).reshape(KV_RANK, MLA_HEADS * HEAD_PAD)
    wv = w_uv[m].reshape(KV_RANK, MLA_HEADS * V_DIM)
    wukt = jnp.pad(jnp.transpose(w_uk[m], (1, 2, 0)), ((0, 0), (0, HEAD_PAD - NOPE_DIM), (0, 0)))
    wuv_h = jnp.transpose(w_uv[m], (1, 0, 2)).reshape(MLA_HEADS // 2, 2, KV_RANK, V_DIM)
    eye2 = jnp.eye(2, dtype=F32)
    wuv_pair = jnp.einsum('pirv,ij->pirjv', wuv_h, eye2).reshape(MLA_HEADS // 2, 2 * KV_RANK, 2 * V_DIM)
    return dict(wdq=w_dq[m].astype(BF16), qn=q_norm[m][None], wq=wq.astype(BF16), wdkv=w_dkv[m].astype(BF16),
                kvn=kv_norm[m][None], wk=wk.astype(BF16), wv=wv.astype(BF16), wukt=wukt.astype(BF16),
                wuv_pair=wuv_pair.astype(BF16), wo=w_o[m].reshape(MLA_HEADS * V_DIM, D_MODEL).astype(BF16))


def _mla_layer(x, lay, tabs, w, m, lat_pool, kr_pool, page_table):
    n_seq, seq, n_dec, t_dec = lay["B"], lay["S"], lay["Bd"], lay["Td"]
    q, k, v, c, kpe = _rows_call(
        _mla_proj_body, "mla_proj", [x, *tabs],
        [w["wdq"], w["qn"], w["wq"], w["wdkv"], w["kvn"], w["wk"], w["wv"]],
        [(MLA_HEADS * HEAD_PAD, BF16), (MLA_HEADS * HEAD_PAD, BF16), (MLA_HEADS * V_DIM, BF16),
         (KV_RANK, F32), (ROPE_DIM, F32)])
    soff, moff = lay["soff"], lay["moff"]
    o_real = _flash(q, k, v, n_seq, seq, 0, min(ATTN_TILE, seq), lead_off=moff)
    o_meta = _flash(q, k, v, n_seq, N_META, moff, N_META)

    n_rows = n_dec * t_dec
    hq = MLA_HEADS * t_dec
    q_lat = pl.pallas_call(
        _qlat_body, grid=(MLA_HEADS,),
        in_specs=[pl.BlockSpec((n_rows, HEAD_PAD), lambda h: (soff // n_rows, h)),
                  pl.BlockSpec((1, HEAD_PAD, KV_RANK), lambda h: (h, 0, 0))],
        out_specs=pl.BlockSpec((1, n_rows, KV_RANK), lambda h: (h, 0, 0)),
        out_shape=jax.ShapeDtypeStruct((MLA_HEADS, n_rows, KV_RANK), BF16),
        compiler_params=_params("parallel"), name="mla_qlat")(q, w["wukt"])
    q_lat = q_lat.reshape(MLA_HEADS, n_dec, t_dec, KV_RANK).transpose(1, 0, 2, 3).reshape(n_dec, hq, KV_RANK)
    q_s = q[soff:soff + n_rows].reshape(n_dec, t_dec, MLA_HEADS, HEAD_PAD)[..., NOPE_DIM:QK_DIM]
    q_pe = q_s.transpose(0, 2, 1, 3).reshape(n_dec, hq, ROPE_DIM)
    c_new = c[soff:soff + n_rows].reshape(n_dec, t_dec, KV_RANK)
    k_new = kpe[soff:soff + n_rows].reshape(n_dec, t_dec, ROPE_DIM)
    n_pages, page = page_table.shape[1], lat_pool.shape[2]
    n_pg = math.gcd(DECODE_PAGES, n_pages)
    kr_pool_t = jnp.swapaxes(kr_pool, 2, 3)
    lat_spec = lambda i: pl.BlockSpec((1, 1, page, KV_RANK), lambda bb, j, pt: (m, pt[bb, j * n_pg + i], 0, 0))
    kr_spec = lambda i: pl.BlockSpec((1, 1, ROPE_DIM, page), lambda bb, j, pt: (m, pt[bb, j * n_pg + i], 0, 0))
    o_lat = pl.pallas_call(
        functools.partial(_decode_body, t_new=t_dec, n_pg=n_pg),
        grid_spec=pltpu.PrefetchScalarGridSpec(
            num_scalar_prefetch=1, grid=(n_dec, n_pages // n_pg),
            in_specs=[pl.BlockSpec((1, hq, KV_RANK), lambda bb, j, pt: (bb, 0, 0)),
                      pl.BlockSpec((1, hq, ROPE_DIM), lambda bb, j, pt: (bb, 0, 0))]
            + [lat_spec(i) for i in range(n_pg)] + [kr_spec(i) for i in range(n_pg)]
            + [pl.BlockSpec((1, t_dec, KV_RANK), lambda bb, j, pt: (bb, 0, 0)),
               pl.BlockSpec((1, t_dec, ROPE_DIM), lambda bb, j, pt: (bb, 0, 0))],
            out_specs=pl.BlockSpec((1, hq, KV_RANK), lambda bb, j, pt: (bb, 0, 0)),
            scratch_shapes=[pltpu.VMEM((hq, 1), F32), pltpu.VMEM((hq, 1), F32), pltpu.VMEM((hq, KV_RANK), F32)]),
        out_shape=jax.ShapeDtypeStruct((n_dec, hq, KV_RANK), BF16),
        compiler_params=_params("parallel", "arbitrary"), name="mla_decode")(
            page_table, q_lat, q_pe, *([lat_pool] * n_pg), *([kr_pool_t] * n_pg), c_new, k_new)
    o_lat = o_lat.reshape(n_dec, MLA_HEADS, t_dec, KV_RANK).transpose(0, 2, 1, 3).reshape(n_rows, MLA_HEADS * KV_RANK)
    o_samp = pl.pallas_call(
        _ouv_body, grid=(MLA_HEADS // 2,),
        in_specs=[pl.BlockSpec((n_rows, 2 * KV_RANK), lambda h: (0, h)),
                  pl.BlockSpec((1, 2 * KV_RANK, 2 * V_DIM), lambda h: (h, 0, 0))],
        out_specs=pl.BlockSpec((n_rows, 2 * V_DIM), lambda h: (0, h)),
        out_shape=jax.ShapeDtypeStruct((n_rows, MLA_HEADS * V_DIM), BF16),
        compiler_params=_params("parallel"), name="mla_ouv")(o_lat, w["wuv_pair"])

    return _assemble(lay, o_real, o_samp, o_meta), c, kpe


def _split_dot(a, sel):
    hi = a.astype(BF16)
    lo = (a - hi.astype(F32)).astype(BF16)
    return _dot(hi, sel) + _dot(lo, sel)


def _head_sum(x, gs, gb):
    return _split_dot(_split_dot(x, gs), gb)


def _softplus(z):
    return jnp.maximum(z, 0.0) + jnp.log(1.0 + jnp.exp(-jnp.abs(z)))


def _rw_proj_body(x_ref, xp_ref, mu_ref, wr_ref, wk_ref, wv_ref, w0_ref, w1_ref, w2_ref, a0_ref, a1_ref, a2_ref,
                  g1_ref, g2_ref, kk_ref, ka_ref, rk_ref, gs_ref, gb_ref,
                  r_o, d_o, k_o, v_o, kk_o, a_o, g_o, bonus_o):
    x = x_ref[...]
    xx = xp_ref[...] - x
    mix = lambda j: (x + xx * mu_ref[j:j + 1, :]).astype(BF16)
    w_lo = _dot(mix(1), w1_ref[...])
    a_lo = _dot(mix(4), a1_ref[...])
    g_lo = _dot(mix(5), g1_ref[...])
    r = _dot(mix(0), wr_ref[...])
    k = _dot(mix(2), wk_ref[...])
    v = _dot(mix(3), wv_ref[...])
    wl = w0_ref[...] + _dot(jnp.tanh(w_lo).astype(BF16), w2_ref[...])
    w = -_softplus(-wl) - 0.5
    a = _sigmoid(a0_ref[...] + _dot(a_lo.astype(BF16), a2_ref[...]))
    g = _dot(_sigmoid(g_lo).astype(BF16), g2_ref[...])
    gs, gb = gs_ref[...], gb_ref[...]
    kk = k * kk_ref[...]
    kk = kk * lax.rsqrt(jnp.maximum(_head_sum(kk * kk, gs, gb), 1e-24))
    k = k * (1.0 + (a - 1.0) * ka_ref[...])
    r_o[...] = r
    d_o[...] = -jnp.exp(w)
    k_o[...] = k
    v_o[...] = v
    kk_o[...] = kk
    a_o[...] = a
    g_o[...] = g
    bonus_o[...] = _head_sum(r * k * rk_ref[...], gs, gb) * v


def _rw_chunk_body(*refs, chunk, n_fac, group):
    s0_ref, y_ref, st_ref, s_sc = refs[6 * group:]
    load = lambda i: jnp.concatenate([refs[i * group + g][...] for g in range(group)], 0)
    ci = pl.program_id(1)

    @pl.when(ci == 0)
    def _init():
        s_sc[...] = s0_ref[...]

    row = lax.broadcasted_iota(jnp.int32, (chunk, chunk), 0)
    col = lax.broadcasted_iota(jnp.int32, (chunk, chunk), 1)
    strict, incl = col < row, col <= row
    rows = group * chunk
    if group == 1:
        same, below = None, incl
    else:
        rr = lax.broadcasted_iota(jnp.int32, (rows, rows), 0)
        cc = lax.broadcasted_iota(jnp.int32, (rows, rows), 1)
        same = (rr // chunk) == (cc // chunk)
        below = jnp.logical_and(same, cc <= rr)
    d_all, k_all, kk_all = load(1), load(2), load(4)
    cum = _dot(below.astype(F32), d_all, HI)
    if group == 1:
        cum_end = cum[chunk - 1:chunk, :]
    else:
        cum_end = _dot(same.astype(F32), d_all, HI)
    be_all = kk_all * load(5)
    w_inv = jnp.exp(-cum)
    w_rem = jnp.exp(cum_end - cum)
    al_t_all = (-kk_all * jnp.exp(cum - d_all)).astype(BF16)
    r_t_all = load(0) * jnp.exp(cum)
    be_h_all = (be_all * w_inv).astype(BF16)
    k_h_all = (k_all * w_inv).astype(BF16)
    be_e_all = (be_all * w_rem).astype(BF16)
    k_e_all = (k_all * w_rem).astype(BF16)
    w_end = jnp.exp(cum_end)
    v_all = load(3).astype(BF16)
    zero = jnp.zeros((chunk, RW_HEAD), BF16)
    chains = [(g, h) for g in range(group) for h in range(RW_HEADS)]
    ids = range(len(chains))
    cut = lambda a, g, h: a[g * chunk:(g + 1) * chunk, h * RW_HEAD:(h + 1) * RW_HEAD]
    gram = [_dot_nt(jnp.concatenate([cut(al_t_all, *c), cut(r_t_all, *c).astype(BF16)], 0),
                    jnp.concatenate([cut(be_h_all, *c), cut(k_h_all, *c)], 0)) for c in chains]
    l_ab = [jnp.where(strict, g[:chunk, :chunk], 0.0).astype(BF16) for g in gram]
    l_ak = [jnp.where(strict, g[:chunk, chunk:], 0.0).astype(BF16) for g in gram]
    m_r = [jnp.concatenate([jnp.where(incl, g[chunk:, :chunk], 0.0),
                            jnp.where(incl, g[chunk:, chunk:], 0.0)], 1).astype(BF16) for g in gram]
    u = [jnp.concatenate([cut(al_t_all, *c).astype(F32), _dot(l_ak[i], cut(v_all, *c))], 1) for i, c in zip(ids, chains)]
    lp = l_ab
    for f in range(n_fac):
        u = [u[i] + _dot(lp[i], u[i].astype(BF16)) for i in ids]
        if f + 1 < n_fac:
            lp = [_dot(lp[i], lp[i]).astype(BF16) for i in ids]
    z = [jnp.concatenate([u[i].astype(BF16), jnp.concatenate([zero, cut(v_all, *c)], 1)], 0) for i, c in zip(ids, chains)]
    g = [_dot(m_r[i], z[i]) for i in ids]
    ad = [_dot_tn(z[i], jnp.concatenate([cut(be_e_all, *c), cut(k_e_all, *c)], 0)) for i, c in zip(ids, chains)]
    s_old = [s_sc[c[0], c[1]] for c in chains]
    s_bf = [s.astype(BF16) for s in s_old]
    y = [_dot_nt((cut(r_t_all, *c) + g[i][:, :RW_HEAD]).astype(BF16), s_bf[i]) + g[i][:, RW_HEAD:]
         for i, c in zip(ids, chains)]
    for gi in range(group):
        y_ref[gi] = jnp.concatenate(y[gi * RW_HEADS:(gi + 1) * RW_HEADS], 1)
    for i, (gi, h) in zip(ids, chains):
        w_h = w_end[gi * chunk:gi * chunk + 1, h * RW_HEAD:(h + 1) * RW_HEAD]
        s_sc[gi, h] = s_old[i] * w_h + _dot(s_bf[i], ad[i][:RW_HEAD].astype(BF16)) + ad[i][RW_HEAD:]

    @pl.when(ci == pl.num_programs(1) - 1)
    def _fin():
        st_ref[...] = s_sc[...]


def _rw_scan(seqs, s0, n_seq, seq_len, row_off, chunk, group=1):
    nc = seq_len // chunk
    assert n_seq % group == 0
    off = row_off // chunk
    n_fac = max(1, math.ceil(math.log2(chunk)))
    row_spec = lambda g: pl.BlockSpec((chunk, D_MODEL), lambda b, c: (off + (b * group + g) * nc + c, 0))
    st_spec = pl.BlockSpec((group, RW_HEADS, RW_HEAD, RW_HEAD), lambda b, c: (b, 0, 0, 0))
    y, st = pl.pallas_call(
        functools.partial(_rw_chunk_body, chunk=chunk, n_fac=n_fac, group=group),
        grid=(n_seq // group, nc),
        in_specs=[row_spec(g) for _ in range(6) for g in range(group)] + [st_spec],
        out_specs=[pl.BlockSpec((group, chunk, D_MODEL), lambda b, c: (b, c, 0)), st_spec],
        out_shape=[jax.ShapeDtypeStruct((n_seq, seq_len, D_MODEL), F32),
                   jax.ShapeDtypeStruct((n_seq, RW_HEADS, RW_HEAD, RW_HEAD), F32)],
        scratch_shapes=[pltpu.VMEM((group, RW_HEADS, RW_HEAD, RW_HEAD), F32)],
        compiler_params=_params("parallel", "arbitrary"), name=f"rwkv_chunk{chunk}")(
            *[a for a in seqs for _ in range(group)], s0)
    return y.reshape(n_seq * seq_len, D_MODEL), st


def _rw_mix(x, y_ref, g_ref, bonus_ref, lg_ref, lb_ref, gs_ref, gb_ref, wo_ref, ng_ref, nb_ref):
    y = y_ref[...]
    gs, gb = gs_ref[...], gb_ref[...]
    mean = _head_sum(y, gs, gb) * (1.0 / RW_HEAD)
    yc = y - mean
    var = _head_sum(yc * yc, gs, gb) * (1.0 / RW_HEAD)
    yn = yc * lax.rsqrt(var + GN_EPS) * lg_ref[...] + lb_ref[...]
    z = ((yn + bonus_ref[...]) * g_ref[...]).astype(BF16)
    return _layer_norm(ALPHA * x + _dot(z, wo_ref[...]), ng_ref[...], nb_ref[...])


def _rwkv_layer(x, lay, m, p, shift_s, wkv_s, g, b):
    n_seq, seq, n_dec, t_dec = lay["B"], lay["S"], lay["Bd"], lay["Td"]
    soff, moff = lay["soff"], lay["moff"]
    n_rows = n_dec * t_dec
    xm = x[moff:moff + n_seq * N_META].reshape(n_seq, N_META, D_MODEL)
    xs = x[soff:soff + n_rows].reshape(n_dec, t_dec, D_MODEL)
    prev_s = jnp.concatenate([shift_s[m][:, None], xs[:, :-1]], 1).reshape(n_rows, D_MODEL)
    xprev = jnp.concatenate([jnp.zeros((1, D_MODEL), F32), x[:-1]], 0)
    xprev = lax.dynamic_update_slice(xprev, prev_s, (soff, 0))
    for bi in range(n_seq):
        xprev = lax.dynamic_update_slice(xprev, xm[bi, -1:], (bi * seq, 0))
        xprev = lax.dynamic_update_slice(xprev, jnp.zeros((1, D_MODEL), F32), (moff + bi * N_META, 0))
    x_last = jnp.stack([x[(bi + 1) * seq - 1] for bi in range(n_seq)])

    row = lambda a: a[m][None]
    gs = jnp.pad(jnp.repeat(jnp.eye(RW_HEADS, dtype=BF16), RW_HEAD, 0), ((0, 0), (0, LANES - RW_HEADS)))
    gb = gs.T
    bf = lambda a: a[m].astype(BF16)
    r, d, k, v, kk, a, gate, bonus = _rows_call(
        _rw_proj_body, "rwkv_proj", [x, xprev],
        [p['rw_mu'][m], bf(p['rw_wr']), bf(p['rw_wk']), bf(p['rw_wv']), row(p['rw_w0']), bf(p['rw_w1']),
         bf(p['rw_w2']), row(p['rw_a0']), bf(p['rw_a1']), bf(p['rw_a2']), bf(p['rw_g1']), bf(p['rw_g2']),
         row(p['rw_k_k']), row(p['rw_k_a']), p['rw_r_k'][m].reshape(1, D_MODEL), gs, gb],
        [(D_MODEL, F32)] * 8)
    seqs = (r, d, k, v, kk, a)
    zero_state = jnp.zeros((n_seq, RW_HEADS, RW_HEAD, RW_HEAD), F32)
    y_meta, st = _rw_scan(seqs, zero_state, n_seq, N_META, moff, N_META, group=n_seq)
    y_real, st_p = _rw_scan(seqs, st, n_seq, seq, 0, min(RW_CHUNK, seq), group=n_seq)
    t_pad = 8
    pad_rows = lambda z: jnp.pad(z[soff:soff + n_rows].reshape(n_dec, t_dec, D_MODEL),
                                 ((0, 0), (0, t_pad - t_dec), (0, 0))).reshape(n_dec * t_pad, D_MODEL)
    y_s, st_s = _rw_scan(tuple(pad_rows(z) for z in seqs), wkv_s[m], n_dec, t_pad, 0, t_pad,
                         group=math.gcd(RW_GROUP, n_dec))
    y_s = y_s.reshape(n_dec, t_pad, D_MODEL)[:, :t_dec].reshape(n_rows, D_MODEL)
    y_all = _assemble(lay, y_real, y_s, y_meta)
    mix = (_rw_mix, [y_all, gate, bonus], [row(p['rw_lnx_g']), row(p['rw_lnx_b']), gs, gb, bf(p['rw_wo']), g, b])
    return mix, st_p, x_last, st_s, xs[:, -1]


def _s5_disc_body(lre_ref, lim_ref, dt_ref, bre_ref, bim_ref, lre_s, lim_s, dt_s, bbr_o, bbi_o, ar_o, ai_o):
    def disc(lre, lim, dt):
        mag = jnp.exp(lre * dt)
        ab_re, ab_im = mag * jnp.cos(lim * dt), mag * jnp.sin(lim * dt)
        den = lre * lre + lim * lim
        nr = ab_re - 1.0
        return ab_re, ab_im, (nr * lre + ab_im * lim) / den, (ab_im * lre - nr * lim) / den

    _, _, co_re, co_im = disc(lre_ref[...], lim_ref[...], dt_ref[...])
    b_re, b_im = bre_ref[...], bim_ref[...]
    bbr_o[...] = co_re * b_re - co_im * b_im
    bbi_o[...] = co_re * b_im + co_im * b_re
    ab_re, ab_im, _, _ = disc(lre_s[...], lim_s[...], dt_s[...])
    ar_o[...] = ab_re
    ai_o[...] = ab_im


def _s5_input(u, bre_ref, bim_ref):
    ub = u.astype(BF16)
    cw = D_MODEL // S5_BLOCKS
    re = [_dot(ub[:, j * cw:(j + 1) * cw], bre_ref[j]) for j in range(S5_BLOCKS)]
    im = [_dot(ub[:, j * cw:(j + 1) * cw], bim_ref[j]) for j in range(S5_BLOCKS)]
    return re, im


def _s5_output(u, hre, him, cre_ref, cim_ref, dv_ref, wv_ref, wg_ref, g_ref, b_ref):
    w = S5_WIDTH // S5_BLOCKS
    hb_re, hb_im = hre.astype(BF16), him.astype(BF16)
    ys = [_dot(hb_re[:, j * w:(j + 1) * w], cre_ref[j]) - _dot(hb_im[:, j * w:(j + 1) * w], cim_ref[j])
          for j in range(S5_BLOCKS)]
    y = jnp.concatenate(ys, 1) + dv_ref[...] * u
    z = jax.nn.gelu(y).astype(BF16)
    out = _dot(z, wv_ref[...]) * _sigmoid(_dot(z, wg_ref[...]))
    return _layer_norm(ALPHA * u + out, g_ref[...], b_ref[...])


def _s5_seq_body(*refs, tt, group):
    x_refs = refs[:group]
    (h0r_ref, h0i_ref, ar_ref, ai_ref, bre_ref, bim_ref, cre_ref, cim_ref, dv_ref, wv_ref, wg_ref,
     g_ref, b_ref, o_ref, htr_ref, hti_ref, hre_sc, him_sc, sr_sc, si_sc) = refs[group:]
    ti = pl.program_id(0)

    @pl.when(ti == 0)
    def _init():
        sr_sc[...] = h0r_ref[...]
        si_sc[...] = h0i_ref[...]

    w = S5_WIDTH // S5_BLOCKS
    us = [r[...] for r in x_refs]
    for s, u in enumerate(us):
        re, im = _s5_input(u, bre_ref, bim_ref)
        for j in range(S5_BLOCKS):
            hre_sc[s, :, j * w:(j + 1) * w] = re[j]
            him_sc[s, :, j * w:(j + 1) * w] = im[j]
    ar, ai = ar_ref[...], ai_ref[...]

    def step(t, carry):
        out = []
        for s in range(group):
            hr, hi = carry[2 * s], carry[2 * s + 1]
            nr = ar * hr - ai * hi + hre_sc[s, pl.ds(t, 1), :]
            ni = ar * hi + ai * hr + him_sc[s, pl.ds(t, 1), :]
            hre_sc[s, pl.ds(t, 1), :] = nr
            him_sc[s, pl.ds(t, 1), :] = ni
            out += [nr, ni]
        return tuple(out)

    init = tuple(v for s in range(group) for v in (sr_sc[s], si_sc[s]))
    fin = lax.fori_loop(0, tt, step, init, unroll=S5_UNROLL)
    for s, u in enumerate(us):
        sr_sc[s] = fin[2 * s]
        si_sc[s] = fin[2 * s + 1]
        o_ref[s] = _s5_output(u, hre_sc[s], him_sc[s], cre_ref, cim_ref, dv_ref, wv_ref, wg_ref, g_ref, b_ref)

    @pl.when(ti == pl.num_programs(0) - 1)
    def _fin():
        htr_ref[...] = sr_sc[...]
        hti_ref[...] = si_sc[...]


def _s5_step_body(x_ref, h0r_ref, h0i_ref, ar_ref, ai_ref, bre_ref, bim_ref, cre_ref, cim_ref, dv_ref, wv_ref, wg_ref,
                  g_ref, b_ref, o_ref, htr_ref, hti_ref, sr_sc, si_sc):
    t = pl.program_id(0)

    @pl.when(t == 0)
    def _init():
        sr_sc[...] = h0r_ref[...]
        si_sc[...] = h0i_ref[...]

    u = x_ref[0]
    re, im = _s5_input(u, bre_ref, bim_ref)
    bur, bui = jnp.concatenate(re, 1), jnp.concatenate(im, 1)
    ar, ai = ar_ref[...], ai_ref[...]
    hr, hi = sr_sc[...], si_sc[...]
    nr = ar * hr - ai * hi + bur
    ni = ar * hi + ai * hr + bui
    sr_sc[...] = nr
    si_sc[...] = ni
    o_ref[0] = _s5_output(u, nr, ni, cre_ref, cim_ref, dv_ref, wv_ref, wg_ref, g_ref, b_ref)

    @pl.when(t == pl.num_programs(0) - 1)
    def _fin():
        htr_ref[...] = nr
        hti_ref[...] = ni


def _s5_weights(m, p):
    rep = lambda a: jnp.repeat(a, S5_GROUP, axis=-1)
    lre, lim = p['s5_lam_re'][m], p['s5_lam_im'][m]
    dt = jnp.broadcast_to(jnp.exp(p['s5_log_dt'][m])[:, None], (S5_GROUPS, S5_STATE))
    flat = lambda a: a[m].reshape(S5_GROUPS, S5_STATE * S5_GROUP)
    n = S5_STATE * S5_GROUP
    bb_re, bb_im, ab_re, ab_im = pl.pallas_call(
        _s5_disc_body,
        out_shape=[jax.ShapeDtypeStruct((S5_GROUPS, n), F32)] * 2 + [jax.ShapeDtypeStruct((S5_GROUPS, S5_STATE), F32)] * 2,
        name="s5_disc")(rep(lre), rep(lim), rep(dt), flat(p['s5_b_re']), flat(p['s5_b_im']), lre, lim, dt)
    gpb = S5_GROUPS // S5_BLOCKS
    eye = jnp.eye(gpb, dtype=F32)

    def in_blocks(bb):
        t = bb.reshape(S5_BLOCKS, gpb, S5_STATE, S5_GROUP).transpose(0, 1, 3, 2)
        return jnp.einsum('jgsp,gh->jgshp', t, eye).reshape(S5_BLOCKS, gpb * S5_GROUP, gpb * S5_STATE).astype(BF16)

    def out_blocks(c):
        t = c.reshape(S5_BLOCKS, gpb, S5_GROUP, S5_STATE).transpose(0, 1, 3, 2)
        return jnp.einsum('jgps,gh->jgphs', t, eye).reshape(S5_BLOCKS, gpb * S5_STATE, gpb * S5_GROUP).astype(BF16)

    return dict(ar=ab_re.reshape(1, S5_WIDTH), ai=ab_im.reshape(1, S5_WIDTH),
                bre=in_blocks(bb_re), bim=in_blocks(bb_im),
                cre=out_blocks(p['s5_c_re'][m]), cim=out_blocks(p['s5_c_im'][m]),
                dv=p['s5_d'][m][None], wv=p['s5_wv'][m].astype(BF16), wg=p['s5_wg'][m].astype(BF16))


def _s5_seq(x, h0r, h0i, w, g, b, n_seq, seq_len, row_off, tt):
    nt = seq_len // tt
    off = row_off // tt
    consts = [w["ar"], w["ai"], w["bre"], w["bim"], w["cre"], w["cim"], w["dv"], w["wv"], w["wg"], g, b]
    st_spec = pl.BlockSpec((n_seq, 1, S5_WIDTH), lambda t: (0, 0, 0))
    x_spec = lambda s: pl.BlockSpec((tt, D_MODEL), lambda t: (off + s * nt + t, 0))
    o, hr, hi = pl.pallas_call(
        functools.partial(_s5_seq_body, tt=tt, group=n_seq), grid=(nt,),
        in_specs=[x_spec(s) for s in range(n_seq)] + [st_spec, st_spec] + [_const_spec(a, 1) for a in consts],
        out_specs=[pl.BlockSpec((n_seq, tt, D_MODEL), lambda t: (0, t, 0)), st_spec, st_spec],
        out_shape=[jax.ShapeDtypeStruct((n_seq, seq_len, D_MODEL), F32)]
        + [jax.ShapeDtypeStruct((n_seq, 1, S5_WIDTH), F32)] * 2,
        scratch_shapes=[pltpu.VMEM((n_seq, tt, S5_WIDTH), F32), pltpu.VMEM((n_seq, tt, S5_WIDTH), F32),
                        pltpu.VMEM((n_seq, 1, S5_WIDTH), F32), pltpu.VMEM((n_seq, 1, S5_WIDTH), F32)],
        compiler_params=_params("arbitrary"), name=f"s5_seq{tt}")(*([x] * n_seq), h0r, h0i, *consts)
    return o.reshape(n_seq * seq_len, D_MODEL), hr, hi


def _s5_layer(x, lay, m, p, re_s, im_s, g, b):
    n_seq, seq, n_dec, t_dec = lay["B"], lay["S"], lay["Bd"], lay["Td"]
    soff, moff = lay["soff"], lay["moff"]
    n_rows = n_dec * t_dec
    w = _s5_weights(m, p)
    zero = jnp.zeros((n_seq, 1, S5_WIDTH), F32)
    o_meta, hr, hi = _s5_seq(x, zero, zero, w, g, b, n_seq, N_META, moff, N_META)
    o_real, hr, hi = _s5_seq(x, hr, hi, w, g, b, n_seq, seq, 0, min(S5_TILE, seq))
    xs = x[soff:soff + n_rows].reshape(n_dec, t_dec, D_MODEL).transpose(1, 0, 2)
    consts = [w["ar"], w["ai"], w["bre"], w["bim"], w["cre"], w["cim"], w["dv"], w["wv"], w["wg"], g, b]
    st_spec = pl.BlockSpec((n_dec, S5_WIDTH), lambda t: (0, 0))
    o_s, sr, si = pl.pallas_call(
        _s5_step_body, grid=(t_dec,),
        in_specs=[pl.BlockSpec((1, n_dec, D_MODEL), lambda t: (t, 0, 0)), st_spec, st_spec]
        + [_const_spec(a, 1) for a in consts],
        out_specs=[pl.BlockSpec((1, n_dec, D_MODEL), lambda t: (t, 0, 0)), st_spec, st_spec],
        out_shape=[jax.ShapeDtypeStruct((t_dec, n_dec, D_MODEL), F32)]
        + [jax.ShapeDtypeStruct((n_dec, S5_WIDTH), F32)] * 2,
        scratch_shapes=[pltpu.VMEM((n_dec, S5_WIDTH), F32), pltpu.VMEM((n_dec, S5_WIDTH), F32)],
        compiler_params=_params("arbitrary"), name="s5_step")(
            xs, re_s[m].reshape(n_dec, S5_WIDTH), im_s[m].reshape(n_dec, S5_WIDTH), *consts)
    o_s = o_s.transpose(1, 0, 2).reshape(n_rows, D_MODEL)
    x_new = _assemble(lay, o_real, o_s, o_meta)
    shp = (S5_GROUPS, S5_STATE)
    return x_new, hr.reshape(n_seq, *shp), hi.reshape(n_seq, *shp), sr.reshape(n_dec, *shp), si.reshape(n_dec, *shp)


def _round_up(n, m):
    return (n + m - 1) // m * m


def kernel(x_prompt, x_sample, cache_mla_latent, cache_mla_krope, state_rwkv_wkv, state_rwkv_shift, state_s5_re, state_s5_im, page_table, meta_tokens, ln_g, ln_b, ffn_w1, ffn_w3, ffn_w2, mla_w_dq, mla_q_norm, mla_w_uq, mla_w_dkv, mla_kv_norm, mla_w_uk, mla_w_uv, mla_w_o, rw_mu, rw_wr, rw_wk, rw_wv, rw_w0, rw_w1, rw_w2, rw_a0, rw_a1, rw_a2, rw_g1, rw_g2, rw_k_k, rw_k_a, rw_r_k, rw_lnx_g, rw_lnx_b, rw_wo, s5_lam_re, s5_lam_im, s5_log_dt, s5_b_re, s5_b_im, s5_c_re, s5_c_im, s5_d, s5_wv, s5_wg):
    p = dict(rw_mu=rw_mu, rw_wr=rw_wr, rw_wk=rw_wk, rw_wv=rw_wv, rw_w0=rw_w0, rw_w1=rw_w1, rw_w2=rw_w2,
             rw_a0=rw_a0, rw_a1=rw_a1, rw_a2=rw_a2, rw_g1=rw_g1, rw_g2=rw_g2, rw_k_k=rw_k_k, rw_k_a=rw_k_a,
             rw_r_k=rw_r_k, rw_lnx_g=rw_lnx_g, rw_lnx_b=rw_lnx_b, rw_wo=rw_wo,
             s5_lam_re=s5_lam_re, s5_lam_im=s5_lam_im, s5_log_dt=s5_log_dt, s5_b_re=s5_b_re, s5_b_im=s5_b_im,
             s5_c_re=s5_c_re, s5_c_im=s5_c_im, s5_d=s5_d, s5_wv=s5_wv, s5_wg=s5_wg)
    n_seq, seq, _ = x_prompt.shape
    n_dec, t_dec, _ = x_sample.shape
    past = page_table.shape[1] * cache_mla_latent.shape[2]
    n_rows = n_dec * t_dec
    soff = _round_up(n_seq * seq, ROW_ALIGN)
    moff = _round_up(soff + n_rows, ROW_ALIGN)
    n_tot = _round_up(moff + n_seq * N_META, ROW_ALIGN)
    lay = dict(B=n_seq, S=seq, Bd=n_dec, Td=t_dec, soff=soff, moff=moff, n_tot=n_tot)
    assemble = functools.partial(_assemble, lay)

    meta = jnp.broadcast_to(meta_tokens[None], (n_seq, N_META, D_MODEL)).reshape(n_seq * N_META, D_MODEL)
    x = assemble(x_prompt.reshape(n_seq * seq, D_MODEL), x_sample.reshape(n_rows, D_MODEL), meta)
    pos = assemble(jnp.tile(N_META + jnp.arange(seq, dtype=jnp.int32), n_seq)[:, None],
                   jnp.tile(past + jnp.arange(t_dec, dtype=jnp.int32), n_dec)[:, None],
                   jnp.tile(jnp.arange(N_META, dtype=jnp.int32), n_seq)[:, None])[:, 0]
    tabs = _rope_tables(pos)

    stacked = lambda a: a.reshape((-1,) + a.shape[2:])
    w1_all, w3_all, w2_all = (stacked(a).astype(BF16) for a in (ffn_w1, ffn_w3, ffn_w2))
    g_all, b_all = (a.reshape(-1, 1, D_MODEL) for a in (ln_g, ln_b))

    lat_p, kr_p, lat_s, kr_s = [], [], [], []
    wkv_p = sh_p = wkv_s = sh_s = re_p = im_p = re_s = im_s = None
    stacks = (w1_all, w3_all, w2_all, g_all, b_all)
    ffn_a = lambda i: (2 * i, 3 * i)
    ffn_b = lambda i: (2 * i + 1, 3 * i + 2)
    x = _chain(x, stacks, [ffn_a(0)])
    for i in range(DEPTH):
        kind, m = i % 3, i // 3
        norm = lambda j: (ln_g[i, j][None], ln_b[i, j][None])
        tail = [ffn_b(i)] + ([ffn_a(i + 1)] if i + 1 < DEPTH else [])
        mix = None
        if kind == 0:
            w = _mla_weights(m, mla_w_dq, mla_q_norm, mla_w_uq, mla_w_dkv, mla_kv_norm, mla_w_uk, mla_w_uv, mla_w_o)
            o_all, c, kpe = _mla_layer(x, lay, tabs, w, m, cache_mla_latent, cache_mla_krope, page_table)
            mix = (_proj_mix, [o_all], [w["wo"], (g_all, 3 * i + 1), (b_all, 3 * i + 1)])
            split = lambda a: (jnp.concatenate(
                [a[moff:moff + n_seq * N_META].reshape(n_seq, N_META, -1), a[:n_seq * seq].reshape(n_seq, seq, -1)], 1),
                a[soff:soff + n_rows].reshape(n_dec, t_dec, -1))
            cp, cs = split(c)
            kp, ks = split(kpe)
            lat_p.append(cp), lat_s.append(cs), kr_p.append(kp), kr_s.append(ks)
        elif kind == 1:
            mix, wkv_p, sh_p, wkv_s, sh_s = _rwkv_layer(x, lay, m, p, state_rwkv_shift, state_rwkv_wkv, *norm(1))
        else:
            x, re_p, im_p, re_s, im_s = _s5_layer(x, lay, m, p, state_s5_re, state_s5_im, *norm(1))
        x = _chain(x, stacks, tail, mix)

    y_prompt = x[:n_seq * seq].reshape(n_seq, seq, D_MODEL)
    y_sample = x[soff:soff + n_rows].reshape(n_dec, t_dec, D_MODEL)
    return (y_prompt, y_sample, jnp.stack(lat_p), jnp.stack(kr_p), jnp.stack(lat_s), jnp.stack(kr_s),
            wkv_p[None], sh_p[None], wkv_s[None], sh_s[None], re_p[None], im_p[None], re_s[None], im_s[None])
```
